```python
import math
import jax
import jax.numpy as jnp
from jax import lax
import numpy as np

D_MODEL = 1024
BATCH = 4
SEQ = 8192
DEPTH = 2

GRID_W = 64
CTX_LEN = 256
NORM_EPS = 1e-6

RWKV_HEADS = 8
RWKV_HEAD_DIM = 64
RWKV_WIDTH = RWKV_HEADS * RWKV_HEAD_DIM
W_LORA = 64
A_LORA = 64
G_LORA = 128
RWKV_IN = 3 * RWKV_WIDTH + W_LORA + A_LORA + G_LORA
DECAY_SCALE = math.exp(-0.5)
GN_EPS = 64e-5
SHORT_CONV = 3

S5_GROUP = 16
S5_GROUPS = 32
S5_WIDTH = S5_GROUP * S5_GROUPS
S5_STATE = 64
DT_MIN = 1e-3
DT_MAX = 1e-1

HYB_IN = RWKV_IN + S5_WIDTH
HYB_OUT = RWKV_WIDTH + S5_WIDTH

MLA_HEADS = 16
Q_LORA = 256
KV_LORA = 128
QK_NOPE = 64
QK_ROPE = 32
V_DIM = 64
MLA_IN = Q_LORA + KV_LORA + QK_ROPE
MLA_OUT = MLA_HEADS * V_DIM
MLA_SCALE = (QK_NOPE + QK_ROPE) ** -0.5
ROPE_AXIS_DIMS = QK_ROPE // 2
ROPE_BASE = 10000.0
Q_BLOCK = 128

N_EXPERTS = 16
N_GROUPS = 4
EXPERTS_PER_GROUP = N_EXPERTS // N_GROUPS
TOP_K = 2
D_EXPERT = 256

kernel_name = 'hybrid_rwkv7_s5_mla_moe_diffusion_block'


def rmsnorm(x, g):
    xf = x.astype(jnp.float32)
    y = xf * lax.rsqrt(jnp.mean(jnp.square(xf), -1, keepdims=True) + NORM_EPS)
    return (y * g.astype(jnp.float32)).astype(x.dtype)


def modulate(h, shift, scale):
    return h * (1.0 + scale) + shift


def swiglu(t, wg, wu, wd):
    return (jax.nn.silu(t @ wg) * (t @ wu)) @ wd


def centred_depthwise_conv(x, w):
    return lax.conv_general_dilated(x, w[:, None, :], window_strides=(1,), padding='SAME',
                                    dimension_numbers=('NWC', 'WIO', 'NWC'),
                                    feature_group_count=x.shape[-1])


def rwkv_inputs(p, conv_w, w0, w_up, a0, a_up, g_up, k_k, k_a):
    p = centred_depthwise_conv(p, conv_w).astype(jnp.float32)
    r, k, v, wd, ad, gd = jnp.split(p, [RWKV_WIDTH, 2 * RWKV_WIDTH, 3 * RWKV_WIDTH,
                                        3 * RWKV_WIDTH + W_LORA, 3 * RWKV_WIDTH + W_LORA + A_LORA], axis=-1)
    heads = lambda t: t.reshape(t.shape[:-1] + (RWKV_HEADS, RWKV_HEAD_DIM))
    w = jnp.exp(-DECAY_SCALE * jax.nn.sigmoid(w0[:, None, None, :] + jnp.einsum('blr,zrc->zblc', jnp.tanh(wd), w_up)))
    a = jax.nn.sigmoid(a0[:, None, None, :] + jnp.einsum('blr,zrc->zblc', ad, a_up))
    g = jax.nn.sigmoid(gd) @ g_up
    kk = heads(k * k_k)
    kk = kk * lax.rsqrt(jnp.sum(jnp.square(kk), -1, keepdims=True) + 1e-12)
    k_dir = k * (1.0 + (a - 1.0) * k_a)
    return heads(r), heads(w), heads(k_dir), heads(v), kk, heads(a), g, heads(k)


def _rwkv_step(S, inp):
    r, w, k, v, kk, a = inp
    sa = jnp.einsum('zbhvk,zbhk->zbhv', S, kk)
    S = S * w[..., None, :] - sa[..., None] * (a * kk)[..., None, :] + v[..., None] * k[..., None, :]
    return S, jnp.einsum('zbhvk,zbhk->zbhv', S, r)


def rwkv_bidir(r, w, k, v, kk, a, s0):
    both = lambda t: jnp.stack([t, jnp.flip(t, 1)])
    per_dir = lambda t: jnp.stack([t[0], jnp.flip(t[1], 1)])
    xs = (both(r), per_dir(w), per_dir(k), both(v), both(kk), per_dir(a))
    xs = tuple(jnp.moveaxis(t, 2, 0) for t in xs)
    s_fin, ys = lax.scan(_rwkv_step, s0, xs)
    y = ys[:, 0] + jnp.flip(ys[:, 1], 0)
    return jnp.moveaxis(y, 0, 1), s_fin


def rwkv_output(y, rin, r_k, ln_w, ln_b):
    r, _, _, v, _, _, g, k = rin
    mu = jnp.mean(y, -1, keepdims=True)
    var = jnp.mean(jnp.square(y - mu), -1, keepdims=True)
    yn = ((y - mu) * lax.rsqrt(var + GN_EPS)).reshape(y.shape[:2] + (RWKV_WIDTH,)) * ln_w + ln_b
    bonus = (jnp.sum(r * k * r_k, -1, keepdims=True) * v).reshape(yn.shape)
    return (yn + bonus) * g


def s5_discretise(lam_re, lam_im, log_dt, b_re, b_im, c_re, c_im):
    f32 = jnp.float32
    lam = lax.complex(lam_re.astype(f32), lam_im.astype(f32))
    dt = jnp.exp(log_dt.astype(f32))[..., None]
    lam_bar = jnp.exp(lam * dt)
    b = lax.complex(b_re.astype(f32), b_im.astype(f32))
    b_bar = ((lam_bar - 1.0) / lam)[..., None] * b
    cmat = lax.complex(c_re.astype(f32), c_im.astype(f32))
    return lam_bar, b_bar, cmat


def _linear_combine(e1, e2):
    a1, b1 = e1
    a2, b2 = e2
    return a1 * a2, a2 * b1 + b2


def s5_scan(u, lam_bar, b_bar, h0):
    bu = jnp.einsum('lbgc,gpc->lbgp', u.astype(jnp.complex64), b_bar)
    a = jnp.broadcast_to(lam_bar, (u.shape[0], 1) + lam_bar.shape)
    cum_a, h = lax.associative_scan(_linear_combine, (a, bu), axis=0)
    return h + cum_a * h0[None]


def s5_bidir(u, lam_bar, b_bar, cmat, d, h0_f, h0_b):
    bsz, n = u.shape[:2]
    ut = jnp.transpose(u.astype(jnp.float32).reshape(bsz, n, S5_GROUPS, S5_GROUP), (1, 0, 2, 3))
    hf = s5_scan(ut, lam_bar[0], b_bar[0], h0_f)
    hb = s5_scan(jnp.flip(ut, 0), lam_bar[1], b_bar[1], h0_b)
    y = (jnp.real(jnp.einsum('lbgp,gcp->lbgc', hf, cmat[0]))
         + jnp.flip(jnp.real(jnp.einsum('lbgp,gcp->lbgc', hb, cmat[1])), 0)
         + d.astype(jnp.float32).reshape(S5_GROUPS, S5_GROUP) * ut)
    return jnp.transpose(y, (1, 0, 2, 3)), hf[-1], hb[-1]


def s5_glu(y, glu_w, glu_b):
    z = jax.nn.gelu(y)
    gate = jax.nn.sigmoid(jnp.einsum('blgc,gce->blge', z, glu_w) + glu_b.reshape(S5_GROUPS, S5_GROUP))
    return (z * gate).reshape(y.shape[:2] + (S5_WIDTH,))


def rwkv_s5_mixer(hc, hl, w_in, w_out, conv_w, w0, w_up, a0, a_up, g_up, k_k, k_a, r_k, ln_w, ln_b,
                  lam_re, lam_im, log_dt, b_re, b_im, c_re, c_im, d, glu_w, glu_b, with_ctx_out):
    bsz = hl.shape[0]
    pc = hc @ w_in
    pl = hl @ w_in
    rc = rwkv_inputs(pc[..., :RWKV_IN], conv_w, w0, w_up, a0, a_up, g_up, k_k, k_a)
    rl = rwkv_inputs(pl[..., :RWKV_IN], conv_w, w0, w_up, a0, a_up, g_up, k_k, k_a)
    s0 = jnp.zeros((2, bsz, RWKV_HEADS, RWKV_HEAD_DIM, RWKV_HEAD_DIM), jnp.float32)
    yc, s_ctx = rwkv_bidir(*rc[:6], s0)
    yl, _ = rwkv_bidir(*rl[:6], s_ctx)
    lam_bar, b_bar, cmat = s5_discretise(lam_re, lam_im, log_dt, b_re, b_im, c_re, c_im)
    h0 = jnp.zeros((bsz, S5_GROUPS, S5_STATE), jnp.complex64)
    zc, hf_c, hb_c = s5_bidir(pc[..., RWKV_IN:], lam_bar, b_bar, cmat, d, h0, h0)
    zl, _, _ = s5_bidir(pl[..., RWKV_IN:], lam_bar, b_bar, cmat, d, hf_c, hb_c)

    def merge(y, rin, z, dtype):
        cat = jnp.concatenate([rwkv_output(y, rin, r_k, ln_w, ln_b), s5_glu(z, glu_w, glu_b)], -1)
        return cat.astype(dtype) @ w_out

    o_l = merge(yl, rl, zl, hl.dtype)
    o_c = merge(yc, rc, zc, hc.dtype) if with_ctx_out else None
    return o_c, o_l


def axial_rope_tables(n_tokens):
    rows = n_tokens // GRID_W
    row = jnp.repeat(jnp.arange(rows, dtype=jnp.float32), GRID_W)
    col = jnp.tile(jnp.arange(GRID_W, dtype=jnp.float32), rows)
    inv_freq = ROPE_BASE ** (-jnp.arange(0, ROPE_AXIS_DIMS, 2, dtype=jnp.float32) / ROPE_AXIS_DIMS)
    ang = jnp.concatenate([row[:, None] * inv_freq, col[:, None] * inv_freq], -1)
    return jnp.cos(ang), jnp.sin(ang)


def apply_axial_rope(x, cos, sin):
    half = QK_ROPE // 2
    x1 = x[..., :half].astype(jnp.float32)
    x2 = x[..., half:].astype(jnp.float32)
    c, s = cos[:, None, :], sin[:, None, :]
    return jnp.concatenate([x1 * c - x2 * s, x1 * s + x2 * c], -1).astype(x.dtype)


def mla_q(q_down, q_norm, q_up, rope):
    q = (rmsnorm(q_down, q_norm) @ q_up).reshape(q_down.shape[:2] + (MLA_HEADS, QK_NOPE + QK_ROPE))
    q_nope, q_pe = q[..., :QK_NOPE], q[..., QK_NOPE:]
    if rope is not None:
        q_pe = apply_axial_rope(q_pe, *rope)
    return jnp.concatenate([q_nope, q_pe], -1)


def mla_kv(kv_down, k_pe, kv_norm, kv_up, rope):
    kv = (rmsnorm(kv_down, kv_norm) @ kv_up).reshape(kv_down.shape[:2] + (MLA_HEADS, QK_NOPE + V_DIM))
    k_nope, v = kv[..., :QK_NOPE], kv[..., QK_NOPE:]
    k_pe = k_pe[:, :, None, :]
    if rope is not None:
        k_pe = apply_axial_rope(k_pe, *rope)
    k = jnp.concatenate([k_nope, jnp.broadcast_to(k_pe, k_nope.shape[:-1] + (QK_ROPE,))], -1)
    return k, v


def latent_attention(q, k_lat, v_lat, k_ctx, v_ctx):
    k_all = jnp.concatenate([k_ctx, k_lat], 1)
    v_all = jnp.concatenate([v_ctx, v_lat], 1)
    bsz, n, h, dq = q.shape
    qb = jnp.moveaxis(q.reshape(bsz, n // Q_BLOCK, Q_BLOCK, h, dq), 1, 0)

    def block(qblk):
        s = jnp.einsum('bqhd,bkhd->bhqk', qblk, k_all, preferred_element_type=jnp.float32) * MLA_SCALE
        p = jax.nn.softmax(s, axis=-1).astype(v_all.dtype)
        return jnp.einsum('bhqk,bkhd->bqhd', p, v_all)

    o = lax.map(block, qb)
    return jnp.moveaxis(o, 0, 1).reshape(bsz, n, h * V_DIM)


def context_attention(q, k, v):
    s = jnp.einsum('bqhd,bkhd->bhqk', q, k, preferred_element_type=jnp.float32) * MLA_SCALE
    p = jax.nn.softmax(s, axis=-1).astype(v.dtype)
    o = jnp.einsum('bhqk,bkhd->bqhd', p, v)
    return o.reshape(q.shape[:2] + (MLA_HEADS * V_DIM,))


def mla_mixer(hc, hl, w_in, q_norm, q_up, kv_norm, kv_up, w_out, with_ctx_out):
    rope = axial_rope_tables(hl.shape[1])
    pl = hl @ w_in
    ql = mla_q(pl[..., :Q_LORA], q_norm, q_up, rope)
    kl, vl = mla_kv(pl[..., Q_LORA:Q_LORA + KV_LORA], pl[..., Q_LORA + KV_LORA:], kv_norm, kv_up, rope)
    pc = hc @ w_in[:, Q_LORA:]
    kc, vc = mla_kv(pc[..., :KV_LORA], pc[..., KV_LORA:], kv_norm, kv_up, None)
    o_l = latent_attention(ql, kl, vl, kc, vc) @ w_out
    o_c = None
    if with_ctx_out:
        qc = mla_q(hc @ w_in[:, :Q_LORA], q_norm, q_up, None)
        o_c = context_attention(qc, kc, vc) @ w_out
    return o_c, o_l


def moe_ffn(h, router_w, router_b, ex_gate, ex_up, ex_down, sh_gate, sh_up, sh_down):
    shp = h.shape
    t = h.reshape(-1, shp[-1])
    scores = jax.nn.sigmoid(jnp.einsum('td,de->te', t, router_w, preferred_element_type=jnp.float32))
    biased = (scores + router_b.astype(jnp.float32)).reshape(-1, N_GROUPS, EXPERTS_PER_GROUP)
    group_score = jnp.sum(lax.top_k(biased, 2)[0], -1)
    g_sel = jnp.argmax(group_score, -1)
    in_group = jnp.take_along_axis(biased, g_sel[:, None, None], axis=1)[:, 0]
    _, local = lax.top_k(in_group, TOP_K)
    idx = g_sel[:, None] * EXPERTS_PER_GROUP + local
    w_sel = jnp.take_along_axis(scores, idx, axis=1)
    w_sel = w_sel / jnp.sum(w_sel, -1, keepdims=True)
    combine = jnp.einsum('tke,tk->te', jax.nn.one_hot(idx, N_EXPERTS, dtype=jnp.float32), w_sel).astype(t.dtype)
    out = swiglu(t, sh_gate, sh_up, sh_down)
    for e in range(N_EXPERTS):
        out = out + combine[:, e:e + 1] * swiglu(t, ex_gate[e], ex_up[e], ex_down[e])
    return out.reshape(shp)


def setup_inputs(seed: int = 0) -> dict:
    key = jax.random.key(seed)
    keys = iter(jax.random.split(key, 64))
    f32 = jnp.float32

    def nrm(shape, scale):
        return scale * jax.random.normal(next(keys), shape, f32)

    D = D_MODEL
    NE = (DEPTH + 1) // 2
    NO = DEPTH // 2
    G, C, P = S5_GROUPS, S5_GROUP, S5_STATE
    E, F = N_EXPERTS, D_EXPERT
    return {
        'x': nrm((BATCH, SEQ, D), 1.0),
        'c': nrm((BATCH, D), 1.0),
        'ctx': nrm((BATCH, CTX_LEN, D), 1.0),
        'c_ctx': nrm((D,), 1.0),
        'mod_w': nrm((DEPTH, D, 6 * D), 0.5 * D ** -0.5),
        'mod_b': nrm((DEPTH, 6 * D), 0.02),
        'norm1_g': 1.0 + nrm((DEPTH, D), 0.02),
        'norm2_g': 1.0 + nrm((DEPTH, D), 0.02),
        'final_g': 1.0 + nrm((D,), 0.02),
        'hy_w_in': nrm((NE, D, HYB_IN), D ** -0.5),
        'hy_w_out': nrm((NE, HYB_OUT, D), HYB_OUT ** -0.5),
        'rk_conv': jnp.array([0.25, 0.5, 0.25], f32)[None, :, None] + nrm((NE, SHORT_CONV, RWKV_IN), 0.05),
        'rk_w0': nrm((NE, 2, RWKV_WIDTH), 1.0) - 0.5,
        'rk_w_up': nrm((NE, 2, W_LORA, RWKV_WIDTH), W_LORA ** -0.5),
        'rk_a0': nrm((NE, 2, RWKV_WIDTH), 0.5),
        'rk_a_up': nrm((NE, 2, A_LORA, RWKV_WIDTH), 0.5 * A_LORA ** -0.5),
        'rk_g_up': nrm((NE, G_LORA, RWKV_WIDTH), G_LORA ** -0.5),
        'rk_k_k': 0.85 + nrm((NE, RWKV_WIDTH), 0.05),
        'rk_k_a': 1.0 + nrm((NE, RWKV_WIDTH), 0.05),
        'rk_r_k': nrm((NE, RWKV_HEADS, RWKV_HEAD_DIM), 0.1),
        'rk_ln_w': 1.0 + nrm((NE, RWKV_WIDTH), 0.02),
        'rk_ln_b': nrm((NE, RWKV_WIDTH), 0.02),
        's5_lam_re': -0.5 + nrm((NE, 2, G, P), 0.02),
        's5_lam_im': jnp.broadcast_to(jnp.pi * jnp.arange(P, dtype=f32), (NE, 2, G, P)),
        's5_log_dt': jax.random.uniform(next(keys), (NE, 2, G), f32, math.log(DT_MIN), math.log(DT_MAX)),
        's5_b_re': nrm((NE, 2, G, P, C), (2 * C) ** -0.5),
        's5_b_im': nrm((NE, 2, G, P, C), (2 * C) ** -0.5),
        's5_c_re': nrm((NE, 2, G, C, P), P ** -0.5),
        's5_c_im': nrm((NE, 2, G, C, P), P ** -0.5),
        's5_d': nrm((NE, S5_WIDTH), 1.0),
        's5_glu_w': nrm((NE, G, C, C), C ** -0.5),
        's5_glu_b': nrm((NE, S5_WIDTH), 0.02),
        'mla_w_in': nrm((NO, D, MLA_IN), D ** -0.5),
        'mla_q_norm': 1.0 + nrm((NO, Q_LORA), 0.02),
        'mla_q_up': nrm((NO, Q_LORA, MLA_HEADS * (QK_NOPE + QK_ROPE)), Q_LORA ** -0.5),
        'mla_kv_norm': 1.0 + nrm((NO, KV_LORA), 0.02),
        'mla_kv_up': nrm((NO, KV_LORA, MLA_HEADS * (QK_NOPE + V_DIM)), KV_LORA ** -0.5),
        'mla_w_out': nrm((NO, MLA_OUT, D), MLA_OUT ** -0.5),
        'router_w': nrm((D, E), D ** -0.5),
        'router_b': nrm((E,), 0.01),
        'ex_gate': nrm((DEPTH, E, D, F), D ** -0.5),
        'ex_up': nrm((DEPTH, E, D, F), D ** -0.5),
        'ex_down': nrm((DEPTH, E, F, D), F ** -0.5),
        'sh_gate': nrm((DEPTH, D, F), D ** -0.5),
        'sh_up': nrm((DEPTH, D, F), D ** -0.5),
        'sh_down': nrm((DEPTH, F, D), F ** -0.5),
    }


def reference(x, c, ctx, c_ctx, mod_w, mod_b, norm1_g, norm2_g, final_g,
              hy_w_in, hy_w_out, rk_conv, rk_w0, rk_w_up, rk_a0, rk_a_up, rk_g_up, rk_k_k, rk_k_a, rk_r_k,
              rk_ln_w, rk_ln_b, s5_lam_re, s5_lam_im, s5_log_dt, s5_b_re, s5_b_im, s5_c_re, s5_c_im, s5_d,
              s5_glu_w, s5_glu_b, mla_w_in, mla_q_norm, mla_q_up, mla_kv_norm, mla_kv_up, mla_w_out,
              router_w, router_b, ex_gate, ex_up, ex_down, sh_gate, sh_up, sh_down):
    h_lat, h_ctx = x, ctx
    silu_c = jax.nn.silu(c)
    silu_cc = jax.nn.silu(c_ctx)
    for layer in range(DEPTH):
        last = layer == DEPTH - 1
        i = layer // 2
        m_l = jnp.split((silu_c @ mod_w[layer] + mod_b[layer])[:, None, :], 6, axis=-1)
        m_c = jnp.split(silu_cc @ mod_w[layer] + mod_b[layer], 6, axis=-1)
        a_l = modulate(rmsnorm(h_lat, norm1_g[layer]), m_l[0], m_l[1])
        a_c = modulate(rmsnorm(h_ctx, norm1_g[layer]), m_c[0], m_c[1])
        if layer % 2 == 0:
            o_c, o_l = rwkv_s5_mixer(a_c, a_l, hy_w_in[i], hy_w_out[i], rk_conv[i], rk_w0[i], rk_w_up[i],
                                     rk_a0[i], rk_a_up[i], rk_g_up[i], rk_k_k[i], rk_k_a[i], rk_r_k[i],
                                     rk_ln_w[i], rk_ln_b[i], s5_lam_re[i], s5_lam_im[i], s5_log_dt[i],
                                     s5_b_re[i], s5_b_im[i], s5_c_re[i], s5_c_im[i], s5_d[i],
                                     s5_glu_w[i], s5_glu_b[i], not last)
        else:
            o_c, o_l = mla_mixer(a_c, a_l, mla_w_in[i], mla_q_norm[i], mla_q_up[i], mla_kv_norm[i],
                                 mla_kv_up[i], mla_w_out[i], not last)
        h_lat = h_lat + m_l[2] * o_l
        f_l = modulate(rmsnorm(h_lat, norm2_g[layer]), m_l[3], m_l[4])
        if last:
            h_lat = h_lat + m_l[5] * moe_ffn(f_l, router_w, router_b, ex_gate[layer], ex_up[layer],
                                             ex_down[layer], sh_gate[layer], sh_up[layer], sh_down[layer])
        else:
            h_ctx = h_ctx + m_c[2] * o_c
            f_c = modulate(rmsnorm(h_ctx, norm2_g[layer]), m_c[3], m_c[4])
            n_lat = f_l.shape[1]
            f = moe_ffn(jnp.concatenate([f_l, f_c], 1), router_w, router_b, ex_gate[layer], ex_up[layer],
                        ex_down[layer], sh_gate[layer], sh_up[layer], sh_down[layer])
            h_lat = h_lat + m_l[5] * f[:, :n_lat]
            h_ctx = h_ctx + m_c[5] * f[:, n_lat:]
    return rmsnorm(h_lat, final_g)
```

```python
import functools
import math

import jax
import jax.numpy as jnp
from jax import lax
from jax.experimental import pallas as pl
from jax.experimental.pallas import tpu as pltpu

F32 = jnp.float32
BF16 = jnp.bfloat16
HI = lax.Precision.HIGHEST

D_MODEL = 1024
GRID_W = 64
NORM_EPS = 1e-6

RWKV_HEADS = 8
RWKV_HEAD_DIM = 64
RWKV_WIDTH = 512
W_LORA = 64
A_LORA = 64
G_LORA = 128
RWKV_IN = 3 * RWKV_WIDTH + W_LORA + A_LORA + G_LORA
DECAY_SCALE = math.exp(-0.5)
GN_EPS = 64e-5

S5_GROUP = 16
S5_GROUPS = 32
S5_WIDTH = 512
S5_STATE = 64
HYB_IN = RWKV_IN + S5_WIDTH

MLA_HEADS = 16
Q_LORA = 256
KV_LORA = 128
QK_NOPE = 64
QK_ROPE = 32
V_DIM = 64
MLA_IN = Q_LORA + KV_LORA + QK_ROPE
MLA_SCALE = (QK_NOPE + QK_ROPE) ** -0.5
ROPE_AXIS_DIMS = QK_ROPE // 2
ROPE_BASE = 10000.0
HEAD_PAD = 128

N_EXPERTS = 16
N_GROUPS = 4
EXPERTS_PER_GROUP = 4
D_EXPERT = 256

ROW_BLK = 256
MOE_BLK = 256
ATT_TQ = 256
ATT_TK = 768
VMEM_LIMIT = 56 * 1024 * 1024


def _cparams(sem):
    return pltpu.CompilerParams(dimension_semantics=sem, vmem_limit_bytes=VMEM_LIMIT)


def _norm_mod(x, g, shift, scale):
    y = x * lax.rsqrt(jnp.mean(x * x, -1, keepdims=True) + NORM_EPS) * g
    return y * (1.0 + scale) + shift


def _sigmoid(x):
    return 1.0 / (1.0 + jnp.exp(-x))


def _adaln_kernel(s_ref, w_ref, b_ref, o_ref):
    s = s_ref[...]
    s = s * _sigmoid(s)
    o_ref[...] = jnp.dot(s, w_ref[...], precision=HI, preferred_element_type=F32) + b_ref[...]


def adaln(cc, w, b):
    n = w.shape[1]
    tn = 512
    return pl.pallas_call(
        _adaln_kernel,
        grid=(n // tn,),
        in_specs=[pl.BlockSpec((8, D_MODEL), lambda j: (0, 0)),
                  pl.BlockSpec((D_MODEL, tn), lambda j: (0, j)),
                  pl.BlockSpec((1, tn), lambda j: (0, j))],
        out_specs=pl.BlockSpec((8, tn), lambda j: (0, j)),
        out_shape=jax.ShapeDtypeStruct((8, n), F32),
        compiler_params=_cparams(("parallel",)),
        name="adaln",
    )(cc, w, b)


def _mod_spec(n_lat, blk=ROW_BLK):
    return pl.BlockSpec((1, 1, 6, D_MODEL), lambda b, i, *_: (b, i // (n_lat // blk), 0, 0))


def _inproj_kernel(h_ref, g_ref, mod_ref, w_ref, o_ref):
    m = mod_ref[0, 0]
    a = _norm_mod(h_ref[0], g_ref[...], m[0:1], m[1:2])
    o_ref[0] = jnp.dot(a.astype(BF16), w_ref[...], preferred_element_type=F32)


def inproj(h, n_lat, g, mod, w):
    bsz, lt, d = h.shape
    n = w.shape[1]
    return pl.pallas_call(
        _inproj_kernel,
        grid=(bsz, lt // ROW_BLK),
        in_specs=[pl.BlockSpec((1, ROW_BLK, d), lambda b, i: (b, i, 0)),
                  pl.BlockSpec((1, d), lambda b, i: (0, 0)),
                  _mod_spec(n_lat),
                  pl.BlockSpec((d, n), lambda b, i: (0, 0))],
        out_specs=pl.BlockSpec((1, ROW_BLK, n), lambda b, i: (b, i, 0)),
        out_shape=jax.ShapeDtypeStruct((bsz, lt, n), F32),
        compiler_params=_cparams(("parallel", "parallel")),
        name="inproj",
    )(h, g, mod, w)


def _outproj_kernel(a_ref, w_ref, h_ref, mod_ref, o_ref):
    m = mod_ref[0, 0]
    o = jnp.dot(a_ref[0].astype(BF16), w_ref[...], preferred_element_type=F32)
    o_ref[0] = h_ref[0] + m[2:3] * o


def outproj(a, w, h, n_lat, mod):
    bsz, la, k = a.shape
    d = h.shape[2]
    return pl.pallas_call(
        _outproj_kernel,
        grid=(bsz, la // ROW_BLK),
        in_specs=[pl.BlockSpec((1, ROW_BLK, k), lambda b, i: (b, i, 0)),
                  pl.BlockSpec((k, d), lambda b, i: (0, 0)),
                  pl.BlockSpec((1, ROW_BLK, d), lambda b, i: (b, i, 0)),
                  _mod_spec(n_lat)],
        out_specs=pl.BlockSpec((1, ROW_BLK, d), lambda b, i: (b, i, 0)),
        out_shape=jax.ShapeDtypeStruct((bsz, la, d), F32),
        compiler_params=_cparams(("parallel", "parallel")),
        name="outproj",
    )(a, w, h, mod)


def _route(scores, rb):
    t = scores.shape[0]
    biased = scores + rb
    col = [biased[:, e:e + 1] for e in range(N_EXPERTS)]
    sc = [scores[:, e:e + 1] for e in range(N_EXPERTS)]
    gscore = []
    for gi in range(N_GROUPS):
        a, b, c, d = col[4 * gi:4 * gi + 4]
        hi1, lo1 = jnp.maximum(a, b), jnp.minimum(a, b)
        hi2, lo2 = jnp.maximum(c, d), jnp.minimum(c, d)
        gscore.append(jnp.maximum(hi1, hi2) + jnp.maximum(jnp.minimum(hi1, hi2), jnp.maximum(lo1, lo2)))
    gsel = []
    taken = None
    for gi in range(N_GROUPS):
        best = None
        for gj in range(gi + 1, N_GROUPS):
            best = gscore[gj] if best is None else jnp.maximum(best, gscore[gj])
        s = (gscore[gi] >= best) if best is not None else jnp.full((t, 1), True)
        if taken is not None:
            s = jnp.logical_and(s, jnp.logical_not(taken))
        taken = s if taken is None else jnp.logical_or(taken, s)
        gsel.append(s)
    masks = []
    for gi in range(N_GROUPS):
        v = col[4 * gi:4 * gi + 4]
        for j in range(4):
            rank = jnp.zeros((t, 1), F32)
            for i in range(4):
                if i == j:
                    continue
                ahead = (v[i] >= v[j]) if i < j else (v[i] > v[j])
                rank = rank + ahead.astype(F32)
            masks.append(jnp.logical_and(gsel[gi], rank < 2.0))
    wsel = [jnp.where(masks[e], sc[e], 0.0) for e in range(N_EXPERTS)]
    denom = wsel[0]
    for e in range(1, N_EXPERTS):
        denom = denom + wsel[e]
    return [w / denom for w in wsel]


def _moe_kernel(h_ref, g_ref, mod_ref, rw_ref, rb_ref, wg_ref, wu_ref, wd_ref, fg_ref, o_ref,
                f_sc, comb_sc, acc_sc, *, final_norm):
    e = pl.program_id(2)
    n_e = pl.num_programs(2)
    tm = f_sc.shape[0]

    @pl.when(e == 0)
    def _():
        m = mod_ref[0, 0]
        f = _norm_mod(h_ref[0], g_ref[...], m[3:4], m[4:5])
        f_sc[...] = f.astype(BF16)
        logits = jnp.dot(f, rw_ref[...], precision=HI, preferred_element_type=F32)
        cw = _route(_sigmoid(logits), rb_ref[...])
        lane = lax.broadcasted_iota(jnp.int32, (tm, 128), 1)
        comb = jnp.where(lane == N_EXPERTS, 1.0, 0.0)
        for ei in range(N_EXPERTS):
            comb = jnp.where(lane == ei, cw[ei], comb)
        comb_sc[...] = comb
        acc_sc[...] = jnp.zeros_like(acc_sc)

    fb = f_sc[...]
    gt = jnp.dot(fb, wg_ref[0], preferred_element_type=F32)
    up = jnp.dot(fb, wu_ref[0], preferred_element_type=F32)
    lane = lax.broadcasted_iota(jnp.int32, (tm, 128), 1)
    cw = jnp.sum(jnp.where(lane == e, comb_sc[...], 0.0), axis=1, keepdims=True)
    act = gt * _sigmoid(gt) * up * cw
    acc_sc[...] += jnp.dot(act.astype(BF16), wd_ref[0], preferred_element_type=F32)

    @pl.when(e == n_e - 1)
    def _():
        m = mod_ref[0, 0]
        y = h_ref[0] + m[5:6] * acc_sc[...]
        if final_norm:
            y = y * lax.rsqrt(jnp.mean(y * y, -1, keepdims=True) + NORM_EPS) * fg_ref[...]
        o_ref[0] = y


def moe(h, n_lat, g, mod, rw, rb, wg, wu, wd, fg, final_norm):
    bsz, lo, d = h.shape
    ne = wg.shape[0]
    kern = functools.partial(_moe_kernel, final_norm=final_norm)
    return pl.pallas_call(
        kern,
        grid=(bsz, lo // MOE_BLK, ne),
        in_specs=[pl.BlockSpec((1, MOE_BLK, d), lambda b, i, e: (b, i, 0)),
                  pl.BlockSpec((1, d), lambda b, i, e: (0, 0)),
                  _mod_spec(n_lat, MOE_BLK),
                  pl.BlockSpec((d, N_EXPERTS), lambda b, i, e: (0, 0)),
                  pl.BlockSpec((1, N_EXPERTS), lambda b, i, e: (0, 0)),
                  pl.BlockSpec((1, d, D_EXPERT), lambda b, i, e: (e, 0, 0)),
                  pl.BlockSpec((1, d, D_EXPERT), lambda b, i, e: (e, 0, 0)),
                  pl.BlockSpec((1, D_EXPERT, d), lambda b, i, e: (e, 0, 0)),
                  pl.BlockSpec((1, d), lambda b, i, e: (0, 0))],
        out_specs=pl.BlockSpec((1, MOE_BLK, d), lambda b, i, e: (b, i, 0)),
        out_shape=jax.ShapeDtypeStruct((bsz, lo, d), F32),
        scratch_shapes=[pltpu.VMEM((MOE_BLK, d), BF16),
                        pltpu.VMEM((MOE_BLK, 128), F32),
                        pltpu.VMEM((MOE_BLK, d), F32)],
        compiler_params=_cparams(("parallel", "parallel", "arbitrary")),
        name="moe",
    )(h, g, mod, rw, rb, wg, wu, wd, fg)


def _rope(x, ct, s1, s2):
    n = x.shape[1]
    return x * ct + pltpu.roll(x, n - ROPE_AXIS_DIMS, 1) * s1 + pltpu.roll(x, ROPE_AXIS_DIMS, 1) * s2


def _mla_proj_kernel(h_ref, g_ref, mod_ref, win_ref, qn_ref, qup_ref, kvn_ref, kup_ref, vup_ref, epl_ref,
                     one_ref, ct_ref, s1_ref, s2_ref, q_ref, k_ref, v_ref):
    m = mod_ref[0, 0]
    a = _norm_mod(h_ref[0], g_ref[...], m[0:1], m[1:2])
    p = jnp.dot(a.astype(BF16), win_ref[...], preferred_element_type=F32)
    qd = p[:, :Q_LORA]
    kvd = p[:, Q_LORA:Q_LORA + KV_LORA]
    kpe = p[:, Q_LORA + KV_LORA:]
    qn = qd * lax.rsqrt(jnp.mean(qd * qd, -1, keepdims=True) + NORM_EPS) * qn_ref[...]
    kvn = kvd * lax.rsqrt(jnp.mean(kvd * kvd, -1, keepdims=True) + NORM_EPS) * kvn_ref[...]
    kvb = kvn.astype(BF16)
    ct = jnp.concatenate([ct_ref[...]] * MLA_HEADS, axis=1)
    s1 = jnp.concatenate([s1_ref[...]] * MLA_HEADS, axis=1)
    s2 = jnp.concatenate([s2_ref[...]] * MLA_HEADS, axis=1)
    q = jnp.dot(qn.astype(BF16), qup_ref[...], preferred_element_type=F32)
    q_ref[0] = (_rope(q, ct, s1, s2) * MLA_SCALE).astype(BF16)
    k = (jnp.dot(kvb, kup_ref[...], preferred_element_type=F32)
         + jnp.dot(kpe.astype(BF16), epl_ref[...], preferred_element_type=F32))
    k_ref[0] = _rope(k, ct, s1, s2).astype(BF16)
    v = jnp.dot(kvb, vup_ref[...], preferred_element_type=F32) + one_ref[...]
    v_ref[0] = v.astype(BF16)


def mla_proj(h, n_lat, g, mod, win, qn, qup, kvn, kup, vup, epl, one, ct, s1, s2):
    bsz, lt, d = h.shape
    hp = MLA_HEADS * HEAD_PAD
    full = lambda shape: pl.BlockSpec(shape, lambda b, i: (0,) * len(shape))
    tab = pl.BlockSpec((ROW_BLK, HEAD_PAD), lambda b, i: (i, 0))
    out = pl.BlockSpec((1, ROW_BLK, hp), lambda b, i: (b, i, 0))
    return pl.pallas_call(
        _mla_proj_kernel,
        grid=(bsz, lt // ROW_BLK),
        in_specs=[pl.BlockSpec((1, ROW_BLK, d), lambda b, i: (b, i, 0)),
                  full((1, d)), _mod_spec(n_lat), full((d, MLA_IN)), full((1, Q_LORA)), full((Q_LORA, hp)),
                  full((1, KV_LORA)), full((KV_LORA, hp)), full((KV_LORA, hp)), full((QK_ROPE, hp)),
                  full((1, hp)), tab, tab, tab],
        out_specs=[out, out, out],
        out_shape=[jax.ShapeDtypeStruct((bsz, lt, hp), BF16)] * 3,
        compiler_params=_cparams(("parallel", "parallel")),
        name="mla_proj",
    )(h, g, mod, win, qn, qup, kvn, kup, vup, epl, one, ct, s1, s2)


def _attn_kernel(q_ref, k_ref, v_ref, o_ref, *, tk, nk):
    q = q_ref[0]
    tq = q.shape[0]

    def body(j, carry):
        m, acc = carry
        start = pl.multiple_of(j * tk, tk)
        kc = k_ref[0, pl.ds(start, tk), :]
        vc = v_ref[0, pl.ds(start, tk), :]
        s = lax.dot_general(q, kc, (((1,), (1,)), ((), ())), preferred_element_type=F32)
        m_new = jnp.maximum(m, jnp.max(s, axis=1, keepdims=True))
        alpha = jnp.exp(m - m_new)
        p = jnp.exp(s - m_new).astype(BF16)
        acc = acc * alpha + jnp.dot(p, vc, preferred_element_type=F32)
        return m_new, acc

    m0 = jnp.full((tq, 1), -1e30, F32)
    acc0 = jnp.zeros((tq, HEAD_PAD), F32)
    _, acc = lax.fori_loop(0, nk, body, (m0, acc0))
    o_ref[0] = (acc / acc[:, V_DIM:V_DIM + 1]).astype(BF16)


def attention(q, k, v, lq):
    bsz, lt, hp = q.shape
    kern = functools.partial(_attn_kernel, tk=ATT_TK, nk=lt // ATT_TK)
    return pl.pallas_call(
        kern,
        grid=(bsz, MLA_HEADS, lq // ATT_TQ),
        in_specs=[pl.BlockSpec((1, ATT_TQ, HEAD_PAD), lambda b, h, i: (b, i, h)),
                  pl.BlockSpec((1, lt, HEAD_PAD), lambda b, h, i: (b, 0, h)),
                  pl.BlockSpec((1, lt, HEAD_PAD), lambda b, h, i: (b, 0, h))],
        out_specs=pl.BlockSpec((1, ATT_TQ, HEAD_PAD), lambda b, h, i: (b, i, h)),
        out_shape=jax.ShapeDtypeStruct((bsz, lq, hp), BF16),
        compiler_params=_cparams(("parallel", "parallel", "parallel")),
        name="attention",
    )(q, k, v)


def _rope_tables(n_lat, n_ctx):
    rows = n_lat // GRID_W
    row = jnp.repeat(jnp.arange(rows, dtype=F32), GRID_W)
    col = jnp.tile(jnp.arange(GRID_W, dtype=F32), rows)
    inv_freq = ROPE_BASE ** (-jnp.arange(0, ROPE_AXIS_DIMS, 2, dtype=F32) / ROPE_AXIS_DIMS)
    ang = jnp.concatenate([row[:, None] * inv_freq, col[:, None] * inv_freq], -1)
    cos = jnp.concatenate([jnp.cos(ang), jnp.ones((n_ctx, ROPE_AXIS_DIMS), F32)], 0)
    sin = jnp.concatenate([jnp.sin(ang), jnp.zeros((n_ctx, ROPE_AXIS_DIMS), F32)], 0)
    lt = n_lat + n_ctx
    one = jnp.ones((lt, QK_NOPE), F32)
    z16 = jnp.zeros((lt, ROPE_AXIS_DIMS), F32)
    z32 = jnp.zeros((lt, HEAD_PAD - QK_NOPE - QK_ROPE), F32)
    z64 = jnp.zeros((lt, QK_NOPE), F32)
    ct = jnp.concatenate([one, cos, cos, z32], 1)
    s1 = jnp.concatenate([z64, -sin, z16, z32], 1)
    s2 = jnp.concatenate([z64, z16, sin, z32], 1)
    return ct, s1, s2


def _pad_heads(w, width, offset=0):
    k = w.shape[0]
    w = w.reshape(k, MLA_HEADS, width)
    w = jnp.pad(w, ((0, 0), (0, 0), (offset, HEAD_PAD - width - offset)))
    return w.reshape(k, MLA_HEADS * HEAD_PAD)


def mla_layer(h, n_ctx, g, mod, w_in, q_norm, q_up, kv_norm, kv_up, w_out):
    bsz, lt, d = h.shape
    n_lat = lt - n_ctx
    ct, s1, s2 = _rope_tables(n_lat, n_ctx)
    qup = _pad_heads(q_up, QK_NOPE + QK_ROPE).astype(BF16)
    kvu = kv_up.reshape(KV_LORA, MLA_HEADS, QK_NOPE + V_DIM)
    kup = _pad_heads(kvu[:, :, :QK_NOPE].reshape(KV_LORA, -1), QK_NOPE).astype(BF16)
    vup = _pad_heads(kvu[:, :, QK_NOPE:].reshape(KV_LORA, -1), V_DIM).astype(BF16)
    epl = _pad_heads(jnp.tile(jnp.eye(QK_ROPE, dtype=F32), (1, MLA_HEADS)), QK_ROPE, QK_NOPE).astype(BF16)
    one = _pad_heads(jnp.ones((1, MLA_HEADS), F32), 1, V_DIM)
    q, k, v = mla_proj(h, n_lat, g, mod, w_in.astype(BF16), q_norm[None], qup, kv_norm[None], kup, vup, epl, one,
                       ct, s1, s2)
    o = attention(q, k, v, n_lat)
    wo = w_out.reshape(MLA_HEADS, V_DIM, d)
    wo = jnp.pad(wo, ((0, 0), (0, HEAD_PAD - V_DIM), (0, 0))).reshape(MLA_HEADS * HEAD_PAD, d).astype(BF16)
    return outproj(o, wo, h, n_lat, mod)


def _jx_conv(x, w):
    return lax.conv_general_dilated(x, w[:, None, :], window_strides=(1,), padding='SAME',
                                    dimension_numbers=('NWC', 'WIO', 'NWC'), feature_group_count=x.shape[-1])


def _jx_rwkv_inputs(p, conv_w, w0, w_up, a0, a_up, g_up, k_k, k_a):
    p = _jx_conv(p, conv_w).astype(F32)
    r, k, v, wd, ad, gd = jnp.split(p, [RWKV_WIDTH, 2 * RWKV_WIDTH, 3 * RWKV_WIDTH,
                                        3 * RWKV_WIDTH + W_LORA, 3 * RWKV_WIDTH + W_LORA + A_LORA], axis=-1)
    heads = lambda t: t.reshape(t.shape[:-1] + (RWKV_HEADS, RWKV_HEAD_DIM))
    w = jnp.exp(-DECAY_SCALE * jax.nn.sigmoid(w0[:, None, None, :] + jnp.einsum('blr,zrc->zblc', jnp.tanh(wd), w_up)))
    a = jax.nn.sigmoid(a0[:, None, None, :] + jnp.einsum('blr,zrc->zblc', ad, a_up))
    g = jax.nn.sigmoid(gd) @ g_up
    kk = heads(k * k_k)
    kk = kk * lax.rsqrt(jnp.sum(jnp.square(kk), -1, keepdims=True) + 1e-12)
    k_dir = k * (1.0 + (a - 1.0) * k_a)
    return heads(r), heads(w), heads(k_dir), heads(v), kk, heads(a), g, heads(k)


def _jx_rwkv_step(S, inp):
    r, w, k, v, kk, a = inp
    sa = jnp.einsum('zbhvk,zbhk->zbhv', S, kk)
    S = S * w[..., None, :] - sa[..., None] * (a * kk)[..., None, :] + v[..., None] * k[..., None, :]
    return S, jnp.einsum('zbhvk,zbhk->zbhv', S, r)


def _jx_rwkv_bidir(r, w, k, v, kk, a, s0):
    both = lambda t: jnp.stack([t, jnp.flip(t, 1)])
    per_dir = lambda t: jnp.stack([t[0], jnp.flip(t[1], 1)])
    xs = (both(r), per_dir(w), per_dir(k), both(v), both(kk), per_dir(a))
    xs = tuple(jnp.moveaxis(t, 2, 0) for t in xs)
    s_fin, ys = lax.scan(_jx_rwkv_step, s0, xs)
    y = ys[:, 0] + jnp.flip(ys[:, 1], 0)
    return jnp.moveaxis(y, 0, 1), s_fin


def _jx_rwkv_output(y, rin, r_k, ln_w, ln_b):
    r, _, _, v, _, _, g, k = rin
    mu = jnp.mean(y, -1, keepdims=True)
    var = jnp.mean(jnp.square(y - mu), -1, keepdims=True)
    yn = ((y - mu) * lax.rsqrt(var + GN_EPS)).reshape(y.shape[:2] + (RWKV_WIDTH,)) * ln_w + ln_b
    bonus = (jnp.sum(r * k * r_k, -1, keepdims=True) * v).reshape(yn.shape)
    return (yn + bonus) * g


def _jx_s5_discretise(lam_re, lam_im, log_dt, b_re, b_im, c_re, c_im):
    lam = lax.complex(lam_re.astype(F32), lam_im.astype(F32))
    dt = jnp.exp(log_dt.astype(F32))[..., None]
    lam_bar = jnp.exp(lam * dt)
    b = lax.complex(b_re.astype(F32), b_im.astype(F32))
    b_bar = ((lam_bar - 1.0) / lam)[..., None] * b
    cmat = lax.complex(c_re.astype(F32), c_im.astype(F32))
    return lam_bar, b_bar, cmat


def _jx_combine(e1, e2):
    a1, b1 = e1
    a2, b2 = e2
    return a1 * a2, a2 * b1 + b2


def _jx_s5_scan(u, lam_bar, b_bar, h0):
    bu = jnp.einsum('lbgc,gpc->lbgp', u.astype(jnp.complex64), b_bar)
    a = jnp.broadcast_to(lam_bar, (u.shape[0], 1) + lam_bar.shape)
    cum_a, h = lax.associative_scan(_jx_combine, (a, bu), axis=0)
    return h + cum_a * h0[None]


def _jx_s5_bidir(u, lam_bar, b_bar, cmat, d, h0_f, h0_b):
    bsz, n = u.shape[:2]
    ut = jnp.transpose(u.astype(F32).reshape(bsz, n, S5_GROUPS, S5_GROUP), (1, 0, 2, 3))
    hf = _jx_s5_scan(ut, lam_bar[0], b_bar[0], h0_f)
    hb = _jx_s5_scan(jnp.flip(ut, 0), lam_bar[1], b_bar[1], h0_b)
    y = (jnp.real(jnp.einsum('lbgp,gcp->lbgc', hf, cmat[0]))
         + jnp.flip(jnp.real(jnp.einsum('lbgp,gcp->lbgc', hb, cmat[1])), 0)
         + d.astype(F32).reshape(S5_GROUPS, S5_GROUP) * ut)
    return jnp.transpose(y, (1, 0, 2, 3)), hf[-1], hb[-1]


def _jx_s5_glu(y, glu_w, glu_b):
    z = jax.nn.gelu(y)
    gate = jax.nn.sigmoid(jnp.einsum('blgc,gce->blge', z, glu_w) + glu_b.reshape(S5_GROUPS, S5_GROUP))
    return (z * gate).reshape(y.shape[:2] + (S5_WIDTH,))


def _jx_hybrid_core(p, n_lat, conv_w, w0, w_up, a0, a_up, g_up, k_k, k_a, r_k, ln_w, ln_b,
                    lam_re, lam_im, log_dt, b_re, b_im, c_re, c_im, d, glu_w, glu_b):
    bsz = p.shape[0]
    pc, pl_ = p[:, n_lat:], p[:, :n_lat]
    rc = _jx_rwkv_inputs(pc[..., :RWKV_IN], conv_w, w0, w_up, a0, a_up, g_up, k_k, k_a)
    rl = _jx_rwkv_inputs(pl_[..., :RWKV_IN], conv_w, w0, w_up, a0, a_up, g_up, k_k, k_a)
    s0 = jnp.zeros((2, bsz, RWKV_HEADS, RWKV_HEAD_DIM, RWKV_HEAD_DIM), F32)
    yc, s_ctx = _jx_rwkv_bidir(*rc[:6], s0)
    yl, _ = _jx_rwkv_bidir(*rl[:6], s_ctx)
    lam_bar, b_bar, cmat = _jx_s5_discretise(lam_re, lam_im, log_dt, b_re, b_im, c_re, c_im)
    h0 = jnp.zeros((bsz, S5_GROUPS, S5_STATE), jnp.complex64)
    zc, hf_c, hb_c = _jx_s5_bidir(pc[..., RWKV_IN:], lam_bar, b_bar, cmat, d, h0, h0)
    zl, _, _ = _jx_s5_bidir(pl_[..., RWKV_IN:], lam_bar, b_bar, cmat, d, hf_c, hb_c)
    merge = lambda y, rin, z: jnp.concatenate([_jx_rwkv_output(y, rin, r_k, ln_w, ln_b),
                                               _jx_s5_glu(z, glu_w, glu_b)], -1)
    return jnp.concatenate([merge(yl, rl, zl), merge(yc, rc, zc)], 1)


def _stack_experts(ex, sh):
    return jnp.concatenate([ex, sh[None]], 0).astype(BF16)


def kernel(x, c, ctx, c_ctx, mod_w, mod_b, norm1_g, norm2_g, final_g, hy_w_in, hy_w_out, rk_conv, rk_w0, rk_w_up, rk_a0, rk_a_up, rk_g_up, rk_k_k, rk_k_a, rk_r_k, rk_ln_w, rk_ln_b, s5_lam_re, s5_lam_im, s5_log_dt, s5_b_re, s5_b_im, s5_c_re, s5_c_im, s5_d, s5_glu_w, s5_glu_b, mla_w_in, mla_q_norm, mla_q_up, mla_kv_norm, mla_kv_up, mla_w_out, router_w, router_b, ex_gate, ex_up, ex_down, sh_gate, sh_up, sh_down):
    bsz, n_lat, d = x.shape
    n_ctx = ctx.shape[1]
    depth = mod_w.shape[0]
    assert n_ctx == ROW_BLK and depth == 2 and bsz <= 7
    h = jnp.concatenate([x, ctx], 1)
    cc = jnp.concatenate([c, c_ctx[None], jnp.zeros((8 - bsz - 1, d), F32)], 0)
    rb = router_b[None].astype(F32)
    fg = final_g[None]
    for layer in range(depth):
        last = layer == depth - 1
        i = layer // 2
        mv = adaln(cc, mod_w[layer], mod_b[layer][None])
        m_l = mv[:bsz].reshape(bsz, 1, 6, d)
        m_c = jnp.broadcast_to(mv[bsz].reshape(1, 1, 6, d), (bsz, 1, 6, d))
        mod = jnp.concatenate([m_l, m_c], 1)
        g1 = norm1_g[layer][None]
        g2 = norm2_g[layer][None]
        if layer % 2 == 0:
            p = inproj(h, n_lat, g1, mod, hy_w_in[i].astype(BF16))
            cat = _jx_hybrid_core(p, n_lat, rk_conv[i], rk_w0[i], rk_w_up[i], rk_a0[i], rk_a_up[i], rk_g_up[i],
                                  rk_k_k[i], rk_k_a[i], rk_r_k[i], rk_ln_w[i], rk_ln_b[i], s5_lam_re[i],
                                  s5_lam_im[i], s5_log_dt[i], s5_b_re[i], s5_b_im[i], s5_c_re[i], s5_c_im[i],
                                  s5_d[i], s5_glu_w[i], s5_glu_b[i])
            h = outproj(cat, hy_w_out[i].astype(BF16), h, n_lat, mod)
        else:
            h = mla_layer(h, n_ctx, g1, mod, mla_w_in[i], mla_q_norm[i], mla_q_up[i], mla_kv_norm[i],
                          mla_kv_up[i], mla_w_out[i])
        wg = _stack_experts(ex_gate[layer], sh_gate[layer])
        wu = _stack_experts(ex_up[layer], sh_up[layer])
        wd = _stack_experts(ex_down[layer], sh_down[layer])
        h = moe(h, n_lat, g2, mod, router_w, rb, wg, wu, wd, fg, last)
    return h
```

```python
import functools
import math

import jax
import jax.numpy as jnp
from jax import lax
from jax.experimental import pallas as pl
from jax.experimental.pallas import tpu as pltpu

F32 = jnp.float32
BF16 = jnp.bfloat16
HI = lax.Precision.HIGHEST

D_MODEL = 1024
GRID_W = 64
NORM_EPS = 1e-6

RWKV_HEADS = 8
RWKV_HEAD_DIM = 64
RWKV_WIDTH = 512
W_LORA = 64
A_LORA = 64
G_LORA = 128
RWKV_IN = 3 * RWKV_WIDTH + W_LORA + A_LORA + G_LORA
DECAY_SCALE = math.exp(-0.5)
GN_EPS = 64e-5

S5_GROUP = 16
S5_GROUPS = 32
S5_WIDTH = 512
S5_STATE = 64
S5_N = S5_GROUPS * S5_STATE
HYB_IN = RWKV_IN + S5_WIDTH

MLA_HEADS = 16
Q_LORA = 256
KV_LORA = 128
QK_NOPE = 64
QK_ROPE = 32
V_DIM = 64
MLA_IN = Q_LORA + KV_LORA + QK_ROPE
MLA_SCALE = (QK_NOPE + QK_ROPE) ** -0.5
ROPE_AXIS_DIMS = QK_ROPE // 2
ROPE_BASE = 10000.0
HEAD_PAD = 128

N_EXPERTS = 16
N_GROUPS = 4
EXPERTS_PER_GROUP = 4
D_EXPERT = 256

ROW_BLK = 256
MOE_BLK = 256
ATT_TQ = 256
ATT_TK = 768
CHUNK = 64
VMEM_LIMIT = 56 * 1024 * 1024


def _cparams(sem):
    return pltpu.CompilerParams(dimension_semantics=sem, vmem_limit_bytes=VMEM_LIMIT)


def _norm_mod(x, g, shift, scale):
    y = x * lax.rsqrt(jnp.mean(x * x, -1, keepdims=True) + NORM_EPS) * g
    return y * (1.0 + scale) + shift


def _sigmoid(x):
    return 1.0 / (1.0 + jnp.exp(-x))


def _dot(a, b, hi=False):
    if hi:
        return jnp.dot(a, b, precision=HI, preferred_element_type=F32)
    return jnp.dot(a.astype(BF16), b.astype(BF16), preferred_element_type=F32)


def _dot_nt(a, b):
    return lax.dot_general(a.astype(BF16), b.astype(BF16), (((1,), (1,)), ((), ())), preferred_element_type=F32)


def _dot_tn(a, b):
    return jnp.dot(a.T.astype(BF16), b.astype(BF16), preferred_element_type=F32)


def _adaln_kernel(s_ref, w_ref, b_ref, o_ref):
    s = s_ref[...]
    s = s * _sigmoid(s)
    o_ref[...] = jnp.dot(s, w_ref[...], precision=HI, preferred_element_type=F32) + b_ref[...]


def adaln(cc, w, b):
    n = w.shape[1]
    tn = 512
    return pl.pallas_call(
        _adaln_kernel,
        grid=(n // tn,),
        in_specs=[pl.BlockSpec((8, D_MODEL), lambda j: (0, 0)),
                  pl.BlockSpec((D_MODEL, tn), lambda j: (0, j)),
                  pl.BlockSpec((1, tn), lambda j: (0, j))],
        out_specs=pl.BlockSpec((8, tn), lambda j: (0, j)),
        out_shape=jax.ShapeDtypeStruct((8, n), F32),
        compiler_params=_cparams(("parallel",)),
        name="adaln",
    )(cc, w, b)


def _mod_spec(n_lat, blk=ROW_BLK):
    return pl.BlockSpec((1, 1, 6, D_MODEL), lambda b, i, *_: (b, i // (n_lat // blk), 0, 0))


def _outproj_kernel(a_ref, w_ref, h_ref, mod_ref, o_ref):
    m = mod_ref[0, 0]
    o = jnp.dot(a_ref[0].astype(BF16), w_ref[...], preferred_element_type=F32)
    o_ref[0] = h_ref[0] + m[2:3] * o


def outproj(a, w, h, n_lat, mod):
    bsz, la, k = a.shape
    d = h.shape[2]
    return pl.pallas_call(
        _outproj_kernel,
        grid=(bsz, la // ROW_BLK),
        in_specs=[pl.BlockSpec((1, ROW_BLK, k), lambda b, i: (b, i, 0)),
                  pl.BlockSpec((k, d), lambda b, i: (0, 0)),
                  pl.BlockSpec((1, ROW_BLK, d), lambda b, i: (b, i, 0)),
                  _mod_spec(n_lat)],
        out_specs=pl.BlockSpec((1, ROW_BLK, d), lambda b, i: (b, i, 0)),
        out_shape=jax.ShapeDtypeStruct((bsz, la, d), F32),
        compiler_params=_cparams(("parallel", "parallel")),
        name="outproj",
    )(a, w, h, mod)


def _route(scores, rb):
    t = scores.shape[0]
    biased = scores + rb
    col = [biased[:, e:e + 1] for e in range(N_EXPERTS)]
    sc = [scores[:, e:e + 1] for e in range(N_EXPERTS)]
    gscore = []
    for gi in range(N_GROUPS):
        a, b, c, d = col[4 * gi:4 * gi + 4]
        hi1, lo1 = jnp.maximum(a, b), jnp.minimum(a, b)
        hi2, lo2 = jnp.maximum(c, d), jnp.minimum(c, d)
        gscore.append(jnp.maximum(hi1, hi2) + jnp.maximum(jnp.minimum(hi1, hi2), jnp.maximum(lo1, lo2)))
    gsel = []
    taken = None
    for gi in range(N_GROUPS):
        best = None
        for gj in range(gi + 1, N_GROUPS):
            best = gscore[gj] if best is None else jnp.maximum(best, gscore[gj])
        s = (gscore[gi] >= best) if best is not None else jnp.full((t, 1), True)
        if taken is not None:
            s = jnp.logical_and(s, jnp.logical_not(taken))
        taken = s if taken is None else jnp.logical_or(taken, s)
        gsel.append(s)
    masks = []
    for gi in range(N_GROUPS):
        v = col[4 * gi:4 * gi + 4]
        for j in range(4):
            rank = jnp.zeros((t, 1), F32)
            for i in range(4):
                if i == j:
                    continue
                ahead = (v[i] >= v[j]) if i < j else (v[i] > v[j])
                rank = rank + ahead.astype(F32)
            masks.append(jnp.logical_and(gsel[gi], rank < 2.0))
    wsel = [jnp.where(masks[e], sc[e], 0.0) for e in range(N_EXPERTS)]
    denom = wsel[0]
    for e in range(1, N_EXPERTS):
        denom = denom + wsel[e]
    return [w / denom for w in wsel]


def _moe_kernel(h_ref, g_ref, mod_ref, rw_ref, rb_ref, wg_ref, wu_ref, wd_ref, fg_ref, o_ref,
                f_sc, comb_sc, acc_sc, *, final_norm):
    e = pl.program_id(2)
    n_e = pl.num_programs(2)
    tm = f_sc.shape[0]

    @pl.when(e == 0)
    def _():
        m = mod_ref[0, 0]
        f = _norm_mod(h_ref[0], g_ref[...], m[3:4], m[4:5])
        f_sc[...] = f.astype(BF16)
        logits = jnp.dot(f, rw_ref[...], precision=HI, preferred_element_type=F32)
        cw = _route(_sigmoid(logits), rb_ref[...])
        lane = lax.broadcasted_iota(jnp.int32, (tm, 128), 1)
        comb = jnp.where(lane == N_EXPERTS, 1.0, 0.0)
        for ei in range(N_EXPERTS):
            comb = jnp.where(lane == ei, cw[ei], comb)
        comb_sc[...] = comb
        acc_sc[...] = jnp.zeros_like(acc_sc)

    fb = f_sc[...]
    gt = jnp.dot(fb, wg_ref[0], preferred_element_type=F32)
    up = jnp.dot(fb, wu_ref[0], preferred_element_type=F32)
    lane = lax.broadcasted_iota(jnp.int32, (tm, 128), 1)
    cw = jnp.sum(jnp.where(lane == e, comb_sc[...], 0.0), axis=1, keepdims=True)
    act = gt * _sigmoid(gt) * up * cw
    acc_sc[...] += jnp.dot(act.astype(BF16), wd_ref[0], preferred_element_type=F32)

    @pl.when(e == n_e - 1)
    def _():
        m = mod_ref[0, 0]
        y = h_ref[0] + m[5:6] * acc_sc[...]
        if final_norm:
            y = y * lax.rsqrt(jnp.mean(y * y, -1, keepdims=True) + NORM_EPS) * fg_ref[...]
        o_ref[0] = y


def moe(h, n_lat, g, mod, rw, rb, wg, wu, wd, fg, final_norm):
    bsz, lo, d = h.shape
    ne = wg.shape[0]
    kern = functools.partial(_moe_kernel, final_norm=final_norm)
    return pl.pallas_call(
        kern,
        grid=(bsz, lo // MOE_BLK, ne),
        in_specs=[pl.BlockSpec((1, MOE_BLK, d), lambda b, i, e: (b, i, 0)),
                  pl.BlockSpec((1, d), lambda b, i, e: (0, 0)),
                  _mod_spec(n_lat, MOE_BLK),
                  pl.BlockSpec((d, N_EXPERTS), lambda b, i, e: (0, 0)),
                  pl.BlockSpec((1, N_EXPERTS), lambda b, i, e: (0, 0)),
                  pl.BlockSpec((1, d, D_EXPERT), lambda b, i, e: (e, 0, 0)),
                  pl.BlockSpec((1, d, D_EXPERT), lambda b, i, e: (e, 0, 0)),
                  pl.BlockSpec((1, D_EXPERT, d), lambda b, i, e: (e, 0, 0)),
                  pl.BlockSpec((1, d), lambda b, i, e: (0, 0))],
        out_specs=pl.BlockSpec((1, MOE_BLK, d), lambda b, i, e: (b, i, 0)),
        out_shape=jax.ShapeDtypeStruct((bsz, lo, d), F32),
        scratch_shapes=[pltpu.VMEM((MOE_BLK, d), BF16),
                        pltpu.VMEM((MOE_BLK, 128), F32),
                        pltpu.VMEM((MOE_BLK, d), F32)],
        compiler_params=_cparams(("parallel", "parallel", "arbitrary")),
        name="moe",
    )(h, g, mod, rw, rb, wg, wu, wd, fg)


def _rope(x, ct, s1, s2):
    n = x.shape[1]
    return x * ct + pltpu.roll(x, n - ROPE_AXIS_DIMS, 1) * s1 + pltpu.roll(x, ROPE_AXIS_DIMS, 1) * s2


def _mla_proj_kernel(h_ref, g_ref, mod_ref, win_ref, qn_ref, qup_ref, kvn_ref, kup_ref, vup_ref, epl_ref,
                     one_ref, ct_ref, s1_ref, s2_ref, q_ref, k_ref, v_ref):
    m = mod_ref[0, 0]
    a = _norm_mod(h_ref[0], g_ref[...], m[0:1], m[1:2])
    p = jnp.dot(a.astype(BF16), win_ref[...], preferred_element_type=F32)
    qd = p[:, :Q_LORA]
    kvd = p[:, Q_LORA:Q_LORA + KV_LORA]
    kpe = p[:, Q_LORA + KV_LORA:]
    qn = qd * lax.rsqrt(jnp.mean(qd * qd, -1, keepdims=True) + NORM_EPS) * qn_ref[...]
    kvn = kvd * lax.rsqrt(jnp.mean(kvd * kvd, -1, keepdims=True) + NORM_EPS) * kvn_ref[...]
    kvb = kvn.astype(BF16)
    ct = jnp.concatenate([ct_ref[...]] * MLA_HEADS, axis=1)
    s1 = jnp.concatenate([s1_ref[...]] * MLA_HEADS, axis=1)
    s2 = jnp.concatenate([s2_ref[...]] * MLA_HEADS, axis=1)
    q = jnp.dot(qn.astype(BF16), qup_ref[...], preferred_element_type=F32)
    q_ref[0] = (_rope(q, ct, s1, s2) * MLA_SCALE).astype(BF16)
    k = (jnp.dot(kvb, kup_ref[...], preferred_element_type=F32)
         + jnp.dot(kpe.astype(BF16), epl_ref[...], preferred_element_type=F32))
    k_ref[0] = _rope(k, ct, s1, s2).astype(BF16)
    v = jnp.dot(kvb, vup_ref[...], preferred_element_type=F32) + one_ref[...]
    v_ref[0] = v.astype(BF16)


def mla_proj(h, n_lat, g, mod, win, qn, qup, kvn, kup, vup, epl, one, ct, s1, s2):
    bsz, lt, d = h.shape
    hp = MLA_HEADS * HEAD_PAD
    full = lambda shape: pl.BlockSpec(shape, lambda b, i: (0,) * len(shape))
    tab = pl.BlockSpec((ROW_BLK, HEAD_PAD), lambda b, i: (i, 0))
    out = pl.BlockSpec((1, ROW_BLK, hp), lambda b, i: (b, i, 0))
    return pl.pallas_call(
        _mla_proj_kernel,
        grid=(bsz, lt // ROW_BLK),
        in_specs=[pl.BlockSpec((1, ROW_BLK, d), lambda b, i: (b, i, 0)),
                  full((1, d)), _mod_spec(n_lat), full((d, MLA_IN)), full((1, Q_LORA)), full((Q_LORA, hp)),
                  full((1, KV_LORA)), full((KV_LORA, hp)), full((KV_LORA, hp)), full((QK_ROPE, hp)),
                  full((1, hp)), tab, tab, tab],
        out_specs=[out, out, out],
        out_shape=[jax.ShapeDtypeStruct((bsz, lt, hp), BF16)] * 3,
        compiler_params=_cparams(("parallel", "parallel")),
        name="mla_proj",
    )(h, g, mod, win, qn, qup, kvn, kup, vup, epl, one, ct, s1, s2)


def _attn_kernel(q_ref, k_ref, v_ref, o_ref, *, tk, nk):
    q = q_ref[0]
    tq = q.shape[0]

    def body(j, carry):
        m, acc = carry
        start = pl.multiple_of(j * tk, tk)
        kc = k_ref[0, pl.ds(start, tk), :]
        vc = v_ref[0, pl.ds(start, tk), :]
        s = lax.dot_general(q, kc, (((1,), (1,)), ((), ())), preferred_element_type=F32)
        m_new = jnp.maximum(m, jnp.max(s, axis=1, keepdims=True))
        alpha = jnp.exp(m - m_new)
        p = jnp.exp(s - m_new).astype(BF16)
        acc = acc * alpha + jnp.dot(p, vc, preferred_element_type=F32)
        return m_new, acc

    m0 = jnp.full((tq, 1), -1e30, F32)
    acc0 = jnp.zeros((tq, HEAD_PAD), F32)
    _, acc = lax.fori_loop(0, nk, body, (m0, acc0))
    o_ref[0] = (acc / acc[:, V_DIM:V_DIM + 1]).astype(BF16)


def attention(q, k, v, lq):
    bsz, lt, hp = q.shape
    kern = functools.partial(_attn_kernel, tk=ATT_TK, nk=lt // ATT_TK)
    return pl.pallas_call(
        kern,
        grid=(bsz, MLA_HEADS, lq // ATT_TQ),
        in_specs=[pl.BlockSpec((1, ATT_TQ, HEAD_PAD), lambda b, h, i: (b, i, h)),
                  pl.BlockSpec((1, lt, HEAD_PAD), lambda b, h, i: (b, 0, h)),
                  pl.BlockSpec((1, lt, HEAD_PAD), lambda b, h, i: (b, 0, h))],
        out_specs=pl.BlockSpec((1, ATT_TQ, HEAD_PAD), lambda b, h, i: (b, i, h)),
        out_shape=jax.ShapeDtypeStruct((bsz, lq, hp), BF16),
        compiler_params=_cparams(("parallel", "parallel", "parallel")),
        name="attention",
    )(q, k, v)


def _rope_tables(n_lat, n_ctx):
    rows = n_lat // GRID_W
    row = jnp.repeat(jnp.arange(rows, dtype=F32), GRID_W)
    col = jnp.tile(jnp.arange(GRID_W, dtype=F32), rows)
    inv_freq = ROPE_BASE ** (-jnp.arange(0, ROPE_AXIS_DIMS, 2, dtype=F32) / ROPE_AXIS_DIMS)
    ang = jnp.concatenate([row[:, None] * inv_freq, col[:, None] * inv_freq], -1)
    cos = jnp.concatenate([jnp.cos(ang), jnp.ones((n_ctx, ROPE_AXIS_DIMS), F32)], 0)
    sin = jnp.concatenate([jnp.sin(ang), jnp.zeros((n_ctx, ROPE_AXIS_DIMS), F32)], 0)
    lt = n_lat + n_ctx
    one = jnp.ones((lt, QK_NOPE), F32)
    z16 = jnp.zeros((lt, ROPE_AXIS_DIMS), F32)
    z32 = jnp.zeros((lt, HEAD_PAD - QK_NOPE - QK_ROPE), F32)
    z64 = jnp.zeros((lt, QK_NOPE), F32)
    ct = jnp.concatenate([one, cos, cos, z32], 1)
    s1 = jnp.concatenate([z64, -sin, z16, z32], 1)
    s2 = jnp.concatenate([z64, z16, sin, z32], 1)
    return ct, s1, s2


def _pad_heads(w, width, offset=0):
    k = w.shape[0]
    w = w.reshape(k, MLA_HEADS, width)
    w = jnp.pad(w, ((0, 0), (0, 0), (offset, HEAD_PAD - width - offset)))
    return w.reshape(k, MLA_HEADS * HEAD_PAD)


def mla_layer(h, n_ctx, g, mod, w_in, q_norm, q_up, kv_norm, kv_up, w_out):
    bsz, lt, d = h.shape
    n_lat = lt - n_ctx
    ct, s1, s2 = _rope_tables(n_lat, n_ctx)
    qup = _pad_heads(q_up, QK_NOPE + QK_ROPE).astype(BF16)
    kvu = kv_up.reshape(KV_LORA, MLA_HEADS, QK_NOPE + V_DIM)
    kup = _pad_heads(kvu[:, :, :QK_NOPE].reshape(KV_LORA, -1), QK_NOPE).astype(BF16)
    vup = _pad_heads(kvu[:, :, QK_NOPE:].reshape(KV_LORA, -1), V_DIM).astype(BF16)
    epl = _pad_heads(jnp.tile(jnp.eye(QK_ROPE, dtype=F32), (1, MLA_HEADS)), QK_ROPE, QK_NOPE).astype(BF16)
    one = _pad_heads(jnp.ones((1, MLA_HEADS), F32), 1, V_DIM)
    q, k, v = mla_proj(h, n_lat, g, mod, w_in.astype(BF16), q_norm[None], qup, kv_norm[None], kup, vup, epl, one,
                       ct, s1, s2)
    o = attention(q, k, v, n_lat)
    wo = w_out.reshape(MLA_HEADS, V_DIM, d)
    wo = jnp.pad(wo, ((0, 0), (0, HEAD_PAD - V_DIM), (0, 0))).reshape(MLA_HEADS * HEAD_PAD, d).astype(BF16)
    return outproj(o, wo, h, n_lat, mod)


def _inproj2_kernel(h_ref, g_ref, mod_ref, w_ref, p_ref, u_ref):
    m = mod_ref[0, 0]
    a = _norm_mod(h_ref[0], g_ref[...], m[0:1], m[1:2])
    o = jnp.dot(a.astype(BF16), w_ref[...], preferred_element_type=F32)
    p_ref[0] = o[:, :RWKV_IN]
    u_ref[...] = o[:, RWKV_IN:]


def inproj2(h, n_lat, g, mod, w):
    bsz, lt, d = h.shape
    return pl.pallas_call(
        _inproj2_kernel,
        grid=(bsz, lt // ROW_BLK),
        in_specs=[pl.BlockSpec((1, ROW_BLK, d), lambda b, i: (b, i, 0)),
                  pl.BlockSpec((1, d), lambda b, i: (0, 0)),
                  _mod_spec(n_lat),
                  pl.BlockSpec((d, HYB_IN), lambda b, i: (0, 0))],
        out_specs=[pl.BlockSpec((1, ROW_BLK, RWKV_IN), lambda b, i: (b, i, 0)),
                   pl.BlockSpec((ROW_BLK, S5_WIDTH), lambda b, i: (i, b))],
        out_shape=[jax.ShapeDtypeStruct((bsz, lt, RWKV_IN), F32),
                   jax.ShapeDtypeStruct((lt, bsz * S5_WIDTH), F32)],
        compiler_params=_cparams(("parallel", "parallel")),
        name="inproj2",
    )(h, g, mod, w)


def _tri_inverse_all(mats, idx_r, idx_c):
    c = mats[0].shape[0]
    eye = (idx_r == idx_c).astype(F32)
    blk8 = (idx_r >> 3) == (idx_c >> 3)
    ns = [jnp.where(blk8, -a, 0.0) for a in mats]
    ts = [eye + n for n in ns]
    n2 = [_dot(n, n) for n in ns]
    ts = [t + _dot(m, t) for m, t in zip(n2, ts)]
    n4 = [_dot(m, m) for m in n2]
    ts = [t + _dot(m, t) for m, t in zip(n4, ts)]
    sh = 3
    while (1 << sh) < c:
        off = jnp.logical_and((idx_r >> (sh + 1)) == (idx_c >> (sh + 1)), (idx_r >> sh) != (idx_c >> sh))
        ms = [_dot(jnp.where(off, a, 0.0), t) for a, t in zip(mats, ts)]
        ts = [t - _dot(t, m) for m, t in zip(ms, ts)]
        sh += 1
    return ts


def _rwkv_conv(pc, prev_row, next_row, cw):
    c = pc.shape[0]
    row = lax.broadcasted_iota(jnp.int32, (c, 1), 0)
    xm1 = jnp.where(row == 0, prev_row, pltpu.roll(pc, 1, 0))
    xp1 = jnp.where(row == c - 1, next_row, pltpu.roll(pc, c - 1, 0))
    return cw[0:1] * xm1 + cw[1:2] * pc + cw[2:3] * xp1


def _rwkv_chunk_inputs(x, z, w0, wup, a0, aup, kkv, kav):
    c = x.shape[0]
    n = RWKV_HEAD_DIM
    r = x[:, 0:RWKV_WIDTH]
    k = x[:, RWKV_WIDTH:2 * RWKV_WIDTH]
    v = x[:, 2 * RWKV_WIDTH:3 * RWKV_WIDTH]
    wd = x[:, 3 * RWKV_WIDTH:3 * RWKV_WIDTH + W_LORA]
    ad = x[:, 3 * RWKV_WIDTH + W_LORA:3 * RWKV_WIDTH + W_LORA + A_LORA]
    lw = -DECAY_SCALE * _sigmoid(w0 + _dot(jnp.tanh(wd), wup, True))
    a = _sigmoid(a0 + _dot(ad, aup, True))
    idx_r = lax.broadcasted_iota(jnp.int32, (c, c), 0)
    idx_c = lax.broadcasted_iota(jnp.int32, (c, c), 1)
    if z == 0:
        incl = idx_c <= idx_r
        strict = idx_c < idx_r
        last = c - 1
    else:
        incl = idx_c >= idx_r
        strict = idx_c > idx_r
        last = 0
    cum = _dot(incl.astype(F32), lw, True)
    tot = cum[last:last + 1]
    e1 = jnp.exp(cum)
    e2 = jnp.exp(cum - lw)
    e3 = jnp.exp(-cum)
    e4 = jnp.exp(tot - cum)
    etot = jnp.exp(tot)
    kkf = k * kkv
    heads = []
    for h in range(RWKV_HEADS):
        sl = slice(h * n, (h + 1) * n)
        kk = kkf[:, sl]
        kk = kk * lax.rsqrt(jnp.sum(kk * kk, -1, keepdims=True) + 1e-12)
        ah = a[:, sl]
        bh = ah * kk
        kd = k[:, sl] * (1.0 + (ah - 1.0) * kav[:, sl])
        heads.append(dict(
            xq=jnp.concatenate([kk * e2[:, sl], r[:, sl] * e1[:, sl]], 0),
            yk=jnp.concatenate([kd * e3[:, sl], bh * e3[:, sl]], 0),
            ke=jnp.concatenate([kd * e4[:, sl], bh * e4[:, sl]], 0),
            v=v[:, sl], etot=etot[:, sl], incl=incl, strict=strict, state=(z, h)))
    return heads


def _rwkv_solve_all(heads, s_sc):
    c = heads[0]['v'].shape[0]
    idx_r = lax.broadcasted_iota(jnp.int32, (c, c), 0)
    idx_c = lax.broadcasted_iota(jnp.int32, (c, c), 1)
    gs = [_dot_nt(hd['xq'], hd['yk']) for hd in heads]
    ss = [s_sc[hd['state'][0], hd['state'][1]] for hd in heads]
    xs = [_dot_nt(hd['xq'], s) for hd, s in zip(heads, ss)]
    avs = [_dot(jnp.where(hd['strict'], g[:c, :c], 0.0), hd['v']) for hd, g in zip(heads, gs)]
    ts = _tri_inverse_all([jnp.where(hd['strict'], g[:c, c:], 0.0) for hd, g in zip(heads, gs)], idx_r, idx_c)
    sas = [_dot(t, x[:c] + av) for t, x, av in zip(ts, xs, avs)]
    outs = []
    for hd, g, x, sa, s in zip(heads, gs, xs, sas, ss):
        a_r = jnp.concatenate([jnp.where(hd['incl'], g[c:, :c], 0.0), jnp.where(hd['incl'], -g[c:, c:], 0.0)], 1)
        outs.append(x[c:] + _dot(a_r, jnp.concatenate([hd['v'], sa], 0)))
        z, h = hd['state']
        s_sc[z, h] = s * hd['etot'] + _dot_tn(jnp.concatenate([hd['v'], -sa], 0), hd['ke'])
    return outs


def _rwkv_kernel(pa_ref, pap_ref, pan_ref, pb_ref, pbp_ref, pbn_ref, cw_ref, w0_ref, wup_ref, a0_ref, aup_ref,
                 gup_ref, kk_ref, ka_ref, rk_ref, y0_ref, y1_ref, bonus_ref, g_ref, s_sc, *, nl, nc):
    s = pl.program_id(1)

    @pl.when(s == 0)
    def _():
        s_sc[...] = jnp.zeros_like(s_sc)

    ca = (s + nl) % nc
    cb = nc - 1 - s
    cw = cw_ref[...]

    def load(p_ref, pp_ref, pn_ref, cidx):
        first = jnp.logical_or(cidx == 0, cidx == nl)
        lastc = jnp.logical_or(cidx == nl - 1, cidx == nc - 1)
        prev_row = jnp.where(first, 0.0, pp_ref[0, 7:8, :])
        next_row = jnp.where(lastc, 0.0, pn_ref[0, 0:1, :])
        return _rwkv_conv(p_ref[0], prev_row, next_row, cw)

    xa = load(pa_ref, pap_ref, pan_ref, ca)
    xb = load(pb_ref, pbp_ref, pbn_ref, cb)
    heads = (_rwkv_chunk_inputs(xa, 0, w0_ref[0:1], wup_ref[0], a0_ref[0:1], aup_ref[0], kk_ref[...], ka_ref[...])
             + _rwkv_chunk_inputs(xb, 1, w0_ref[1:2], wup_ref[1], a0_ref[1:2], aup_ref[1], kk_ref[...], ka_ref[...]))
    outs = _rwkv_solve_all(heads, s_sc)
    y0_ref[0] = jnp.concatenate(outs[:RWKV_HEADS], 1)
    y1_ref[0] = jnp.concatenate(outs[RWKV_HEADS:], 1)
    gd = xa[:, 3 * RWKV_WIDTH + W_LORA + A_LORA:]
    g_ref[0] = _dot(_sigmoid(gd), gup_ref[...], True)
    r = xa[:, 0:RWKV_WIDTH]
    k = xa[:, RWKV_WIDTH:2 * RWKV_WIDTH]
    v = xa[:, 2 * RWKV_WIDTH:3 * RWKV_WIDTH]
    rkr = r * k * rk_ref[...]
    bon = []
    for h in range(RWKV_HEADS):
        sl = slice(h * RWKV_HEAD_DIM, (h + 1) * RWKV_HEAD_DIM)
        bon.append(jnp.sum(rkr[:, sl], -1, keepdims=True) * v[:, sl])
    bonus_ref[0] = jnp.concatenate(bon, 1)


def rwkv_scan(p, n_lat, cw, w0, wup, a0, aup, gup, kkv, kav, rk):
    bsz, lt, _ = p.shape
    nl = n_lat // CHUNK
    nc = lt // CHUNK
    nb8 = lt // 8
    ca = lambda s: (s + nl) % nc
    cb = lambda s: nc - 1 - s
    pspec = lambda cf: pl.BlockSpec((1, CHUNK, RWKV_IN), lambda b, s: (b, cf(s), 0))
    prev = lambda cf: pl.BlockSpec((1, 8, RWKV_IN), lambda b, s: (b, jnp.maximum(cf(s) * (CHUNK // 8) - 1, 0), 0))
    nxt = lambda cf: pl.BlockSpec((1, 8, RWKV_IN),
                                  lambda b, s: (b, jnp.minimum((cf(s) + 1) * (CHUNK // 8), nb8 - 1), 0))
    full = lambda a: pl.BlockSpec(a.shape, lambda b, s: (0,) * a.ndim)
    ospec = lambda cf: pl.BlockSpec((1, CHUNK, RWKV_WIDTH), lambda b, s: (b, cf(s), 0))
    oshape = jax.ShapeDtypeStruct((bsz, lt, RWKV_WIDTH), F32)
    kern = functools.partial(_rwkv_kernel, nl=nl, nc=nc)
    params = (cw, w0, wup, a0, aup, gup, kkv, kav, rk)
    return pl.pallas_call(
        kern,
        grid=(bsz, nc),
        in_specs=[pspec(ca), prev(ca), nxt(ca), pspec(cb), prev(cb), nxt(cb)] + [full(a) for a in params],
        out_specs=[ospec(ca), ospec(cb), ospec(ca), ospec(ca)],
        out_shape=[oshape] * 4,
        scratch_shapes=[pltpu.VMEM((2, RWKV_HEADS, RWKV_HEAD_DIM, RWKV_HEAD_DIM), F32)],
        compiler_params=_cparams(("parallel", "arbitrary")),
        name="rwkv_scan",
    )(p, p, p, p, p, p, *params)


def _s5_kernel(u_ref, lam_ref, bf_ref, cf_ref, y_ref, bu_sc, h_sc, *, reverse):
    s = pl.program_id(0)

    @pl.when(s == 0)
    def _():
        h_sc[...] = jnp.zeros_like(h_sc)

    rows = bu_sc.shape[0]
    bu_sc[...] = jnp.dot(u_ref[...].astype(BF16), bf_ref[...], preferred_element_type=F32)
    lr = lam_ref[0:1, :]
    li = lam_ref[1:2, :]
    ntile = rows // 8
    row = lax.broadcasted_iota(jnp.int32, (8, 1), 0)
    first_half = (row >= 4) if reverse else (row < 4)

    def body(j, carry):
        hr, hi = carry
        jj = (ntile - 1 - j) if reverse else j
        start = pl.multiple_of(jj * 8, 8)
        xr = bu_sc[pl.ds(start, 8), pl.ds(0, S5_N)]
        xi = bu_sc[pl.ds(start, 8), pl.ds(S5_N, S5_N)]
        ar, ai = lr * hr - li * hi + xr, lr * hi + li * hr + xi
        sr, si = pltpu.roll(ar, 4, 0), pltpu.roll(ai, 4, 0)
        br, bi = lr * sr - li * si + xr, lr * si + li * sr + xi
        bu_sc[pl.ds(start, 8), pl.ds(0, S5_N)] = jnp.where(first_half, ar, br)
        bu_sc[pl.ds(start, 8), pl.ds(S5_N, S5_N)] = jnp.where(first_half, ai, bi)
        return pltpu.roll(br, 4, 0), pltpu.roll(bi, 4, 0)

    hr, hi = lax.fori_loop(0, ntile, body, (h_sc[0], h_sc[1]))
    h_sc[0] = hr
    h_sc[1] = hi
    y_ref[...] = jnp.dot(bu_sc[...].astype(BF16), cf_ref[...], preferred_element_type=F32)


def s5_scan(u, n_lat, bsz, lam, bfull, cfull, reverse):
    assert bsz == 4
    rows_total = u.shape[0]
    lt = rows_total // bsz
    nl = n_lat // CHUNK
    nc = lt // CHUNK
    rb = CHUNK * bsz
    cidx = (lambda s: nc - 1 - s) if reverse else (lambda s: (s + nl) % nc)
    kern = functools.partial(_s5_kernel, reverse=reverse)
    return pl.pallas_call(
        kern,
        grid=(nc,),
        in_specs=[pl.BlockSpec((rb, S5_WIDTH), lambda s: (cidx(s), 0)),
                  pl.BlockSpec((2, S5_N), lambda s: (0, 0)),
                  pl.BlockSpec((S5_WIDTH, 2 * S5_N), lambda s: (0, 0)),
                  pl.BlockSpec((2 * S5_N, S5_WIDTH), lambda s: (0, 0))],
        out_specs=pl.BlockSpec((rb, S5_WIDTH), lambda s: (cidx(s), 0)),
        out_shape=jax.ShapeDtypeStruct((rows_total, S5_WIDTH), F32),
        scratch_shapes=[pltpu.VMEM((rb, 2 * S5_N), F32), pltpu.VMEM((2, 8, S5_N), F32)],
        compiler_params=_cparams(("arbitrary",)),
        name="s5_rev" if reverse else "s5_fwd",
    )(u, lam, bfull, cfull)


def _s5_params(lam_re, lam_im, log_dt, b_re, b_im, c_re, c_im):
    dt = jnp.exp(log_dt.astype(F32))[..., None]
    mag = jnp.exp(lam_re * dt)
    lbr = mag * jnp.cos(lam_im * dt)
    lbi = mag * jnp.sin(lam_im * dt)
    den = lam_re * lam_re + lam_im * lam_im
    nr, ni = lbr - 1.0, lbi
    fr = (nr * lam_re + ni * lam_im) / den
    fi = (ni * lam_re - nr * lam_im) / den
    bbr = fr[..., None] * b_re - fi[..., None] * b_im
    bbi = fr[..., None] * b_im + fi[..., None] * b_re
    eye = jnp.eye(S5_GROUPS, dtype=F32)

    def bd_in(w):
        return jnp.einsum('gpc,gh->gchp', w, eye).reshape(S5_WIDTH, S5_N)

    def bd_out(w):
        return jnp.einsum('gcp,gh->gphc', w, eye).reshape(S5_N, S5_WIDTH)

    lam = jnp.stack([lbr.reshape(2, S5_N), lbi.reshape(2, S5_N)], 1)
    bfull = jnp.stack([jnp.concatenate([bd_in(bbr[z]), bd_in(bbi[z])], 1) for z in range(2)])
    cfull = jnp.stack([jnp.concatenate([bd_out(c_re[z]), -bd_out(c_im[z])], 0) for z in range(2)])
    return lam, bfull.astype(BF16), cfull.astype(BF16)


def _gelu(x):
    return 0.5 * x * (1.0 + jnp.tanh(math.sqrt(2.0 / math.pi) * (x + 0.044715 * x * x * x)))


def _merge_kernel(y0_ref, y1_ref, bonus_ref, g_ref, s0_ref, s1_ref, u_ref, h_ref, mod_ref, lnw_ref, lnb_ref,
                  d_ref, gluw_ref, glub_ref, wout_ref, o_ref):
    y = y0_ref[0] + y1_ref[0]
    n = RWKV_HEAD_DIM
    parts = []
    for h in range(RWKV_HEADS):
        yh = y[:, h * n:(h + 1) * n]
        mu = jnp.mean(yh, -1, keepdims=True)
        dlt = yh - mu
        var = jnp.mean(dlt * dlt, -1, keepdims=True)
        parts.append(dlt * lax.rsqrt(var + GN_EPS))
    yn = jnp.concatenate(parts, 1) * lnw_ref[...] + lnb_ref[...]
    rw = (yn + bonus_ref[0]) * g_ref[0]
    ys = s0_ref[...] + s1_ref[...] + d_ref[...] * u_ref[...]
    zz = _gelu(ys)
    gate = _sigmoid(_dot(zz, gluw_ref[...]) + glub_ref[...])
    cat = jnp.concatenate([rw, zz * gate], 1).astype(BF16)
    m = mod_ref[0, 0]
    o_ref[0] = h_ref[0] + m[2:3] * jnp.dot(cat, wout_ref[...], preferred_element_type=F32)


def merge(y0, y1, bonus, g, s0, s1, u, h, n_lat, mod, lnw, lnb, dvec, gluw, glub, wout):
    bsz, lt, d = h.shape
    rspec = pl.BlockSpec((1, ROW_BLK, RWKV_WIDTH), lambda b, i: (b, i, 0))
    sspec = pl.BlockSpec((ROW_BLK, S5_WIDTH), lambda b, i: (i, b))
    full = lambda a: pl.BlockSpec(a.shape, lambda b, i: (0,) * a.ndim)
    params = (lnw, lnb, dvec, gluw, glub, wout)
    return pl.pallas_call(
        _merge_kernel,
        grid=(bsz, lt // ROW_BLK),
        in_specs=[rspec, rspec, rspec, rspec, sspec, sspec, sspec,
                  pl.BlockSpec((1, ROW_BLK, d), lambda b, i: (b, i, 0)), _mod_spec(n_lat)]
                 + [full(a) for a in params],
        out_specs=pl.BlockSpec((1, ROW_BLK, d), lambda b, i: (b, i, 0)),
        out_shape=jax.ShapeDtypeStruct((bsz, lt, d), F32),
        compiler_params=_cparams(("parallel", "parallel")),
        name="merge",
    )(y0, y1, bonus, g, s0, s1, u, h, mod, *params)


def hybrid_layer(h, n_lat, g1, mod, w_in, w_out, conv_w, w0, w_up, a0, a_up, g_up, k_k, k_a, r_k, ln_w, ln_b,
                 lam_re, lam_im, log_dt, b_re, b_im, c_re, c_im, dvec, glu_w, glu_b):
    bsz = h.shape[0]
    p, u = inproj2(h, n_lat, g1, mod, w_in.astype(BF16))
    y0, y1, bonus, g = rwkv_scan(p, n_lat, conv_w, w0, w_up, a0, a_up, g_up, k_k[None], k_a[None],
                                 r_k.reshape(1, RWKV_WIDTH))
    lam, bfull, cfull = _s5_params(lam_re, lam_im, log_dt, b_re, b_im, c_re, c_im)
    u2 = u.reshape(-1, S5_WIDTH)
    s0 = s5_scan(u2, n_lat, bsz, lam[0], bfull[0], cfull[0], False).reshape(u.shape)
    s1 = s5_scan(u2, n_lat, bsz, lam[1], bfull[1], cfull[1], True).reshape(u.shape)
    gluw = jnp.einsum('gce,gh->gche', glu_w, jnp.eye(S5_GROUPS, dtype=F32)).reshape(S5_WIDTH, S5_WIDTH)
    return merge(y0, y1, bonus, g, s0, s1, u, h, n_lat, mod, ln_w[None], ln_b[None], dvec[None],
                 gluw.astype(BF16), glu_b[None], w_out.astype(BF16))


def _stack_experts(ex, sh):
    return jnp.concatenate([ex, sh[None]], 0).astype(BF16)


def kernel(x, c, ctx, c_ctx, mod_w, mod_b, norm1_g, norm2_g, final_g, hy_w_in, hy_w_out, rk_conv, rk_w0, rk_w_up, rk_a0, rk_a_up, rk_g_up, rk_k_k, rk_k_a, rk_r_k, rk_ln_w, rk_ln_b, s5_lam_re, s5_lam_im, s5_log_dt, s5_b_re, s5_b_im, s5_c_re, s5_c_im, s5_d, s5_glu_w, s5_glu_b, mla_w_in, mla_q_norm, mla_q_up, mla_kv_norm, mla_kv_up, mla_w_out, router_w, router_b, ex_gate, ex_up, ex_down, sh_gate, sh_up, sh_down):
    bsz, n_lat, d = x.shape
    n_ctx = ctx.shape[1]
    depth = mod_w.shape[0]
    assert n_ctx == ROW_BLK and depth == 2 and bsz <= 7
    h = jnp.concatenate([x, ctx], 1)
    cc = jnp.concatenate([c, c_ctx[None], jnp.zeros((8 - bsz - 1, d), F32)], 0)
    rb = router_b[None].astype(F32)
    fg = final_g[None]
    for layer in range(depth):
        last = layer == depth - 1
        i = layer // 2
        mv = adaln(cc, mod_w[layer], mod_b[layer][None])
        m_l = mv[:bsz].reshape(bsz, 1, 6, d)
        m_c = jnp.broadcast_to(mv[bsz].reshape(1, 1, 6, d), (bsz, 1, 6, d))
        mod = jnp.concatenate([m_l, m_c], 1)
        g1 = norm1_g[layer][None]
        g2 = norm2_g[layer][None]
        if layer % 2 == 0:
            h = hybrid_layer(h, n_lat, g1, mod, hy_w_in[i], hy_w_out[i], rk_conv[i], rk_w0[i], rk_w_up[i], rk_a0[i],
                             rk_a_up[i], rk_g_up[i], rk_k_k[i], rk_k_a[i], rk_r_k[i], rk_ln_w[i], rk_ln_b[i],
                             s5_lam_re[i], s5_lam_im[i], s5_log_dt[i], s5_b_re[i], s5_b_im[i], s5_c_re[i],
                             s5_c_im[i], s5_d[i], s5_glu_w[i], s5_glu_b[i])
        else:
            h = mla_layer(h, n_ctx, g1, mod, mla_w_in[i], mla_q_norm[i], mla_q_up[i], mla_kv_norm[i],
                          mla_kv_up[i], mla_w_out[i])
        wg = _stack_experts(ex_gate[layer], sh_gate[layer])
        wu = _stack_experts(ex_up[layer], sh_up[layer])
        wd = _stack_experts(ex_down[layer], sh_down[layer])
        h = moe(h, n_lat, g2, mod, router_w, rb, wg, wu, wd, fg, last)
    return h
```

```python
import functools
import math

import jax
import jax.numpy as jnp
from jax import lax
from jax.experimental import pallas as pl
from jax.experimental.pallas import tpu as pltpu

F32 = jnp.float32
BF16 = jnp.bfloat16
HI = lax.Precision.HIGHEST

D_MODEL = 1024
GRID_W = 64
NORM_EPS = 1e-6

RWKV_HEADS = 8
RWKV_HEAD_DIM = 64
RWKV_WIDTH = 512
W_LORA = 64
A_LORA = 64
G_LORA = 128
RWKV_IN = 3 * RWKV_WIDTH + W_LORA + A_LORA + G_LORA
DECAY_SCALE = math.exp(-0.5)
GN_EPS = 64e-5

S5_GROUP = 16
S5_GROUPS = 32
S5_WIDTH = 512
S5_STATE = 64
S5_N = S5_GROUPS * S5_STATE
HYB_IN = RWKV_IN + S5_WIDTH

MLA_HEADS = 16
Q_LORA = 256
KV_LORA = 128
QK_NOPE = 64
QK_ROPE = 32
V_DIM = 64
MLA_IN = Q_LORA + KV_LORA + QK_ROPE
MLA_SCALE = (QK_NOPE + QK_ROPE) ** -0.5
ROPE_AXIS_DIMS = QK_ROPE // 2
ROPE_BASE = 10000.0
HEAD_PAD = 128

N_EXPERTS = 16
N_GROUPS = 4
EXPERTS_PER_GROUP = 4
D_EXPERT = 256

ROW_BLK = 256
MOE_BLKS = (1024, 768, 512, 256)
ATT_TQS = (1024, 512, 256)
ATT_TKS = (1408, 768, 256)
ATT_SUB = 256
CHUNK = 64
VMEM_LIMIT = 56 * 1024 * 1024


def _cparams(sem):
    return pltpu.CompilerParams(dimension_semantics=sem, vmem_limit_bytes=VMEM_LIMIT)


def _norm_mod(x, g, shift, scale):
    y = x * lax.rsqrt(jnp.mean(x * x, -1, keepdims=True) + NORM_EPS) * g
    return y * (1.0 + scale) + shift


def _sigmoid(x):
    return 1.0 / (1.0 + jnp.exp(-x))


def _dot(a, b, hi=False):
    if hi:
        return jnp.dot(a, b, precision=HI, preferred_element_type=F32)
    return jnp.dot(a.astype(BF16), b.astype(BF16), preferred_element_type=F32)


def _dot_nt(a, b):
    return lax.dot_general(a.astype(BF16), b.astype(BF16), (((1,), (1,)), ((), ())), preferred_element_type=F32)


def _dot_tn(a, b):
    return jnp.dot(a.T.astype(BF16), b.astype(BF16), preferred_element_type=F32)


def _adaln_kernel(s_ref, w_ref, b_ref, o_ref):
    s = s_ref[...]
    s = s * _sigmoid(s)
    o_ref[...] = jnp.dot(s, w_ref[...], precision=HI, preferred_element_type=F32) + b_ref[...]


def adaln(cc, w, b):
    n = w.shape[1]
    tn = 512
    return pl.pallas_call(
        _adaln_kernel,
        grid=(n // tn,),
        in_specs=[pl.BlockSpec((8, D_MODEL), lambda j: (0, 0)),
                  pl.BlockSpec((D_MODEL, tn), lambda j: (0, j)),
                  pl.BlockSpec((1, tn), lambda j: (0, j))],
        out_specs=pl.BlockSpec((8, tn), lambda j: (0, j)),
        out_shape=jax.ShapeDtypeStruct((8, n), F32),
        compiler_params=_cparams(("parallel",)),
        name="adaln",
    )(cc, w, b)


def _mod_spec(n_lat, blk=ROW_BLK):
    return pl.BlockSpec((1, 1, 6, D_MODEL), lambda b, i, *_: (b, i // (n_lat // blk), 0, 0))


def _outproj_kernel(a_ref, w_ref, h_ref, mod_ref, o_ref):
    m = mod_ref[0, 0]
    o = jnp.dot(a_ref[0].astype(BF16), w_ref[...], preferred_element_type=F32)
    o_ref[0] = h_ref[0] + m[2:3] * o


def outproj(a, w, h, n_lat, mod):
    bsz, la, k = a.shape
    d = h.shape[2]
    return pl.pallas_call(
        _outproj_kernel,
        grid=(bsz, la // ROW_BLK),
        in_specs=[pl.BlockSpec((1, ROW_BLK, k), lambda b, i: (b, i, 0)),
                  pl.BlockSpec((k, d), lambda b, i: (0, 0)),
                  pl.BlockSpec((1, ROW_BLK, d), lambda b, i: (b, i, 0)),
                  _mod_spec(n_lat)],
        out_specs=pl.BlockSpec((1, ROW_BLK, d), lambda b, i: (b, i, 0)),
        out_shape=jax.ShapeDtypeStruct((bsz, la, d), F32),
        compiler_params=_cparams(("parallel", "parallel")),
        name="outproj",
    )(a, w, h, mod)


def _route(scores, rb):
    t = scores.shape[1]
    biased = scores + rb
    col = [biased[e:e + 1, :] for e in range(N_EXPERTS)]
    sc = [scores[e:e + 1, :] for e in range(N_EXPERTS)]
    gscore = []
    for gi in range(N_GROUPS):
        a, b, c, d = col[4 * gi:4 * gi + 4]
        hi1, lo1 = jnp.maximum(a, b), jnp.minimum(a, b)
        hi2, lo2 = jnp.maximum(c, d), jnp.minimum(c, d)
        gscore.append(jnp.maximum(hi1, hi2) + jnp.maximum(jnp.minimum(hi1, hi2), jnp.maximum(lo1, lo2)))
    gsel = []
    taken = None
    for gi in range(N_GROUPS):
        best = None
        for gj in range(gi + 1, N_GROUPS):
            best = gscore[gj] if best is None else jnp.maximum(best, gscore[gj])
        s = (gscore[gi] >= best) if best is not None else jnp.full((1, t), True)
        if taken is not None:
            s = jnp.logical_and(s, jnp.logical_not(taken))
        taken = s if taken is None else jnp.logical_or(taken, s)
        gsel.append(s)
    masks = []
    for gi in range(N_GROUPS):
        v = col[4 * gi:4 * gi + 4]
        for j in range(4):
            rank = jnp.zeros((1, t), F32)
            for i in range(4):
                if i == j:
                    continue
                ahead = (v[i] >= v[j]) if i < j else (v[i] > v[j])
                rank = rank + ahead.astype(F32)
            masks.append(jnp.logical_and(gsel[gi], rank < 2.0))
    wsel = [jnp.where(masks[e], sc[e], 0.0) for e in range(N_EXPERTS)]
    denom = wsel[0]
    for e in range(1, N_EXPERTS):
        denom = denom + wsel[e]
    return [w / denom for w in wsel]


def _moe_kernel(h_ref, g_ref, mod_ref, rw_ref, rb_ref, wg_ref, wu_ref, wd_ref, fg_ref, o_ref,
                f_sc, comb_sc, acc_sc, *, final_norm, n_lat, has_ctx):
    i = pl.program_id(1)
    e = pl.program_id(2)
    n_e = pl.num_programs(2)
    tm = f_sc.shape[0]

    def mod_row(k):
        lat = mod_ref[0, 0, k:k + 1, :]
        if not has_ctx:
            return lat
        row = i * tm + lax.broadcasted_iota(jnp.int32, (tm, 1), 0)
        return jnp.where(row >= n_lat, mod_ref[0, 1, k:k + 1, :], lat)

    @pl.when(e == 0)
    def _():
        f = _norm_mod(h_ref[0], g_ref[...], mod_row(3), mod_row(4))
        f_sc[...] = f.astype(BF16)
        logits = lax.dot_general(rw_ref[...], f, (((1,), (1,)), ((), ())), precision=HI,
                                 preferred_element_type=F32)
        cw = _route(_sigmoid(logits), rb_ref[...])
        sub = lax.broadcasted_iota(jnp.int32, (128, tm), 0)
        comb = jnp.where(sub == N_EXPERTS, 1.0, 0.0)
        for ei in range(N_EXPERTS):
            comb = jnp.where(sub == ei, cw[ei], comb)
        comb_sc[...] = comb.T
        acc_sc[...] = jnp.zeros_like(acc_sc)

    fb = f_sc[...]
    gt = jnp.dot(fb, wg_ref[0], preferred_element_type=F32)
    up = jnp.dot(fb, wu_ref[0], preferred_element_type=F32)
    lane = lax.broadcasted_iota(jnp.int32, (tm, 128), 1)
    cw = jnp.sum(jnp.where(lane == e, comb_sc[...], 0.0), axis=1, keepdims=True)
    act = gt * _sigmoid(gt) * up * cw
    acc_sc[...] += jnp.dot(act.astype(BF16), wd_ref[0], preferred_element_type=F32)

    @pl.when(e == n_e - 1)
    def _():
        y = h_ref[0] + mod_row(5) * acc_sc[...]
        if final_norm:
            y = y * lax.rsqrt(jnp.mean(y * y, -1, keepdims=True) + NORM_EPS) * fg_ref[...]
        o_ref[0] = y


def moe(h, n_lat, g, mod, rw, rb, wg, wu, wd, fg, final_norm):
    bsz, lo, d = h.shape
    ne = wg.shape[0]
    tm = next(t for t in MOE_BLKS if lo % t == 0)
    kern = functools.partial(_moe_kernel, final_norm=final_norm, n_lat=n_lat, has_ctx=lo > n_lat)
    return pl.pallas_call(
        kern,
        grid=(bsz, lo // tm, ne),
        in_specs=[pl.BlockSpec((1, tm, d), lambda b, i, e: (b, i, 0)),
                  pl.BlockSpec((1, d), lambda b, i, e: (0, 0)),
                  pl.BlockSpec((1, 2, 6, d), lambda b, i, e: (b, 0, 0, 0)),
                  pl.BlockSpec((N_EXPERTS, d), lambda b, i, e: (0, 0)),
                  pl.BlockSpec((N_EXPERTS, 1), lambda b, i, e: (0, 0)),
                  pl.BlockSpec((1, d, D_EXPERT), lambda b, i, e: (e, 0, 0)),
                  pl.BlockSpec((1, d, D_EXPERT), lambda b, i, e: (e, 0, 0)),
                  pl.BlockSpec((1, D_EXPERT, d), lambda b, i, e: (e, 0, 0)),
                  pl.BlockSpec((1, d), lambda b, i, e: (0, 0))],
        out_specs=pl.BlockSpec((1, tm, d), lambda b, i, e: (b, i, 0)),
        out_shape=jax.ShapeDtypeStruct((bsz, lo, d), F32),
        scratch_shapes=[pltpu.VMEM((tm, d), BF16),
                        pltpu.VMEM((tm, 128), F32),
                        pltpu.VMEM((tm, d), F32)],
        compiler_params=_cparams(("parallel", "parallel", "arbitrary")),
        name="moe",
    )(h, g, mod, rw, rb, wg, wu, wd, fg)


def _rope(x, ct, s1, s2):
    n = x.shape[1]
    return x * ct + pltpu.roll(x, n - ROPE_AXIS_DIMS, 1) * s1 + pltpu.roll(x, ROPE_AXIS_DIMS, 1) * s2


def _mla_proj_kernel(h_ref, g_ref, mod_ref, win_ref, qn_ref, qup_ref, kvn_ref, kup_ref, vup_ref, epl_ref,
                     one_ref, ct_ref, s1_ref, s2_ref, q_ref, k_ref, v_ref):
    m = mod_ref[0, 0]
    a = _norm_mod(h_ref[0], g_ref[...], m[0:1], m[1:2])
    p = jnp.dot(a.astype(BF16), win_ref[...], preferred_element_type=F32)
    qd = p[:, :Q_LORA]
    kvd = p[:, Q_LORA:Q_LORA + KV_LORA]
    kpe = p[:, Q_LORA + KV_LORA:]
    qn = qd * lax.rsqrt(jnp.mean(qd * qd, -1, keepdims=True) + NORM_EPS) * qn_ref[...]
    kvn = kvd * lax.rsqrt(jnp.mean(kvd * kvd, -1, keepdims=True) + NORM_EPS) * kvn_ref[...]
    kvb = kvn.astype(BF16)
    ct = jnp.concatenate([ct_ref[...]] * MLA_HEADS, axis=1)
    s1 = jnp.concatenate([s1_ref[...]] * MLA_HEADS, axis=1)
    s2 = jnp.concatenate([s2_ref[...]] * MLA_HEADS, axis=1)
    q = jnp.dot(qn.astype(BF16), qup_ref[...], preferred_element_type=F32)
    q_ref[0] = (_rope(q, ct, s1, s2) * MLA_SCALE).astype(BF16)
    k = (jnp.dot(kvb, kup_ref[...], preferred_element_type=F32)
         + jnp.dot(kpe.astype(BF16), epl_ref[...], preferred_element_type=F32))
    k_ref[0] = _rope(k, ct, s1, s2).astype(BF16)
    v = jnp.dot(kvb, vup_ref[...], preferred_element_type=F32) + one_ref[...]
    v_ref[0] = v.astype(BF16)


def mla_proj(h, n_lat, g, mod, win, qn, qup, kvn, kup, vup, epl, one, ct, s1, s2):
    bsz, lt, d = h.shape
    hp = MLA_HEADS * HEAD_PAD
    full = lambda shape: pl.BlockSpec(shape, lambda b, i: (0,) * len(shape))
    tab = pl.BlockSpec((ROW_BLK, HEAD_PAD), lambda b, i: (i, 0))
    out = pl.BlockSpec((1, ROW_BLK, hp), lambda b, i: (b, i, 0))
    return pl.pallas_call(
        _mla_proj_kernel,
        grid=(bsz, lt // ROW_BLK),
        in_specs=[pl.BlockSpec((1, ROW_BLK, d), lambda b, i: (b, i, 0)),
                  full((1, d)), _mod_spec(n_lat), full((d, MLA_IN)), full((1, Q_LORA)), full((Q_LORA, hp)),
                  full((1, KV_LORA)), full((KV_LORA, hp)), full((KV_LORA, hp)), full((QK_ROPE, hp)),
                  full((1, hp)), tab, tab, tab],
        out_specs=[out, out, out],
        out_shape=[jax.ShapeDtypeStruct((bsz, lt, hp), BF16)] * 3,
        compiler_params=_cparams(("parallel", "parallel")),
        name="mla_proj",
    )(h, g, mod, win, qn, qup, kvn, kup, vup, epl, one, ct, s1, s2)


def _attn_kernel(q_ref, k_ref, v_ref, o_ref, *, tk, nk):
    tq = q_ref.shape[1]
    nsub = tq // ATT_SUB
    qs = [q_ref[0, i * ATT_SUB:(i + 1) * ATT_SUB, :] for i in range(nsub)]

    def body(j, carry):
        ms, accs = carry
        start = pl.multiple_of(j * tk, 128)
        kc = k_ref[0, pl.ds(start, tk), :]
        vc = v_ref[0, pl.ds(start, tk), :]
        ss = [lax.dot_general(q, kc, (((1,), (1,)), ((), ())), preferred_element_type=F32) for q in qs]
        new_m, new_acc = [], []
        for s, m, acc in zip(ss, ms, accs):
            m_new = jnp.maximum(m, jnp.max(s, axis=1, keepdims=True))
            alpha = jnp.exp(m - m_new)
            p = jnp.exp(s - m_new).astype(BF16)
            new_acc.append(acc * alpha + jnp.dot(p, vc, preferred_element_type=F32))
            new_m.append(m_new)
        return tuple(new_m), tuple(new_acc)

    m0 = tuple(jnp.full((ATT_SUB, 1), -1e30, F32) for _ in range(nsub))
    acc0 = tuple(jnp.zeros((ATT_SUB, HEAD_PAD), F32) for _ in range(nsub))
    _, accs = lax.fori_loop(0, nk, body, (m0, acc0))
    for i, acc in enumerate(accs):
        o_ref[0, i * ATT_SUB:(i + 1) * ATT_SUB, :] = (acc / acc[:, V_DIM:V_DIM + 1]).astype(BF16)


def attention(q, k, v, lq):
    bsz, lt, hp = q.shape
    tq = next(t for t in ATT_TQS if lq % t == 0)
    tk = next(t for t in ATT_TKS if lt % t == 0)
    kern = functools.partial(_attn_kernel, tk=tk, nk=lt // tk)
    return pl.pallas_call(
        kern,
        grid=(bsz, MLA_HEADS, lq // tq),
        in_specs=[pl.BlockSpec((1, tq, HEAD_PAD), lambda b, h, i: (b, i, h)),
                  pl.BlockSpec((1, lt, HEAD_PAD), lambda b, h, i: (b, 0, h)),
                  pl.BlockSpec((1, lt, HEAD_PAD), lambda b, h, i: (b, 0, h))],
        out_specs=pl.BlockSpec((1, tq, HEAD_PAD), lambda b, h, i: (b, i, h)),
        out_shape=jax.ShapeDtypeStruct((bsz, lq, hp), BF16),
        compiler_params=_cparams(("parallel", "parallel", "parallel")),
        name="attention",
    )(q, k, v)


def _rope_tables(n_lat, n_ctx):
    rows = n_lat // GRID_W
    row = jnp.repeat(jnp.arange(rows, dtype=F32), GRID_W)
    col = jnp.tile(jnp.arange(GRID_W, dtype=F32), rows)
    inv_freq = ROPE_BASE ** (-jnp.arange(0, ROPE_AXIS_DIMS, 2, dtype=F32) / ROPE_AXIS_DIMS)
    ang = jnp.concatenate([row[:, None] * inv_freq, col[:, None] * inv_freq], -1)
    cos = jnp.concatenate([jnp.cos(ang), jnp.ones((n_ctx, ROPE_AXIS_DIMS), F32)], 0)
    sin = jnp.concatenate([jnp.sin(ang), jnp.zeros((n_ctx, ROPE_AXIS_DIMS), F32)], 0)
    lt = n_lat + n_ctx
    one = jnp.ones((lt, QK_NOPE), F32)
    z16 = jnp.zeros((lt, ROPE_AXIS_DIMS), F32)
    z32 = jnp.zeros((lt, HEAD_PAD - QK_NOPE - QK_ROPE), F32)
    z64 = jnp.zeros((lt, QK_NOPE), F32)
    ct = jnp.concatenate([one, cos, cos, z32], 1)
    s1 = jnp.concatenate([z64, -sin, z16, z32], 1)
    s2 = jnp.concatenate([z64, z16, sin, z32], 1)
    return ct, s1, s2


def _pad_heads(w, width, offset=0):
    k = w.shape[0]
    w = w.reshape(k, MLA_HEADS, width)
    w = jnp.pad(w, ((0, 0), (0, 0), (offset, HEAD_PAD - width - offset)))
    return w.reshape(k, MLA_HEADS * HEAD_PAD)


def mla_layer(h, n_ctx, g, mod, w_in, q_norm, q_up, kv_norm, kv_up, w_out):
    bsz, lt, d = h.shape
    n_lat = lt - n_ctx
    ct, s1, s2 = _rope_tables(n_lat, n_ctx)
    qup = _pad_heads(q_up, QK_NOPE + QK_ROPE).astype(BF16)
    kvu = kv_up.reshape(KV_LORA, MLA_HEADS, QK_NOPE + V_DIM)
    kup = _pad_heads(kvu[:, :, :QK_NOPE].reshape(KV_LORA, -1), QK_NOPE).astype(BF16)
    vup = _pad_heads(kvu[:, :, QK_NOPE:].reshape(KV_LORA, -1), V_DIM).astype(BF16)
    epl = _pad_heads(jnp.tile(jnp.eye(QK_ROPE, dtype=F32), (1, MLA_HEADS)), QK_ROPE, QK_NOPE).astype(BF16)
    one = _pad_heads(jnp.ones((1, MLA_HEADS), F32), 1, V_DIM)
    q, k, v = mla_proj(h, n_lat, g, mod, w_in.astype(BF16), q_norm[None], qup, kv_norm[None], kup, vup, epl, one,
                       ct, s1, s2)
    o = attention(q, k, v, n_lat)
    wo = w_out.reshape(MLA_HEADS, V_DIM, d)
    wo = jnp.pad(wo, ((0, 0), (0, HEAD_PAD - V_DIM), (0, 0))).reshape(MLA_HEADS * HEAD_PAD, d).astype(BF16)
    return outproj(o, wo, h, n_lat, mod)


def _inproj2_kernel(h_ref, g_ref, mod_ref, w_ref, p_ref, u_ref):
    m = mod_ref[0, 0]
    a = _norm_mod(h_ref[0], g_ref[...], m[0:1], m[1:2])
    o = jnp.dot(a.astype(BF16), w_ref[...], preferred_element_type=F32)
    p_ref[0] = o[:, :RWKV_IN]
    u_ref[...] = o[:, RWKV_IN:]


def inproj2(h, n_lat, g, mod, w):
    bsz, lt, d = h.shape
    return pl.pallas_call(
        _inproj2_kernel,
        grid=(bsz, lt // ROW_BLK),
        in_specs=[pl.BlockSpec((1, ROW_BLK, d), lambda b, i: (b, i, 0)),
                  pl.BlockSpec((1, d), lambda b, i: (0, 0)),
                  _mod_spec(n_lat),
                  pl.BlockSpec((d, HYB_IN), lambda b, i: (0, 0))],
        out_specs=[pl.BlockSpec((1, ROW_BLK, RWKV_IN), lambda b, i: (b, i, 0)),
                   pl.BlockSpec((ROW_BLK, S5_WIDTH), lambda b, i: (i, b))],
        out_shape=[jax.ShapeDtypeStruct((bsz, lt, RWKV_IN), F32),
                   jax.ShapeDtypeStruct((lt, bsz * S5_WIDTH), F32)],
        compiler_params=_cparams(("parallel", "parallel")),
        name="inproj2",
    )(h, g, mod, w)


def _tri_inverse_all(mats, idx_r, idx_c):
    c = mats[0].shape[0]
    eye = (idx_r == idx_c).astype(F32)
    blk8 = (idx_r >> 3) == (idx_c >> 3)
    ns = [jnp.where(blk8, -a, 0.0) for a in mats]
    ts = [eye + n for n in ns]
    n2 = [_dot(n, n) for n in ns]
    ts = [t + _dot(m, t) for m, t in zip(n2, ts)]
    n4 = [_dot(m, m) for m in n2]
    ts = [t + _dot(m, t) for m, t in zip(n4, ts)]
    sh = 3
    while (1 << sh) < c:
        off = jnp.logical_and((idx_r >> (sh + 1)) == (idx_c >> (sh + 1)), (idx_r >> sh) != (idx_c >> sh))
        ms = [_dot(jnp.where(off, a, 0.0), t) for a, t in zip(mats, ts)]
        ts = [t - _dot(t, m) for m, t in zip(ms, ts)]
        sh += 1
    return ts


def _rwkv_conv(pc, prev_row, next_row, cw):
    c = pc.shape[0]
    row = lax.broadcasted_iota(jnp.int32, (c, 1), 0)
    xm1 = jnp.where(row == 0, prev_row, pltpu.roll(pc, 1, 0))
    xp1 = jnp.where(row == c - 1, next_row, pltpu.roll(pc, c - 1, 0))
    return cw[0:1] * xm1 + cw[1:2] * pc + cw[2:3] * xp1


def _rwkv_chunk_inputs(x, z, w0, wup, a0, aup, kkv, kav):
    c = x.shape[0]
    n = RWKV_HEAD_DIM
    r = x[:, 0:RWKV_WIDTH]
    k = x[:, RWKV_WIDTH:2 * RWKV_WIDTH]
    v = x[:, 2 * RWKV_WIDTH:3 * RWKV_WIDTH]
    wd = x[:, 3 * RWKV_WIDTH:3 * RWKV_WIDTH + W_LORA]
    ad = x[:, 3 * RWKV_WIDTH + W_LORA:3 * RWKV_WIDTH + W_LORA + A_LORA]
    lw = -DECAY_SCALE * _sigmoid(w0 + _dot(jnp.tanh(wd), wup, True))
    a = _sigmoid(a0 + _dot(ad, aup, True))
    idx_r = lax.broadcasted_iota(jnp.int32, (c, c), 0)
    idx_c = lax.broadcasted_iota(jnp.int32, (c, c), 1)
    if z == 0:
        incl = idx_c <= idx_r
        strict = idx_c < idx_r
        last = c - 1
    else:
        incl = idx_c >= idx_r
        strict = idx_c > idx_r
        last = 0
    cum = _dot(incl.astype(F32), lw, True)
    tot = cum[last:last + 1]
    e1 = jnp.exp(cum)
    e2 = jnp.exp(cum - lw)
    e3 = jnp.exp(-cum)
    e4 = jnp.exp(tot - cum)
    etot = jnp.exp(tot)
    kkf = k * kkv
    heads = []
    for h in range(RWKV_HEADS):
        sl = slice(h * n, (h + 1) * n)
        kk = kkf[:, sl]
        kk = kk * lax.rsqrt(jnp.sum(kk * kk, -1, keepdims=True) + 1e-12)
        ah = a[:, sl]
        bh = ah * kk
        kd = k[:, sl] * (1.0 + (ah - 1.0) * kav[:, sl])
        heads.append(dict(
            xq=jnp.concatenate([kk * e2[:, sl], r[:, sl] * e1[:, sl]], 0),
            yk=jnp.concatenate([kd * e3[:, sl], bh * e3[:, sl]], 0),
            ke=jnp.concatenate([kd * e4[:, sl], bh * e4[:, sl]], 0),
            v=v[:, sl], etot=etot[:, sl], incl=incl, strict=strict, state=(z, h)))
    return heads


def _rwkv_solve_all(heads, s_sc):
    c = heads[0]['v'].shape[0]
    idx_r = lax.broadcasted_iota(jnp.int32, (c, c), 0)
    idx_c = lax.broadcasted_iota(jnp.int32, (c, c), 1)
    gs = [_dot_nt(hd['xq'], hd['yk']) for hd in heads]
    ss = [s_sc[hd['state'][0], hd['state'][1]] for hd in heads]
    xs = [_dot_nt(hd['xq'], s) for hd, s in zip(heads, ss)]
    avs = [_dot(jnp.where(hd['strict'], g[:c, :c], 0.0), hd['v']) for hd, g in zip(heads, gs)]
    ts = _tri_inverse_all([jnp.where(hd['strict'], g[:c, c:], 0.0) for hd, g in zip(heads, gs)], idx_r, idx_c)
    sas = [_dot(t, x[:c] + av) for t, x, av in zip(ts, xs, avs)]
    outs = []
    for hd, g, x, sa, s in zip(heads, gs, xs, sas, ss):
        a_r = jnp.concatenate([jnp.where(hd['incl'], g[c:, :c], 0.0), jnp.where(hd['incl'], -g[c:, c:], 0.0)], 1)
        outs.append(x[c:] + _dot(a_r, jnp.concatenate([hd['v'], sa], 0)))
        z, h = hd['state']
        s_sc[z, h] = s * hd['etot'] + _dot_tn(jnp.concatenate([hd['v'], -sa], 0), hd['ke'])
    return outs


def _rwkv_kernel(pa_ref, pap_ref, pan_ref, pb_ref, pbp_ref, pbn_ref, cw_ref, w0_ref, wup_ref, a0_ref, aup_ref,
                 gup_ref, kk_ref, ka_ref, rk_ref, y0_ref, y1_ref, bonus_ref, g_ref, s_sc, *, nl, nc):
    s = pl.program_id(1)

    @pl.when(s == 0)
    def _():
        s_sc[...] = jnp.zeros_like(s_sc)

    ca = (s + nl) % nc
    cb = nc - 1 - s
    cw = cw_ref[...]

    def load(p_ref, pp_ref, pn_ref, cidx):
        first = jnp.logical_or(cidx == 0, cidx == nl)
        lastc = jnp.logical_or(cidx == nl - 1, cidx == nc - 1)
        prev_row = jnp.where(first, 0.0, pp_ref[0, 7:8, :])
        next_row = jnp.where(lastc, 0.0, pn_ref[0, 0:1, :])
        return _rwkv_conv(p_ref[0], prev_row, next_row, cw)

    xa = load(pa_ref, pap_ref, pan_ref, ca)
    xb = load(pb_ref, pbp_ref, pbn_ref, cb)
    heads = (_rwkv_chunk_inputs(xa, 0, w0_ref[0:1], wup_ref[0], a0_ref[0:1], aup_ref[0], kk_ref[...], ka_ref[...])
             + _rwkv_chunk_inputs(xb, 1, w0_ref[1:2], wup_ref[1], a0_ref[1:2], aup_ref[1], kk_ref[...], ka_ref[...]))
    outs = _rwkv_solve_all(heads, s_sc)
    y0_ref[0] = jnp.concatenate(outs[:RWKV_HEADS], 1)
    y1_ref[0] = jnp.concatenate(outs[RWKV_HEADS:], 1)
    gd = xa[:, 3 * RWKV_WIDTH + W_LORA + A_LORA:]
    g_ref[0] = _dot(_sigmoid(gd), gup_ref[...], True)
    r = xa[:, 0:RWKV_WIDTH]
    k = xa[:, RWKV_WIDTH:2 * RWKV_WIDTH]
    v = xa[:, 2 * RWKV_WIDTH:3 * RWKV_WIDTH]
    rkr = r * k * rk_ref[...]
    bon = []
    for h in range(RWKV_HEADS):
        sl = slice(h * RWKV_HEAD_DIM, (h + 1) * RWKV_HEAD_DIM)
        bon.append(jnp.sum(rkr[:, sl], -1, keepdims=True) * v[:, sl])
    bonus_ref[0] = jnp.concatenate(bon, 1)


def rwkv_scan(p, n_lat, cw, w0, wup, a0, aup, gup, kkv, kav, rk):
    bsz, lt, _ = p.shape
    nl = n_lat // CHUNK
    nc = lt // CHUNK
    nb8 = lt // 8
    ca = lambda s: (s + nl) % nc
    cb = lambda s: nc - 1 - s
    pspec = lambda cf: pl.BlockSpec((1, CHUNK, RWKV_IN), lambda b, s: (b, cf(s), 0))
    prev = lambda cf: pl.BlockSpec((1, 8, RWKV_IN), lambda b, s: (b, jnp.maximum(cf(s) * (CHUNK // 8) - 1, 0), 0))
    nxt = lambda cf: pl.BlockSpec((1, 8, RWKV_IN),
                                  lambda b, s: (b, jnp.minimum((cf(s) + 1) * (CHUNK // 8), nb8 - 1), 0))
    full = lambda a: pl.BlockSpec(a.shape, lambda b, s: (0,) * a.ndim)
    ospec = lambda cf: pl.BlockSpec((1, CHUNK, RWKV_WIDTH), lambda b, s: (b, cf(s), 0))
    oshape = jax.ShapeDtypeStruct((bsz, lt, RWKV_WIDTH), F32)
    kern = functools.partial(_rwkv_kernel, nl=nl, nc=nc)
    params = (cw, w0, wup, a0, aup, gup, kkv, kav, rk)
    return pl.pallas_call(
        kern,
        grid=(bsz, nc),
        in_specs=[pspec(ca), prev(ca), nxt(ca), pspec(cb), prev(cb), nxt(cb)] + [full(a) for a in params],
        out_specs=[ospec(ca), ospec(cb), ospec(ca), ospec(ca)],
        out_shape=[oshape] * 4,
        scratch_shapes=[pltpu.VMEM((2, RWKV_HEADS, RWKV_HEAD_DIM, RWKV_HEAD_DIM), F32)],
        compiler_params=_cparams(("parallel", "arbitrary")),
        name="rwkv_scan",
    )(p, p, p, p, p, p, *params)


def _s5_kernel(u_ref, lam_ref, bf_ref, cf_ref, y_ref, bu_sc, h_sc, *, reverse):
    s = pl.program_id(0)

    @pl.when(s == 0)
    def _():
        h_sc[...] = jnp.zeros_like(h_sc)

    rows = bu_sc.shape[0]
    bu_sc[...] = jnp.dot(u_ref[...].astype(BF16), bf_ref[...], preferred_element_type=F32)
    lr = lam_ref[0:1, :]
    li = lam_ref[1:2, :]
    ntile = rows // 8
    row = lax.broadcasted_iota(jnp.int32, (8, 1), 0)
    first_half = (row >= 4) if reverse else (row < 4)

    def body(j, carry):
        hr, hi = carry
        jj = (ntile - 1 - j) if reverse else j
        start = pl.multiple_of(jj * 8, 8)
        xr = bu_sc[pl.ds(start, 8), pl.ds(0, S5_N)]
        xi = bu_sc[pl.ds(start, 8), pl.ds(S5_N, S5_N)]
        ar, ai = lr * hr - li * hi + xr, lr * hi + li * hr + xi
        sr, si = pltpu.roll(ar, 4, 0), pltpu.roll(ai, 4, 0)
        br, bi = lr * sr - li * si + xr, lr * si + li * sr + xi
        bu_sc[pl.ds(start, 8), pl.ds(0, S5_N)] = jnp.where(first_half, ar, br)
        bu_sc[pl.ds(start, 8), pl.ds(S5_N, S5_N)] = jnp.where(first_half, ai, bi)
        return pltpu.roll(br, 4, 0), pltpu.roll(bi, 4, 0)

    hr, hi = lax.fori_loop(0, ntile, body, (h_sc[0], h_sc[1]))
    h_sc[0] = hr
    h_sc[1] = hi
    y_ref[...] = jnp.dot(bu_sc[...].astype(BF16), cf_ref[...], preferred_element_type=F32)


def s5_scan(u, n_lat, bsz, lam, bfull, cfull, reverse):
    assert bsz == 4
    rows_total = u.shape[0]
    lt = rows_total // bsz
    nl = n_lat // CHUNK
    nc = lt // CHUNK
    rb = CHUNK * bsz
    cidx = (lambda s: nc - 1 - s) if reverse else (lambda s: (s + nl) % nc)
    kern = functools.partial(_s5_kernel, reverse=reverse)
    return pl.pallas_call(
        kern,
        grid=(nc,),
        in_specs=[pl.BlockSpec((rb, S5_WIDTH), lambda s: (cidx(s), 0)),
                  pl.BlockSpec((2, S5_N), lambda s: (0, 0)),
                  pl.BlockSpec((S5_WIDTH, 2 * S5_N), lambda s: (0, 0)),
                  pl.BlockSpec((2 * S5_N, S5_WIDTH), lambda s: (0, 0))],
        out_specs=pl.BlockSpec((rb, S5_WIDTH), lambda s: (cidx(s), 0)),
        out_shape=jax.ShapeDtypeStruct((rows_total, S5_WIDTH), F32),
        scratch_shapes=[pltpu.VMEM((rb, 2 * S5_N), F32), pltpu.VMEM((2, 8, S5_N), F32)],
        compiler_params=_cparams(("arbitrary",)),
        name="s5_rev" if reverse else "s5_fwd",
    )(u, lam, bfull, cfull)


def _s5_params(lam_re, lam_im, log_dt, b_re, b_im, c_re, c_im):
    dt = jnp.exp(log_dt.astype(F32))[..., None]
    mag = jnp.exp(lam_re * dt)
    lbr = mag * jnp.cos(lam_im * dt)
    lbi = mag * jnp.sin(lam_im * dt)
    den = lam_re * lam_re + lam_im * lam_im
    nr, ni = lbr - 1.0, lbi
    fr = (nr * lam_re + ni * lam_im) / den
    fi = (ni * lam_re - nr * lam_im) / den
    bbr = fr[..., None] * b_re - fi[..., None] * b_im
    bbi = fr[..., None] * b_im + fi[..., None] * b_re
    eye = jnp.eye(S5_GROUPS, dtype=F32)

    def bd_in(w):
        return jnp.einsum('gpc,gh->gchp', w, eye).reshape(S5_WIDTH, S5_N)

    def bd_out(w):
        return jnp.einsum('gcp,gh->gphc', w, eye).reshape(S5_N, S5_WIDTH)

    lam = jnp.stack([lbr.reshape(2, S5_N), lbi.reshape(2, S5_N)], 1)
    bfull = jnp.stack([jnp.concatenate([bd_in(bbr[z]), bd_in(bbi[z])], 1) for z in range(2)])
    cfull = jnp.stack([jnp.concatenate([bd_out(c_re[z]), -bd_out(c_im[z])], 0) for z in range(2)])
    return lam, bfull.astype(BF16), cfull.astype(BF16)


def _gelu(x):
    return 0.5 * x * (1.0 + jnp.tanh(math.sqrt(2.0 / math.pi) * (x + 0.044715 * x * x * x)))


def _merge_kernel(y0_ref, y1_ref, bonus_ref, g_ref, s0_ref, s1_ref, u_ref, h_ref, mod_ref, lnw_ref, lnb_ref,
                  d_ref, gluw_ref, glub_ref, wout_ref, o_ref):
    y = y0_ref[0] + y1_ref[0]
    n = RWKV_HEAD_DIM
    parts = []
    for h in range(RWKV_HEADS):
        yh = y[:, h * n:(h + 1) * n]
        mu = jnp.mean(yh, -1, keepdims=True)
        dlt = yh - mu
        var = jnp.mean(dlt * dlt, -1, keepdims=True)
        parts.append(dlt * lax.rsqrt(var + GN_EPS))
    yn = jnp.concatenate(parts, 1) * lnw_ref[...] + lnb_ref[...]
    rw = (yn + bonus_ref[0]) * g_ref[0]
    ys = s0_ref[...] + s1_ref[...] + d_ref[...] * u_ref[...]
    zz = _gelu(ys)
    gate = _sigmoid(_dot(zz, gluw_ref[...]) + glub_ref[...])
    cat = jnp.concatenate([rw, zz * gate], 1).astype(BF16)
    m = mod_ref[0, 0]
    o_ref[0] = h_ref[0] + m[2:3] * jnp.dot(cat, wout_ref[...], preferred_element_type=F32)


def merge(y0, y1, bonus, g, s0, s1, u, h, n_lat, mod, lnw, lnb, dvec, gluw, glub, wout):
    bsz, lt, d = h.shape
    rspec = pl.BlockSpec((1, ROW_BLK, RWKV_WIDTH), lambda b, i: (b, i, 0))
    sspec = pl.BlockSpec((ROW_BLK, S5_WIDTH), lambda b, i: (i, b))
    full = lambda a: pl.BlockSpec(a.shape, lambda b, i: (0,) * a.ndim)
    params = (lnw, lnb, dvec, gluw, glub, wout)
    return pl.pallas_call(
        _merge_kernel,
        grid=(bsz, lt // ROW_BLK),
        in_specs=[rspec, rspec, rspec, rspec, sspec, sspec, sspec,
                  pl.BlockSpec((1, ROW_BLK, d), lambda b, i: (b, i, 0)), _mod_spec(n_lat)]
                 + [full(a) for a in params],
        out_specs=pl.BlockSpec((1, ROW_BLK, d), lambda b, i: (b, i, 0)),
        out_shape=jax.ShapeDtypeStruct((bsz, lt, d), F32),
        compiler_params=_cparams(("parallel", "parallel")),
        name="merge",
    )(y0, y1, bonus, g, s0, s1, u, h, mod, *params)


def hybrid_layer(h, n_lat, g1, mod, w_in, w_out, conv_w, w0, w_up, a0, a_up, g_up, k_k, k_a, r_k, ln_w, ln_b,
                 lam_re, lam_im, log_dt, b_re, b_im, c_re, c_im, dvec, glu_w, glu_b):
    bsz = h.shape[0]
    p, u = inproj2(h, n_lat, g1, mod, w_in.astype(BF16))
    y0, y1, bonus, g = rwkv_scan(p, n_lat, conv_w, w0, w_up, a0, a_up, g_up, k_k[None], k_a[None],
                                 r_k.reshape(1, RWKV_WIDTH))
    lam, bfull, cfull = _s5_params(lam_re, lam_im, log_dt, b_re, b_im, c_re, c_im)
    u2 = u.reshape(-1, S5_WIDTH)
    s0 = s5_scan(u2, n_lat, bsz, lam[0], bfull[0], cfull[0], False).reshape(u.shape)
    s1 = s5_scan(u2, n_lat, bsz, lam[1], bfull[1], cfull[1], True).reshape(u.shape)
    gluw = jnp.einsum('gce,gh->gche', glu_w, jnp.eye(S5_GROUPS, dtype=F32)).reshape(S5_WIDTH, S5_WIDTH)
    return merge(y0, y1, bonus, g, s0, s1, u, h, n_lat, mod, ln_w[None], ln_b[None], dvec[None],
                 gluw.astype(BF16), glu_b[None], w_out.astype(BF16))


def _stack_experts(ex, sh):
    return jnp.concatenate([ex, sh[None]], 0).astype(BF16)


def kernel(x, c, ctx, c_ctx, mod_w, mod_b, norm1_g, norm2_g, final_g, hy_w_in, hy_w_out, rk_conv, rk_w0, rk_w_up, rk_a0, rk_a_up, rk_g_up, rk_k_k, rk_k_a, rk_r_k, rk_ln_w, rk_ln_b, s5_lam_re, s5_lam_im, s5_log_dt, s5_b_re, s5_b_im, s5_c_re, s5_c_im, s5_d, s5_glu_w, s5_glu_b, mla_w_in, mla_q_norm, mla_q_up, mla_kv_norm, mla_kv_up, mla_w_out, router_w, router_b, ex_gate, ex_up, ex_down, sh_gate, sh_up, sh_down):
    bsz, n_lat, d = x.shape
    n_ctx = ctx.shape[1]
    depth = mod_w.shape[0]
    assert n_ctx == ROW_BLK and depth == 2 and bsz <= 7
    h = jnp.concatenate([x, ctx], 1)
    cc = jnp.concatenate([c, c_ctx[None], jnp.zeros((8 - bsz - 1, d), F32)], 0)
    rb = router_b[:, None].astype(F32)
    rw_t = router_w.T
    fg = final_g[None]
    for layer in range(depth):
        last = layer == depth - 1
        i = layer // 2
        mv = adaln(cc, mod_w[layer], mod_b[layer][None])
        m_l = mv[:bsz].reshape(bsz, 1, 6, d)
        m_c = jnp.broadcast_to(mv[bsz].reshape(1, 1, 6, d), (bsz, 1, 6, d))
        mod = jnp.concatenate([m_l, m_c], 1)
        g1 = norm1_g[layer][None]
        g2 = norm2_g[layer][None]
        if layer % 2 == 0:
            h = hybrid_layer(h, n_lat, g1, mod, hy_w_in[i], hy_w_out[i], rk_conv[i], rk_w0[i], rk_w_up[i], rk_a0[i],
                             rk_a_up[i], rk_g_up[i], rk_k_k[i], rk_k_a[i], rk_r_k[i], rk_ln_w[i], rk_ln_b[i],
                             s5_lam_re[i], s5_lam_im[i], s5_log_dt[i], s5_b_re[i], s5_b_im[i], s5_c_re[i],
                             s5_c_im[i], s5_d[i], s5_glu_w[i], s5_glu_b[i])
        else:
            h = mla_layer(h, n_ctx, g1, mod, mla_w_in[i], mla_q_norm[i], mla_q_up[i], mla_kv_norm[i],
                          mla_kv_up[i], mla_w_out[i])
        wg = _stack_experts(ex_gate[layer], sh_gate[layer])
        wu = _stack_experts(ex_up[layer], sh_up[layer])
        wd = _stack_experts(ex_down[layer], sh_down[layer])
        h = moe(h, n_lat, g2, mod, rw_t, rb, wg, wu, wd, fg, last)
    return h
```

```python
import functools
import math

import jax
import jax.numpy as jnp
from jax import lax
from jax.experimental import pallas as pl
from jax.experimental.pallas import tpu as pltpu

F32 = jnp.float32
BF16 = jnp.bfloat16
HI = lax.Precision.HIGHEST

D_MODEL = 1024
GRID_W = 64
NORM_EPS = 1e-6

RWKV_HEADS = 8
RWKV_HEAD_DIM = 64
RWKV_WIDTH = 512
W_LORA = 64
A_LORA = 64
G_LORA = 128
RWKV_IN = 3 * RWKV_WIDTH + W_LORA + A_LORA + G_LORA
DECAY_SCALE = math.exp(-0.5)
GN_EPS = 64e-5

S5_GROUP = 16
S5_GROUPS = 32
S5_WIDTH = 512
S5_STATE = 64
S5_N = S5_GROUPS * S5_STATE
S5_HALF = S5_N // 2
HYB_IN = RWKV_IN + S5_WIDTH

MLA_HEADS = 16
Q_LORA = 256
KV_LORA = 128
QK_NOPE = 64
QK_ROPE = 32
V_DIM = 64
MLA_IN = Q_LORA + KV_LORA + QK_ROPE
MLA_SCALE = (QK_NOPE + QK_ROPE) ** -0.5
ROPE_AXIS_DIMS = QK_ROPE // 2
ROPE_BASE = 10000.0
HEAD_PAD = 128

N_EXPERTS = 16
N_GROUPS = 4
EXPERTS_PER_GROUP = 4
D_EXPERT = 256

ROW_BLK = 256
MOE_BLKS = (1024, 768, 512, 256)
ATT_TQS = (1024, 512, 256)
ATT_TKS = (1408, 768, 256)
ATT_SUB = 256
CHUNK = 64
RWKV_NB = 2
VMEM_LIMIT = 56 * 1024 * 1024


def _cparams(sem):
    return pltpu.CompilerParams(dimension_semantics=sem, vmem_limit_bytes=VMEM_LIMIT)


def _norm_mod(x, g, shift, scale):
    y = x * lax.rsqrt(jnp.mean(x * x, -1, keepdims=True) + NORM_EPS) * g
    return y * (1.0 + scale) + shift


def _sigmoid(x):
    return 1.0 / (1.0 + jnp.exp(-x))


def _dot(a, b, hi=False):
    if hi:
        return jnp.dot(a, b, precision=HI, preferred_element_type=F32)
    return jnp.dot(a.astype(BF16), b.astype(BF16), preferred_element_type=F32)


def _dot_nt(a, b):
    return lax.dot_general(a.astype(BF16), b.astype(BF16), (((1,), (1,)), ((), ())), preferred_element_type=F32)


def _dot_tn(a, b):
    return jnp.dot(a.T.astype(BF16), b.astype(BF16), preferred_element_type=F32)


def _adaln_kernel(s_ref, w_ref, b_ref, o_ref):
    s = s_ref[...]
    s = s * _sigmoid(s)
    o_ref[...] = jnp.dot(s, w_ref[...], precision=HI, preferred_element_type=F32) + b_ref[...]


def adaln(cc, w, b):
    n = w.shape[1]
    tn = 512
    return pl.pallas_call(
        _adaln_kernel,
        grid=(n // tn,),
        in_specs=[pl.BlockSpec((8, D_MODEL), lambda j: (0, 0)),
                  pl.BlockSpec((D_MODEL, tn), lambda j: (0, j)),
                  pl.BlockSpec((1, tn), lambda j: (0, j))],
        out_specs=pl.BlockSpec((8, tn), lambda j: (0, j)),
        out_shape=jax.ShapeDtypeStruct((8, n), F32),
        compiler_params=_cparams(("parallel",)),
        name="adaln",
    )(cc, w, b)


def _mod_spec(n_lat, blk=ROW_BLK):
    return pl.BlockSpec((1, 1, 6, D_MODEL), lambda b, i, *_: (b, i // (n_lat // blk), 0, 0))


def _outproj_kernel(a_ref, w_ref, h_ref, mod_ref, o_ref):
    m = mod_ref[0, 0]
    o = jnp.dot(a_ref[0].astype(BF16), w_ref[...], preferred_element_type=F32)
    o_ref[0] = h_ref[0] + m[2:3] * o


def outproj(a, w, h, n_lat, mod):
    bsz, la, k = a.shape
    d = h.shape[2]
    return pl.pallas_call(
        _outproj_kernel,
        grid=(bsz, la // ROW_BLK),
        in_specs=[pl.BlockSpec((1, ROW_BLK, k), lambda b, i: (b, i, 0)),
                  pl.BlockSpec((k, d), lambda b, i: (0, 0)),
                  pl.BlockSpec((1, ROW_BLK, d), lambda b, i: (b, i, 0)),
                  _mod_spec(n_lat)],
        out_specs=pl.BlockSpec((1, ROW_BLK, d), lambda b, i: (b, i, 0)),
        out_shape=jax.ShapeDtypeStruct((bsz, la, d), F32),
        compiler_params=_cparams(("parallel", "parallel")),
        name="outproj",
    )(a, w, h, mod)


def _route(scores, rb):
    t = scores.shape[1]
    biased = scores + rb
    col = [biased[e:e + 1, :] for e in range(N_EXPERTS)]
    sc = [scores[e:e + 1, :] for e in range(N_EXPERTS)]
    gscore = []
    for gi in range(N_GROUPS):
        a, b, c, d = col[4 * gi:4 * gi + 4]
        hi1, lo1 = jnp.maximum(a, b), jnp.minimum(a, b)
        hi2, lo2 = jnp.maximum(c, d), jnp.minimum(c, d)
        gscore.append(jnp.maximum(hi1, hi2) + jnp.maximum(jnp.minimum(hi1, hi2), jnp.maximum(lo1, lo2)))
    gsel = []
    taken = None
    for gi in range(N_GROUPS):
        best = None
        for gj in range(gi + 1, N_GROUPS):
            best = gscore[gj] if best is None else jnp.maximum(best, gscore[gj])
        s = (gscore[gi] >= best) if best is not None else jnp.full((1, t), True)
        if taken is not None:
            s = jnp.logical_and(s, jnp.logical_not(taken))
        taken = s if taken is None else jnp.logical_or(taken, s)
        gsel.append(s)
    masks = []
    for gi in range(N_GROUPS):
        v = col[4 * gi:4 * gi + 4]
        for j in range(4):
            rank = jnp.zeros((1, t), F32)
            for i in range(4):
                if i == j:
                    continue
                ahead = (v[i] >= v[j]) if i < j else (v[i] > v[j])
                rank = rank + ahead.astype(F32)
            masks.append(jnp.logical_and(gsel[gi], rank < 2.0))
    wsel = [jnp.where(masks[e], sc[e], 0.0) for e in range(N_EXPERTS)]
    denom = wsel[0]
    for e in range(1, N_EXPERTS):
        denom = denom + wsel[e]
    return [w / denom for w in wsel]


def _moe_kernel(h_ref, g_ref, mod_ref, rw_ref, rb_ref, wg_ref, wu_ref, wd_ref, fg_ref, o_ref,
                f_sc, comb_sc, acc_sc, *, final_norm, n_lat, has_ctx):
    i = pl.program_id(1)
    e = pl.program_id(2)
    n_e = pl.num_programs(2)
    tm = f_sc.shape[0]

    def mod_row(k):
        lat = mod_ref[0, 0, k:k + 1, :]
        if not has_ctx:
            return lat
        row = i * tm + lax.broadcasted_iota(jnp.int32, (tm, 1), 0)
        return jnp.where(row >= n_lat, mod_ref[0, 1, k:k + 1, :], lat)

    @pl.when(e == 0)
    def _():
        f = _norm_mod(h_ref[0], g_ref[...], mod_row(3), mod_row(4))
        f_sc[...] = f.astype(BF16)
        logits = lax.dot_general(rw_ref[...], f, (((1,), (1,)), ((), ())), precision=HI,
                                 preferred_element_type=F32)
        cw = _route(_sigmoid(logits), rb_ref[...])
        sub = lax.broadcasted_iota(jnp.int32, (128, tm), 0)
        comb = jnp.where(sub == N_EXPERTS, 1.0, 0.0)
        for ei in range(N_EXPERTS):
            comb = jnp.where(sub == ei, cw[ei], comb)
        comb_sc[...] = comb.T
        acc_sc[...] = jnp.zeros_like(acc_sc)

    fb = f_sc[...]
    gt = jnp.dot(fb, wg_ref[0], preferred_element_type=F32)
    up = jnp.dot(fb, wu_ref[0], preferred_element_type=F32)
    lane = lax.broadcasted_iota(jnp.int32, (tm, 128), 1)
    cw = jnp.sum(jnp.where(lane == e, comb_sc[...], 0.0), axis=1, keepdims=True)
    act = gt * _sigmoid(gt) * up * cw
    acc_sc[...] += jnp.dot(act.astype(BF16), wd_ref[0], preferred_element_type=F32)

    @pl.when(e == n_e - 1)
    def _():
        y = h_ref[0] + mod_row(5) * acc_sc[...]
        if final_norm:
            y = y * lax.rsqrt(jnp.mean(y * y, -1, keepdims=True) + NORM_EPS) * fg_ref[...]
        o_ref[0] = y


def moe(h, n_lat, g, mod, rw, rb, wg, wu, wd, fg, final_norm):
    bsz, lo, d = h.shape
    ne = wg.shape[0]
    tm = next(t for t in MOE_BLKS if lo % t == 0)
    kern = functools.partial(_moe_kernel, final_norm=final_norm, n_lat=n_lat, has_ctx=lo > n_lat)
    return pl.pallas_call(
        kern,
        grid=(bsz, lo // tm, ne),
        in_specs=[pl.BlockSpec((1, tm, d), lambda b, i, e: (b, i, 0)),
                  pl.BlockSpec((1, d), lambda b, i, e: (0, 0)),
                  pl.BlockSpec((1, 2, 6, d), lambda b, i, e: (b, 0, 0, 0)),
                  pl.BlockSpec((N_EXPERTS, d), lambda b, i, e: (0, 0)),
                  pl.BlockSpec((N_EXPERTS, 1), lambda b, i, e: (0, 0)),
                  pl.BlockSpec((1, d, D_EXPERT), lambda b, i, e: (e, 0, 0)),
                  pl.BlockSpec((1, d, D_EXPERT), lambda b, i, e: (e, 0, 0)),
                  pl.BlockSpec((1, D_EXPERT, d), lambda b, i, e: (e, 0, 0)),
                  pl.BlockSpec((1, d), lambda b, i, e: (0, 0))],
        out_specs=pl.BlockSpec((1, tm, d), lambda b, i, e: (b, i, 0)),
        out_shape=jax.ShapeDtypeStruct((bsz, lo, d), F32),
        scratch_shapes=[pltpu.VMEM((tm, d), BF16),
                        pltpu.VMEM((tm, 128), F32),
                        pltpu.VMEM((tm, d), F32)],
        compiler_params=_cparams(("parallel", "parallel", "arbitrary")),
        name="moe",
    )(h, g, mod, rw, rb, wg, wu, wd, fg)


def _rope(x, ct, s1, s2):
    n = x.shape[1]
    return x * ct + pltpu.roll(x, n - ROPE_AXIS_DIMS, 1) * s1 + pltpu.roll(x, ROPE_AXIS_DIMS, 1) * s2


def _mla_proj_kernel(h_ref, g_ref, mod_ref, win_ref, qn_ref, qup_ref, kvn_ref, kup_ref, vup_ref, epl_ref,
                     one_ref, ct_ref, s1_ref, s2_ref, q_ref, k_ref, v_ref):
    m = mod_ref[0, 0]
    a = _norm_mod(h_ref[0], g_ref[...], m[0:1], m[1:2])
    p = jnp.dot(a.astype(BF16), win_ref[...], preferred_element_type=F32)
    qd = p[:, :Q_LORA]
    kvd = p[:, Q_LORA:Q_LORA + KV_LORA]
    kpe = p[:, Q_LORA + KV_LORA:]
    qn = qd * lax.rsqrt(jnp.mean(qd * qd, -1, keepdims=True) + NORM_EPS) * qn_ref[...]
    kvn = kvd * lax.rsqrt(jnp.mean(kvd * kvd, -1, keepdims=True) + NORM_EPS) * kvn_ref[...]
    kvb = kvn.astype(BF16)
    ct = jnp.concatenate([ct_ref[...]] * MLA_HEADS, axis=1)
    s1 = jnp.concatenate([s1_ref[...]] * MLA_HEADS, axis=1)
    s2 = jnp.concatenate([s2_ref[...]] * MLA_HEADS, axis=1)
    q = jnp.dot(qn.astype(BF16), qup_ref[...], preferred_element_type=F32)
    q_ref[0] = (_rope(q, ct, s1, s2) * MLA_SCALE).astype(BF16)
    k = (jnp.dot(kvb, kup_ref[...], preferred_element_type=F32)
         + jnp.dot(kpe.astype(BF16), epl_ref[...], preferred_element_type=F32))
    k_ref[0] = _rope(k, ct, s1, s2).astype(BF16)
    v = jnp.dot(kvb, vup_ref[...], preferred_element_type=F32) + one_ref[...]
    v_ref[0] = v.astype(BF16)


def mla_proj(h, n_lat, g, mod, win, qn, qup, kvn, kup, vup, epl, one, ct, s1, s2):
    bsz, lt, d = h.shape
    hp = MLA_HEADS * HEAD_PAD
    full = lambda shape: pl.BlockSpec(shape, lambda b, i: (0,) * len(shape))
    tab = pl.BlockSpec((ROW_BLK, HEAD_PAD), lambda b, i: (i, 0))
    out = pl.BlockSpec((1, ROW_BLK, hp), lambda b, i: (b, i, 0))
    return pl.pallas_call(
        _mla_proj_kernel,
        grid=(bsz, lt // ROW_BLK),
        in_specs=[pl.BlockSpec((1, ROW_BLK, d), lambda b, i: (b, i, 0)),
                  full((1, d)), _mod_spec(n_lat), full((d, MLA_IN)), full((1, Q_LORA)), full((Q_LORA, hp)),
                  full((1, KV_LORA)), full((KV_LORA, hp)), full((KV_LORA, hp)), full((QK_ROPE, hp)),
                  full((1, hp)), tab, tab, tab],
        out_specs=[out, out, out],
        out_shape=[jax.ShapeDtypeStruct((bsz, lt, hp), BF16)] * 3,
        compiler_params=_cparams(("parallel", "parallel")),
        name="mla_proj",
    )(h, g, mod, win, qn, qup, kvn, kup, vup, epl, one, ct, s1, s2)


def _attn_kernel(q_ref, k_ref, v_ref, o_ref, *, tk, nk):
    tq = q_ref.shape[1]
    nsub = tq // ATT_SUB
    qs = [q_ref[0, i * ATT_SUB:(i + 1) * ATT_SUB, :] for i in range(nsub)]

    def scores(j):
        kc = k_ref[0, j * tk:(j + 1) * tk, :]
        return [lax.dot_general(q, kc, (((1,), (1,)), ((), ())), preferred_element_type=F32) for q in qs]

    ms = [jnp.full((ATT_SUB, 1), -1e30, F32) for _ in range(nsub)]
    accs = [jnp.zeros((ATT_SUB, HEAD_PAD), F32) for _ in range(nsub)]
    ss = scores(0)
    for j in range(nk):
        ss_next = scores(j + 1) if j + 1 < nk else None
        vc = v_ref[0, j * tk:(j + 1) * tk, :]
        for i in range(nsub):
            m_new = jnp.maximum(ms[i], jnp.max(ss[i], axis=1, keepdims=True))
            alpha = jnp.exp(ms[i] - m_new)
            p = jnp.exp(ss[i] - m_new).astype(BF16)
            accs[i] = accs[i] * alpha + jnp.dot(p, vc, preferred_element_type=F32)
            ms[i] = m_new
        ss = ss_next
    for i, acc in enumerate(accs):
        o_ref[0, i * ATT_SUB:(i + 1) * ATT_SUB, :] = (acc / acc[:, V_DIM:V_DIM + 1]).astype(BF16)


def attention(q, k, v, lq):
    bsz, lt, hp = q.shape
    tq = next(t for t in ATT_TQS if lq % t == 0)
    tk = next(t for t in ATT_TKS if lt % t == 0)
    kern = functools.partial(_attn_kernel, tk=tk, nk=lt // tk)
    return pl.pallas_call(
        kern,
        grid=(bsz, MLA_HEADS, lq // tq),
        in_specs=[pl.BlockSpec((1, tq, HEAD_PAD), lambda b, h, i: (b, i, h)),
                  pl.BlockSpec((1, lt, HEAD_PAD), lambda b, h, i: (b, 0, h)),
                  pl.BlockSpec((1, lt, HEAD_PAD), lambda b, h, i: (b, 0, h))],
        out_specs=pl.BlockSpec((1, tq, HEAD_PAD), lambda b, h, i: (b, i, h)),
        out_shape=jax.ShapeDtypeStruct((bsz, lq, hp), BF16),
        compiler_params=_cparams(("parallel", "parallel", "parallel")),
        name="attention",
    )(q, k, v)


def _rope_tables(n_lat, n_ctx):
    rows = n_lat // GRID_W
    row = jnp.repeat(jnp.arange(rows, dtype=F32), GRID_W)
    col = jnp.tile(jnp.arange(GRID_W, dtype=F32), rows)
    inv_freq = ROPE_BASE ** (-jnp.arange(0, ROPE_AXIS_DIMS, 2, dtype=F32) / ROPE_AXIS_DIMS)
    ang = jnp.concatenate([row[:, None] * inv_freq, col[:, None] * inv_freq], -1)
    cos = jnp.concatenate([jnp.cos(ang), jnp.ones((n_ctx, ROPE_AXIS_DIMS), F32)], 0)
    sin = jnp.concatenate([jnp.sin(ang), jnp.zeros((n_ctx, ROPE_AXIS_DIMS), F32)], 0)
    lt = n_lat + n_ctx
    one = jnp.ones((lt, QK_NOPE), F32)
    z16 = jnp.zeros((lt, ROPE_AXIS_DIMS), F32)
    z32 = jnp.zeros((lt, HEAD_PAD - QK_NOPE - QK_ROPE), F32)
    z64 = jnp.zeros((lt, QK_NOPE), F32)
    ct = jnp.concatenate([one, cos, cos, z32], 1)
    s1 = jnp.concatenate([z64, -sin, z16, z32], 1)
    s2 = jnp.concatenate([z64, z16, sin, z32], 1)
    return ct, s1, s2


def _pad_heads(w, width, offset=0):
    k = w.shape[0]
    w = w.reshape(k, MLA_HEADS, width)
    w = jnp.pad(w, ((0, 0), (0, 0), (offset, HEAD_PAD - width - offset)))
    return w.reshape(k, MLA_HEADS * HEAD_PAD)


def mla_layer(h, n_ctx, g, mod, w_in, q_norm, q_up, kv_norm, kv_up, w_out):
    bsz, lt, d = h.shape
    n_lat = lt - n_ctx
    ct, s1, s2 = _rope_tables(n_lat, n_ctx)
    qup = _pad_heads(q_up, QK_NOPE + QK_ROPE).astype(BF16)
    kvu = kv_up.reshape(KV_LORA, MLA_HEADS, QK_NOPE + V_DIM)
    kup = _pad_heads(kvu[:, :, :QK_NOPE].reshape(KV_LORA, -1), QK_NOPE).astype(BF16)
    vup = _pad_heads(kvu[:, :, QK_NOPE:].reshape(KV_LORA, -1), V_DIM).astype(BF16)
    epl = _pad_heads(jnp.tile(jnp.eye(QK_ROPE, dtype=F32), (1, MLA_HEADS)), QK_ROPE, QK_NOPE).astype(BF16)
    one = _pad_heads(jnp.ones((1, MLA_HEADS), F32), 1, V_DIM)
    q, k, v = mla_proj(h, n_lat, g, mod, w_in.astype(BF16), q_norm[None], qup, kv_norm[None], kup, vup, epl, one,
                       ct, s1, s2)
    o = attention(q, k, v, n_lat)
    wo = w_out.reshape(MLA_HEADS, V_DIM, d)
    wo = jnp.pad(wo, ((0, 0), (0, HEAD_PAD - V_DIM), (0, 0))).reshape(MLA_HEADS * HEAD_PAD, d).astype(BF16)
    return outproj(o, wo, h, n_lat, mod)


def _inproj2_kernel(h_ref, g_ref, mod_ref, w_ref, p_ref, u_ref):
    m = mod_ref[0, 0]
    a = _norm_mod(h_ref[0], g_ref[...], m[0:1], m[1:2])
    o = jnp.dot(a.astype(BF16), w_ref[...], preferred_element_type=F32)
    p_ref[0] = o[:, :RWKV_IN]
    u_ref[...] = o[:, RWKV_IN:]


def inproj2(h, n_lat, g, mod, w):
    bsz, lt, d = h.shape
    return pl.pallas_call(
        _inproj2_kernel,
        grid=(bsz, lt // ROW_BLK),
        in_specs=[pl.BlockSpec((1, ROW_BLK, d), lambda b, i: (b, i, 0)),
                  pl.BlockSpec((1, d), lambda b, i: (0, 0)),
                  _mod_spec(n_lat),
                  pl.BlockSpec((d, HYB_IN), lambda b, i: (0, 0))],
        out_specs=[pl.BlockSpec((1, ROW_BLK, RWKV_IN), lambda b, i: (b, i, 0)),
                   pl.BlockSpec((ROW_BLK, S5_WIDTH), lambda b, i: (i, b))],
        out_shape=[jax.ShapeDtypeStruct((bsz, lt, RWKV_IN), F32),
                   jax.ShapeDtypeStruct((lt, bsz * S5_WIDTH), F32)],
        compiler_params=_cparams(("parallel", "parallel")),
        name="inproj2",
    )(h, g, mod, w)


def _tri_inverse_all(mats, idx_r, idx_c):
    c = mats[0].shape[0]
    eye = (idx_r == idx_c).astype(F32)
    blk8 = (idx_r >> 3) == (idx_c >> 3)
    ns = [jnp.where(blk8, -a, 0.0) for a in mats]
    ts = [eye + n for n in ns]
    n2 = [_dot(n, n) for n in ns]
    ts = [t + _dot(m, t) for m, t in zip(n2, ts)]
    n4 = [_dot(m, m) for m in n2]
    ts = [t + _dot(m, t) for m, t in zip(n4, ts)]
    sh = 3
    while (1 << sh) < c:
        off = jnp.logical_and((idx_r >> (sh + 1)) == (idx_c >> (sh + 1)), (idx_r >> sh) != (idx_c >> sh))
        ms = [_dot(jnp.where(off, a, 0.0), t) for a, t in zip(mats, ts)]
        ts = [t - _dot(t, m) for m, t in zip(ms, ts)]
        sh += 1
    return ts


def _rwkv_conv(pc, prev_row, next_row, cw):
    c = pc.shape[0]
    row = lax.broadcasted_iota(jnp.int32, (c, 1), 0)
    xm1 = jnp.where(row == 0, prev_row, pltpu.roll(pc, 1, 0))
    xp1 = jnp.where(row == c - 1, next_row, pltpu.roll(pc, c - 1, 0))
    return cw[0:1] * xm1 + cw[1:2] * pc + cw[2:3] * xp1


def _rwkv_chunk_inputs(x, z, bi, w0, wup, a0, aup, kkv, kav):
    c = x.shape[0]
    n = RWKV_HEAD_DIM
    r = x[:, 0:RWKV_WIDTH]
    k = x[:, RWKV_WIDTH:2 * RWKV_WIDTH]
    v = x[:, 2 * RWKV_WIDTH:3 * RWKV_WIDTH]
    wd = x[:, 3 * RWKV_WIDTH:3 * RWKV_WIDTH + W_LORA]
    ad = x[:, 3 * RWKV_WIDTH + W_LORA:3 * RWKV_WIDTH + W_LORA + A_LORA]
    lw = -DECAY_SCALE * _sigmoid(w0 + _dot(jnp.tanh(wd), wup, True))
    a = _sigmoid(a0 + _dot(ad, aup, True))
    idx_r = lax.broadcasted_iota(jnp.int32, (c, c), 0)
    idx_c = lax.broadcasted_iota(jnp.int32, (c, c), 1)
    if z == 0:
        incl = idx_c <= idx_r
        strict = idx_c < idx_r
        last = c - 1
    else:
        incl = idx_c >= idx_r
        strict = idx_c > idx_r
        last = 0
    cum = _dot(incl.astype(F32), lw, True)
    tot = cum[last:last + 1]
    e1 = jnp.exp(cum)
    e2 = jnp.exp(cum - lw)
    e3 = jnp.exp(-cum)
    e4 = jnp.exp(tot - cum)
    etot = jnp.exp(tot)
    kkf = k * kkv
    heads = []
    for h in range(RWKV_HEADS):
        sl = slice(h * n, (h + 1) * n)
        kk = kkf[:, sl]
        kk = kk * lax.rsqrt(jnp.sum(kk * kk, -1, keepdims=True) + 1e-12)
        ah = a[:, sl]
        bh = ah * kk
        kd = k[:, sl] * (1.0 + (ah - 1.0) * kav[:, sl])
        heads.append(dict(
            xq=jnp.concatenate([kk * e2[:, sl], r[:, sl] * e1[:, sl]], 0),
            yk=jnp.concatenate([kd * e3[:, sl], bh * e3[:, sl]], 0),
            ke=jnp.concatenate([kd * e4[:, sl], bh * e4[:, sl]], 0),
            v=v[:, sl], etot=etot[:, sl], incl=incl, strict=strict, state=(z, bi, h)))
    return heads


def _rwkv_solve_all(heads, s_sc):
    c = heads[0]['v'].shape[0]
    idx_r = lax.broadcasted_iota(jnp.int32, (c, c), 0)
    idx_c = lax.broadcasted_iota(jnp.int32, (c, c), 1)
    gs = [_dot_nt(hd['xq'], hd['yk']) for hd in heads]
    ss = [s_sc[hd['state']] for hd in heads]
    xs = [_dot_nt(hd['xq'], s) for hd, s in zip(heads, ss)]
    avs = [_dot(jnp.where(hd['strict'], g[:c, :c], 0.0), hd['v']) for hd, g in zip(heads, gs)]
    ts = _tri_inverse_all([jnp.where(hd['strict'], g[:c, c:], 0.0) for hd, g in zip(heads, gs)], idx_r, idx_c)
    sas = [_dot(t, x[:c] + av) for t, x, av in zip(ts, xs, avs)]
    outs = []
    for hd, g, x, sa, s in zip(heads, gs, xs, sas, ss):
        a_r = jnp.concatenate([jnp.where(hd['incl'], g[c:, :c], 0.0), jnp.where(hd['incl'], -g[c:, c:], 0.0)], 1)
        outs.append(x[c:] + _dot(a_r, jnp.concatenate([hd['v'], sa], 0)))
        s_sc[hd['state']] = s * hd['etot'] + _dot_tn(jnp.concatenate([hd['v'], -sa], 0), hd['ke'])
    return outs


def _rwkv_kernel(pa_ref, pap_ref, pan_ref, pb_ref, pbp_ref, pbn_ref, cw_ref, w0_ref, wup_ref, a0_ref, aup_ref,
                 gup_ref, kk_ref, ka_ref, rk_ref, y0_ref, y1_ref, bonus_ref, g_ref, s_sc, *, nl, nc):
    s = pl.program_id(1)

    @pl.when(s == 0)
    def _():
        s_sc[...] = jnp.zeros_like(s_sc)

    ca = (s + nl) % nc
    cb = nc - 1 - s
    cw = cw_ref[...]

    def load(p_ref, pp_ref, pn_ref, cidx, bi):
        first = jnp.logical_or(cidx == 0, cidx == nl)
        lastc = jnp.logical_or(cidx == nl - 1, cidx == nc - 1)
        prev_row = jnp.where(first, 0.0, pp_ref[bi, 7:8, :])
        next_row = jnp.where(lastc, 0.0, pn_ref[bi, 0:1, :])
        return _rwkv_conv(p_ref[bi], prev_row, next_row, cw)

    nb = pa_ref.shape[0]
    heads = []
    xas = []
    for bi in range(nb):
        xa = load(pa_ref, pap_ref, pan_ref, ca, bi)
        xb = load(pb_ref, pbp_ref, pbn_ref, cb, bi)
        xas.append(xa)
        heads += _rwkv_chunk_inputs(xa, 0, bi, w0_ref[0:1], wup_ref[0], a0_ref[0:1], aup_ref[0], kk_ref[...],
                                    ka_ref[...])
        heads += _rwkv_chunk_inputs(xb, 1, bi, w0_ref[1:2], wup_ref[1], a0_ref[1:2], aup_ref[1], kk_ref[...],
                                    ka_ref[...])
    outs = _rwkv_solve_all(heads, s_sc)
    nh = RWKV_HEADS
    for bi in range(nb):
        y0_ref[bi] = jnp.concatenate(outs[2 * nh * bi:2 * nh * bi + nh], 1)
        y1_ref[bi] = jnp.concatenate(outs[2 * nh * bi + nh:2 * nh * (bi + 1)], 1)
        xa = xas[bi]
        gd = xa[:, 3 * RWKV_WIDTH + W_LORA + A_LORA:]
        g_ref[bi] = _dot(_sigmoid(gd), gup_ref[...], True)
        r = xa[:, 0:RWKV_WIDTH]
        k = xa[:, RWKV_WIDTH:2 * RWKV_WIDTH]
        v = xa[:, 2 * RWKV_WIDTH:3 * RWKV_WIDTH]
        rkr = r * k * rk_ref[...]
        bon = []
        for h in range(nh):
            sl = slice(h * RWKV_HEAD_DIM, (h + 1) * RWKV_HEAD_DIM)
            bon.append(jnp.sum(rkr[:, sl], -1, keepdims=True) * v[:, sl])
        bonus_ref[bi] = jnp.concatenate(bon, 1)


def rwkv_scan(p, n_lat, cw, w0, wup, a0, aup, gup, kkv, kav, rk):
    bsz, lt, _ = p.shape
    nl = n_lat // CHUNK
    nc = lt // CHUNK
    nb8 = lt // 8
    ca = lambda s: (s + nl) % nc
    cb = lambda s: nc - 1 - s
    nb = RWKV_NB if bsz % RWKV_NB == 0 else 1
    pspec = lambda cf: pl.BlockSpec((nb, CHUNK, RWKV_IN), lambda b, s: (b, cf(s), 0))
    prev = lambda cf: pl.BlockSpec((nb, 8, RWKV_IN), lambda b, s: (b, jnp.maximum(cf(s) * (CHUNK // 8) - 1, 0), 0))
    nxt = lambda cf: pl.BlockSpec((nb, 8, RWKV_IN),
                                  lambda b, s: (b, jnp.minimum((cf(s) + 1) * (CHUNK // 8), nb8 - 1), 0))
    full = lambda a: pl.BlockSpec(a.shape, lambda b, s: (0,) * a.ndim)
    ospec = lambda cf: pl.BlockSpec((nb, CHUNK, RWKV_WIDTH), lambda b, s: (b, cf(s), 0))
    oshape = jax.ShapeDtypeStruct((bsz, lt, RWKV_WIDTH), F32)
    kern = functools.partial(_rwkv_kernel, nl=nl, nc=nc)
    params = (cw, w0, wup, a0, aup, gup, kkv, kav, rk)
    return pl.pallas_call(
        kern,
        grid=(bsz // nb, nc),
        in_specs=[pspec(ca), prev(ca), nxt(ca), pspec(cb), prev(cb), nxt(cb)] + [full(a) for a in params],
        out_specs=[ospec(ca), ospec(cb), ospec(ca), ospec(ca)],
        out_shape=[oshape] * 4,
        scratch_shapes=[pltpu.VMEM((2, nb, RWKV_HEADS, RWKV_HEAD_DIM, RWKV_HEAD_DIM), F32)],
        compiler_params=_cparams(("parallel", "arbitrary")),
        name="rwkv_scan",
    )(p, p, p, p, p, p, *params)


def _s5_kernel(uf_ref, ur_ref, lam_ref, bm_ref, cm_ref, yf_ref, yr_ref, bf_sc, br_sc, h_sc):
    s = pl.program_id(0)

    @pl.when(s == 0)
    def _():
        h_sc[...] = jnp.zeros_like(h_sc)

    rows = bf_sc.shape[0]
    win = S5_WIDTH // 2
    for z, (u_ref, sc) in enumerate(((uf_ref, bf_sc), (ur_ref, br_sc))):
        ub = u_ref[...].astype(BF16)
        for k in range(2):
            sc[:, 2 * S5_HALF * k:2 * S5_HALF * (k + 1)] = jnp.dot(
                ub[:, win * k:win * (k + 1)], bm_ref[z, k], preferred_element_type=F32)
    ntile = rows // 8
    lo = lax.broadcasted_iota(jnp.int32, (8, 1), 0) < 4

    def body(jj, hs):
        sf = pl.multiple_of(jj * 8, 8)
        sr = pl.multiple_of((ntile - 1 - jj) * 8, 8)
        new = []
        for k in range(2):
            cre = pl.ds(2 * S5_HALF * k, S5_HALF)
            cim = pl.ds(2 * S5_HALF * k + S5_HALF, S5_HALF)
            lr = lam_ref[0, :, S5_HALF * k:S5_HALF * (k + 1)]
            li = lam_ref[1, :, S5_HALF * k:S5_HALF * (k + 1)]
            hr, hi = hs[2 * k], hs[2 * k + 1]
            tfr, tfi = bf_sc[pl.ds(sf, 8), cre], bf_sc[pl.ds(sf, 8), cim]
            trr, tri = br_sc[pl.ds(sr, 8), cre], br_sc[pl.ds(sr, 8), cim]
            xar, xai = jnp.where(lo, tfr, trr), jnp.where(lo, tfi, tri)
            xbr = pltpu.roll(jnp.where(lo, trr, tfr), 4, 0)
            xbi = pltpu.roll(jnp.where(lo, tri, tfi), 4, 0)
            ar, ai = lr * hr - li * hi + xar, lr * hi + li * hr + xai
            br, bi = lr * ar - li * ai + xbr, lr * ai + li * ar + xbi
            rbr, rbi = pltpu.roll(br, 4, 0), pltpu.roll(bi, 4, 0)
            bf_sc[pl.ds(sf, 8), cre] = jnp.where(lo, ar, rbr)
            bf_sc[pl.ds(sf, 8), cim] = jnp.where(lo, ai, rbi)
            br_sc[pl.ds(sr, 8), cre] = jnp.where(lo, rbr, ar)
            br_sc[pl.ds(sr, 8), cim] = jnp.where(lo, rbi, ai)
            new += [br, bi]
        return tuple(new)

    hs = lax.fori_loop(0, ntile, body, tuple(h_sc[i] for i in range(4)))
    for i in range(4):
        h_sc[i] = hs[i]
    for z, (sc, y_ref) in enumerate(((bf_sc, yf_ref), (br_sc, yr_ref))):
        for k in range(2):
            y_ref[:, win * k:win * (k + 1)] = jnp.dot(
                sc[:, 2 * S5_HALF * k:2 * S5_HALF * (k + 1)].astype(BF16), cm_ref[z, k],
                preferred_element_type=F32)


def s5_scan(u, n_lat, bsz, lam, bmat, cmat):
    assert bsz == 4
    rows_total = u.shape[0]
    lt = rows_total // bsz
    nl = n_lat // CHUNK
    nc = lt // CHUNK
    rb = CHUNK * bsz
    cf = lambda s: ((s + nl) % nc, 0)
    cr = lambda s: (nc - 1 - s, 0)
    full = lambda a: pl.BlockSpec(a.shape, lambda s: (0,) * a.ndim)
    oshape = jax.ShapeDtypeStruct((rows_total, S5_WIDTH), F32)
    return pl.pallas_call(
        _s5_kernel,
        grid=(nc,),
        in_specs=[pl.BlockSpec((rb, S5_WIDTH), cf), pl.BlockSpec((rb, S5_WIDTH), cr),
                  full(lam), full(bmat), full(cmat)],
        out_specs=[pl.BlockSpec((rb, S5_WIDTH), cf), pl.BlockSpec((rb, S5_WIDTH), cr)],
        out_shape=[oshape, oshape],
        scratch_shapes=[pltpu.VMEM((rb, 2 * S5_N), F32), pltpu.VMEM((rb, 2 * S5_N), F32),
                        pltpu.VMEM((4, 8, S5_HALF), F32)],
        compiler_params=_cparams(("arbitrary",)),
        name="s5_scan",
    )(u, u, lam, bmat, cmat)


def _s5_params(lam_re, lam_im, log_dt, b_re, b_im, c_re, c_im):
    dt = jnp.exp(log_dt.astype(F32))[..., None]
    mag = jnp.exp(lam_re * dt)
    lbr = mag * jnp.cos(lam_im * dt)
    lbi = mag * jnp.sin(lam_im * dt)
    den = lam_re * lam_re + lam_im * lam_im
    nr, ni = lbr - 1.0, lbi
    fr = (nr * lam_re + ni * lam_im) / den
    fi = (ni * lam_re - nr * lam_im) / den
    bbr = fr[..., None] * b_re - fi[..., None] * b_im
    bbi = fr[..., None] * b_im + fi[..., None] * b_re
    gh = S5_GROUPS // 2
    eye = jnp.eye(gh, dtype=F32)

    def bd_in(w):
        return jnp.einsum('gpc,gh->gchp', w, eye).reshape(S5_WIDTH // 2, S5_HALF)

    def bd_out(w):
        return jnp.einsum('gcp,gh->gphc', w, eye).reshape(S5_HALF, S5_WIDTH // 2)

    halves = [slice(0, gh), slice(gh, S5_GROUPS)]
    bmat = jnp.stack([jnp.stack([jnp.concatenate([bd_in(bbr[z, hs]), bd_in(bbi[z, hs])], 1) for hs in halves])
                      for z in range(2)])
    cmat = jnp.stack([jnp.stack([jnp.concatenate([bd_out(c_re[z, hs]), -bd_out(c_im[z, hs])], 0) for hs in halves])
                      for z in range(2)])
    rows = lambda t: jnp.concatenate([jnp.tile(t[0].reshape(1, S5_N), (4, 1)),
                                      jnp.tile(t[1].reshape(1, S5_N), (4, 1))], 0)
    lam = jnp.stack([rows(lbr), rows(lbi)])
    return lam, bmat.astype(BF16), cmat.astype(BF16)


def _gelu(x):
    return 0.5 * x * (1.0 + jnp.tanh(math.sqrt(2.0 / math.pi) * (x + 0.044715 * x * x * x)))


def _merge_kernel(y0_ref, y1_ref, bonus_ref, g_ref, s0_ref, s1_ref, u_ref, h_ref, mod_ref, lnw_ref, lnb_ref,
                  d_ref, gluw_ref, glub_ref, wout_ref, o_ref):
    y = y0_ref[0] + y1_ref[0]
    n = RWKV_HEAD_DIM
    parts = []
    for h in range(RWKV_HEADS):
        yh = y[:, h * n:(h + 1) * n]
        mu = jnp.mean(yh, -1, keepdims=True)
        dlt = yh - mu
        var = jnp.mean(dlt * dlt, -1, keepdims=True)
        parts.append(dlt * lax.rsqrt(var + GN_EPS))
    yn = jnp.concatenate(parts, 1) * lnw_ref[...] + lnb_ref[...]
    rw = (yn + bonus_ref[0]) * g_ref[0]
    ys = s0_ref[...] + s1_ref[...] + d_ref[...] * u_ref[...]
    zz = _gelu(ys)
    gate = _sigmoid(_dot(zz, gluw_ref[...]) + glub_ref[...])
    cat = jnp.concatenate([rw, zz * gate], 1).astype(BF16)
    m = mod_ref[0, 0]
    o_ref[0] = h_ref[0] + m[2:3] * jnp.dot(cat, wout_ref[...], preferred_element_type=F32)


def merge(y0, y1, bonus, g, s0, s1, u, h, n_lat, mod, lnw, lnb, dvec, gluw, glub, wout):
    bsz, lt, d = h.shape
    rspec = pl.BlockSpec((1, ROW_BLK, RWKV_WIDTH), lambda b, i: (b, i, 0))
    sspec = pl.BlockSpec((ROW_BLK, S5_WIDTH), lambda b, i: (i, b))
    full = lambda a: pl.BlockSpec(a.shape, lambda b, i: (0,) * a.ndim)
    params = (lnw, lnb, dvec, gluw, glub, wout)
    return pl.pallas_call(
        _merge_kernel,
        grid=(bsz, lt // ROW_BLK),
        in_specs=[rspec, rspec, rspec, rspec, sspec, sspec, sspec,
                  pl.BlockSpec((1, ROW_BLK, d), lambda b, i: (b, i, 0)), _mod_spec(n_lat)]
                 + [full(a) for a in params],
        out_specs=pl.BlockSpec((1, ROW_BLK, d), lambda b, i: (b, i, 0)),
        out_shape=jax.ShapeDtypeStruct((bsz, lt, d), F32),
        compiler_params=_cparams(("parallel", "parallel")),
        name="merge",
    )(y0, y1, bonus, g, s0, s1, u, h, mod, *params)


def hybrid_layer(h, n_lat, g1, mod, w_in, w_out, conv_w, w0, w_up, a0, a_up, g_up, k_k, k_a, r_k, ln_w, ln_b,
                 lam_re, lam_im, log_dt, b_re, b_im, c_re, c_im, dvec, glu_w, glu_b):
    bsz = h.shape[0]
    p, u = inproj2(h, n_lat, g1, mod, w_in.astype(BF16))
    y0, y1, bonus, g = rwkv_scan(p, n_lat, conv_w, w0, w_up, a0, a_up, g_up, k_k[None], k_a[None],
                                 r_k.reshape(1, RWKV_WIDTH))
    lam, bmat, cmat = _s5_params(lam_re, lam_im, log_dt, b_re, b_im, c_re, c_im)
    s0, s1 = s5_scan(u.reshape(-1, S5_WIDTH), n_lat, bsz, lam, bmat, cmat)
    s0, s1 = s0.reshape(u.shape), s1.reshape(u.shape)
    gluw = jnp.einsum('gce,gh->gche', glu_w, jnp.eye(S5_GROUPS, dtype=F32)).reshape(S5_WIDTH, S5_WIDTH)
    return merge(y0, y1, bonus, g, s0, s1, u, h, n_lat, mod, ln_w[None], ln_b[None], dvec[None],
                 gluw.astype(BF16), glu_b[None], w_out.astype(BF16))


def _stack_experts(ex, sh):
    return jnp.concatenate([ex, sh[None]], 0).astype(BF16)


def kernel(x, c, ctx, c_ctx, mod_w, mod_b, norm1_g, norm2_g, final_g, hy_w_in, hy_w_out, rk_conv, rk_w0, rk_w_up, rk_a0, rk_a_up, rk_g_up, rk_k_k, rk_k_a, rk_r_k, rk_ln_w, rk_ln_b, s5_lam_re, s5_lam_im, s5_log_dt, s5_b_re, s5_b_im, s5_c_re, s5_c_im, s5_d, s5_glu_w, s5_glu_b, mla_w_in, mla_q_norm, mla_q_up, mla_kv_norm, mla_kv_up, mla_w_out, router_w, router_b, ex_gate, ex_up, ex_down, sh_gate, sh_up, sh_down):
    bsz, n_lat, d = x.shape
    n_ctx = ctx.shape[1]
    depth = mod_w.shape[0]
    assert n_ctx == ROW_BLK and depth == 2 and bsz <= 7
    h = jnp.concatenate([x, ctx], 1)
    cc = jnp.concatenate([c, c_ctx[None], jnp.zeros((8 - bsz - 1, d), F32)], 0)
    rb = router_b[:, None].astype(F32)
    rw_t = router_w.T
    fg = final_g[None]
    for layer in range(depth):
        last = layer == depth - 1
        i = layer // 2
        mv = adaln(cc, mod_w[layer], mod_b[layer][None])
        m_l = mv[:bsz].reshape(bsz, 1, 6, d)
        m_c = jnp.broadcast_to(mv[bsz].reshape(1, 1, 6, d), (bsz, 1, 6, d))
        mod = jnp.concatenate([m_l, m_c], 1)
        g1 = norm1_g[layer][None]
        g2 = norm2_g[layer][None]
        if layer % 2 == 0:
            h = hybrid_layer(h, n_lat, g1, mod, hy_w_in[i], hy_w_out[i], rk_conv[i], rk_w0[i], rk_w_up[i], rk_a0[i],
                             rk_a_up[i], rk_g_up[i], rk_k_k[i], rk_k_a[i], rk_r_k[i], rk_ln_w[i], rk_ln_b[i],
                             s5_lam_re[i], s5_lam_im[i], s5_log_dt[i], s5_b_re[i], s5_b_im[i], s5_c_re[i],
                             s5_c_im[i], s5_d[i], s5_glu_w[i], s5_glu_b[i])
        else:
            h = mla_layer(h, n_ctx, g1, mod, mla_w_in[i], mla_q_norm[i], mla_q_up[i], mla_kv_norm[i],
                          mla_kv_up[i], mla_w_out[i])
        wg = _stack_experts(ex_gate[layer], sh_gate[layer])
        wu = _stack_experts(ex_up[layer], sh_up[layer])
        wd = _stack_experts(ex_down[layer], sh_down[layer])
        h = moe(h, n_lat, g2, mod, rw_t, rb, wg, wu, wd, fg, last)
    return h
```

```python
import functools
import math

import jax
import jax.numpy as jnp
from jax import lax
from jax.experimental import pallas as pl
from jax.experimental.pallas import tpu as pltpu

F32 = jnp.float32
BF16 = jnp.bfloat16
HI = lax.Precision.HIGHEST

D_MODEL = 1024
GRID_W = 64
NORM_EPS = 1e-6

RWKV_HEADS = 8
RWKV_HEAD_DIM = 64
RWKV_WIDTH = 512
W_LORA = 64
A_LORA = 64
G_LORA = 128
RWKV_IN = 3 * RWKV_WIDTH + W_LORA + A_LORA + G_LORA
DECAY_SCALE = math.exp(-0.5)
GN_EPS = 64e-5

S5_GROUP = 16
S5_GROUPS = 32
S5_WIDTH = 512
S5_STATE = 64
S5_N = S5_GROUPS * S5_STATE
S5_HALF = S5_N // 2
LANES = 128
S5_TILES = S5_WIDTH // LANES
HYB_IN = RWKV_IN + S5_WIDTH

MLA_HEADS = 16
Q_LORA = 256
KV_LORA = 128
QK_NOPE = 64
QK_ROPE = 32
V_DIM = 64
MLA_IN = Q_LORA + KV_LORA + QK_ROPE
MLA_SCALE = (QK_NOPE + QK_ROPE) ** -0.5
ROPE_AXIS_DIMS = QK_ROPE // 2
ROPE_BASE = 10000.0
HEAD_PAD = 128

N_EXPERTS = 16
N_GROUPS = 4
EXPERTS_PER_GROUP = 4
D_EXPERT = 256

ROW_BLK = 256
SEQ_BLK = 128
MOE_BLKS = (1024, 768, 512, 256)
ATT_TQS = (1024, 512, 256)
ATT_TKS = (1408, 768, 256)
ATT_SUB = 256
CHUNK = 64
RWKV_NB = 2
VMEM_LIMIT = 56 * 1024 * 1024


def _cparams(sem):
    return pltpu.CompilerParams(dimension_semantics=sem, vmem_limit_bytes=VMEM_LIMIT)


def _norm_mod(x, g, shift, scale):
    y = x * lax.rsqrt(jnp.mean(x * x, -1, keepdims=True) + NORM_EPS) * g
    return y * (1.0 + scale) + shift


def _sigmoid(x):
    return 1.0 / (1.0 + jnp.exp(-x))


def _dot(a, b, hi=False):
    if hi:
        return jnp.dot(a, b, precision=HI, preferred_element_type=F32)
    return jnp.dot(a.astype(BF16), b.astype(BF16), preferred_element_type=F32)


def _dot_nt(a, b):
    return lax.dot_general(a.astype(BF16), b.astype(BF16), (((1,), (1,)), ((), ())), preferred_element_type=F32)


def _dot_tn(a, b):
    return jnp.dot(a.T.astype(BF16), b.astype(BF16), preferred_element_type=F32)


def _adaln_kernel(s_ref, w_ref, b_ref, o_ref):
    s = s_ref[...]
    s = s * _sigmoid(s)
    o_ref[...] = jnp.dot(s, w_ref[...], precision=HI, preferred_element_type=F32) + b_ref[...]


def adaln(cc, w, b):
    n = w.shape[1]
    tn = 512
    return pl.pallas_call(
        _adaln_kernel,
        grid=(n // tn,),
        in_specs=[pl.BlockSpec((8, D_MODEL), lambda j: (0, 0)),
                  pl.BlockSpec((D_MODEL, tn), lambda j: (0, j)),
                  pl.BlockSpec((1, tn), lambda j: (0, j))],
        out_specs=pl.BlockSpec((8, tn), lambda j: (0, j)),
        out_shape=jax.ShapeDtypeStruct((8, n), F32),
        compiler_params=_cparams(("parallel",)),
        name="adaln",
    )(cc, w, b)


def _mod_spec(n_lat, blk=ROW_BLK):
    return pl.BlockSpec((1, 1, 6, D_MODEL), lambda b, i, *_: (b, i // (n_lat // blk), 0, 0))


def _outproj_kernel(a_ref, w_ref, h_ref, mod_ref, o_ref):
    m = mod_ref[0, 0]
    o = jnp.dot(a_ref[0].astype(BF16), w_ref[...], preferred_element_type=F32)
    o_ref[0] = h_ref[0] + m[2:3] * o


def outproj(a, w, h, n_lat, mod):
    bsz, la, k = a.shape
    d = h.shape[2]
    return pl.pallas_call(
        _outproj_kernel,
        grid=(bsz, la // ROW_BLK),
        in_specs=[pl.BlockSpec((1, ROW_BLK, k), lambda b, i: (b, i, 0)),
                  pl.BlockSpec((k, d), lambda b, i: (0, 0)),
                  pl.BlockSpec((1, ROW_BLK, d), lambda b, i: (b, i, 0)),
                  _mod_spec(n_lat)],
        out_specs=pl.BlockSpec((1, ROW_BLK, d), lambda b, i: (b, i, 0)),
        out_shape=jax.ShapeDtypeStruct((bsz, la, d), F32),
        compiler_params=_cparams(("parallel", "parallel")),
        name="outproj",
    )(a, w, h, mod)


def _route(scores, rb):
    t = scores.shape[1]
    biased = scores + rb
    col = [biased[e:e + 1, :] for e in range(N_EXPERTS)]
    sc = [scores[e:e + 1, :] for e in range(N_EXPERTS)]
    gscore = []
    for gi in range(N_GROUPS):
        a, b, c, d = col[4 * gi:4 * gi + 4]
        hi1, lo1 = jnp.maximum(a, b), jnp.minimum(a, b)
        hi2, lo2 = jnp.maximum(c, d), jnp.minimum(c, d)
        gscore.append(jnp.maximum(hi1, hi2) + jnp.maximum(jnp.minimum(hi1, hi2), jnp.maximum(lo1, lo2)))
    gsel = []
    taken = None
    for gi in range(N_GROUPS):
        best = None
        for gj in range(gi + 1, N_GROUPS):
            best = gscore[gj] if best is None else jnp.maximum(best, gscore[gj])
        s = (gscore[gi] >= best) if best is not None else jnp.full((1, t), True)
        if taken is not None:
            s = jnp.logical_and(s, jnp.logical_not(taken))
        taken = s if taken is None else jnp.logical_or(taken, s)
        gsel.append(s)
    masks = []
    for gi in range(N_GROUPS):
        v = col[4 * gi:4 * gi + 4]
        for j in range(4):
            rank = jnp.zeros((1, t), F32)
            for i in range(4):
                if i == j:
                    continue
                ahead = (v[i] >= v[j]) if i < j else (v[i] > v[j])
                rank = rank + ahead.astype(F32)
            masks.append(jnp.logical_and(gsel[gi], rank < 2.0))
    wsel = [jnp.where(masks[e], sc[e], 0.0) for e in range(N_EXPERTS)]
    denom = wsel[0]
    for e in range(1, N_EXPERTS):
        denom = denom + wsel[e]
    return [w / denom for w in wsel]


def _moe_kernel(h_ref, g_ref, mod_ref, rw_ref, rb_ref, wg_ref, wu_ref, wd_ref, fg_ref, o_ref,
                f_sc, comb_sc, acc_sc, *, final_norm, n_lat, has_ctx):
    i = pl.program_id(1)
    e = pl.program_id(2)
    n_e = pl.num_programs(2)
    tm = f_sc.shape[0]

    def mod_row(k):
        lat = mod_ref[0, 0, k:k + 1, :]
        if not has_ctx:
            return lat
        row = i * tm + lax.broadcasted_iota(jnp.int32, (tm, 1), 0)
        return jnp.where(row >= n_lat, mod_ref[0, 1, k:k + 1, :], lat)

    @pl.when(e == 0)
    def _():
        f = _norm_mod(h_ref[0], g_ref[...], mod_row(3), mod_row(4))
        f_sc[...] = f.astype(BF16)
        logits = lax.dot_general(rw_ref[...], f, (((1,), (1,)), ((), ())), precision=HI,
                                 preferred_element_type=F32)
        cw = _route(_sigmoid(logits), rb_ref[...])
        sub = lax.broadcasted_iota(jnp.int32, (128, tm), 0)
        comb = jnp.where(sub == N_EXPERTS, 1.0, 0.0)
        for ei in range(N_EXPERTS):
            comb = jnp.where(sub == ei, cw[ei], comb)
        comb_sc[...] = comb.T
        acc_sc[...] = jnp.zeros_like(acc_sc)

    fb = f_sc[...]
    gt = jnp.dot(fb, wg_ref[0], preferred_element_type=F32)
    up = jnp.dot(fb, wu_ref[0], preferred_element_type=F32)
    lane = lax.broadcasted_iota(jnp.int32, (tm, 128), 1)
    cw = jnp.sum(jnp.where(lane == e, comb_sc[...], 0.0), axis=1, keepdims=True)
    act = gt * _sigmoid(gt) * up * cw
    acc_sc[...] += jnp.dot(act.astype(BF16), wd_ref[0], preferred_element_type=F32)

    @pl.when(e == n_e - 1)
    def _():
        y = h_ref[0] + mod_row(5) * acc_sc[...]
        if final_norm:
            y = y * lax.rsqrt(jnp.mean(y * y, -1, keepdims=True) + NORM_EPS) * fg_ref[...]
        o_ref[0] = y


def moe(h, n_lat, g, mod, rw, rb, wg, wu, wd, fg, final_norm):
    bsz, lo, d = h.shape
    ne = wg.shape[0]
    tm = next(t for t in MOE_BLKS if lo % t == 0)
    kern = functools.partial(_moe_kernel, final_norm=final_norm, n_lat=n_lat, has_ctx=lo > n_lat)
    return pl.pallas_call(
        kern,
        grid=(bsz, lo // tm, ne),
        in_specs=[pl.BlockSpec((1, tm, d), lambda b, i, e: (b, i, 0)),
                  pl.BlockSpec((1, d), lambda b, i, e: (0, 0)),
                  pl.BlockSpec((1, 2, 6, d), lambda b, i, e: (b, 0, 0, 0)),
                  pl.BlockSpec((N_EXPERTS, d), lambda b, i, e: (0, 0)),
                  pl.BlockSpec((N_EXPERTS, 1), lambda b, i, e: (0, 0)),
                  pl.BlockSpec((1, d, D_EXPERT), lambda b, i, e: (e, 0, 0)),
                  pl.BlockSpec((1, d, D_EXPERT), lambda b, i, e: (e, 0, 0)),
                  pl.BlockSpec((1, D_EXPERT, d), lambda b, i, e: (e, 0, 0)),
                  pl.BlockSpec((1, d), lambda b, i, e: (0, 0))],
        out_specs=pl.BlockSpec((1, tm, d), lambda b, i, e: (b, i, 0)),
        out_shape=jax.ShapeDtypeStruct((bsz, lo, d), F32),
        scratch_shapes=[pltpu.VMEM((tm, d), BF16),
                        pltpu.VMEM((tm, 128), F32),
                        pltpu.VMEM((tm, d), F32)],
        compiler_params=_cparams(("parallel", "parallel", "arbitrary")),
        name="moe",
    )(h, g, mod, rw, rb, wg, wu, wd, fg)


def _mla_proj_kernel(h_ref, g_ref, mod_ref, win_ref, qn_ref, qup_ref, qsw_ref, kvn_ref, kup_ref, vup_ref, epl_ref,
                     esw_ref, one_ref, ct_ref, st_ref, q_ref, k_ref, v_ref):
    m = mod_ref[0, 0]
    a = _norm_mod(h_ref[0], g_ref[...], m[0:1], m[1:2])
    p = jnp.dot(a.astype(BF16), win_ref[...], preferred_element_type=F32)
    qd = p[:, :Q_LORA]
    kvd = p[:, Q_LORA:Q_LORA + KV_LORA]
    kpe = p[:, Q_LORA + KV_LORA:]
    qn = qd * lax.rsqrt(jnp.mean(qd * qd, -1, keepdims=True) + NORM_EPS) * qn_ref[...]
    kvn = kvd * lax.rsqrt(jnp.mean(kvd * kvd, -1, keepdims=True) + NORM_EPS) * kvn_ref[...]
    kvb = kvn.astype(BF16)
    ct = jnp.concatenate([ct_ref[...]] * MLA_HEADS, axis=1)
    st = jnp.concatenate([st_ref[...]] * MLA_HEADS, axis=1)
    qb = qn.astype(BF16)
    q = jnp.dot(qb, qup_ref[...], preferred_element_type=F32)
    q_sw = jnp.dot(qb, qsw_ref[...], preferred_element_type=F32)
    q_ref[0] = ((q * ct + q_sw * st) * MLA_SCALE).astype(BF16)
    kpb = kpe.astype(BF16)
    k = (jnp.dot(kvb, kup_ref[...], preferred_element_type=F32)
         + jnp.dot(kpb, epl_ref[...], preferred_element_type=F32))
    k_sw = jnp.dot(kpb, esw_ref[...], preferred_element_type=F32)
    k_ref[0] = (k * ct + k_sw * st).astype(BF16)
    v = jnp.dot(kvb, vup_ref[...], preferred_element_type=F32) + one_ref[...]
    v_ref[0] = v.astype(BF16)


def mla_proj(h, n_lat, g, mod, win, qn, qup, qsw, kvn, kup, vup, epl, esw, one, ct, st):
    bsz, lt, d = h.shape
    hp = MLA_HEADS * HEAD_PAD
    full = lambda shape: pl.BlockSpec(shape, lambda b, i: (0,) * len(shape))
    tab = pl.BlockSpec((ROW_BLK, HEAD_PAD), lambda b, i: (i, 0))
    out = pl.BlockSpec((1, ROW_BLK, hp), lambda b, i: (b, i, 0))
    return pl.pallas_call(
        _mla_proj_kernel,
        grid=(bsz, lt // ROW_BLK),
        in_specs=[pl.BlockSpec((1, ROW_BLK, d), lambda b, i: (b, i, 0)),
                  full((1, d)), _mod_spec(n_lat), full((d, MLA_IN)), full((1, Q_LORA)), full((Q_LORA, hp)),
                  full((Q_LORA, hp)), full((1, KV_LORA)), full((KV_LORA, hp)), full((KV_LORA, hp)),
                  full((QK_ROPE, hp)), full((QK_ROPE, hp)), full((1, hp)), tab, tab],
        out_specs=[out, out, out],
        out_shape=[jax.ShapeDtypeStruct((bsz, lt, hp), BF16)] * 3,
        compiler_params=_cparams(("parallel", "parallel")),
        name="mla_proj",
    )(h, g, mod, win, qn, qup, qsw, kvn, kup, vup, epl, esw, one, ct, st)


def _attn_kernel(q_ref, k_ref, v_ref, o_ref, *, tk, nk):
    tq = q_ref.shape[1]
    nsub = tq // ATT_SUB
    qs = [q_ref[0, i * ATT_SUB:(i + 1) * ATT_SUB, :] for i in range(nsub)]

    def scores(j):
        kc = k_ref[0, j * tk:(j + 1) * tk, :]
        return [lax.dot_general(q, kc, (((1,), (1,)), ((), ())), preferred_element_type=F32) for q in qs]

    ms = [jnp.full((ATT_SUB, 1), -1e30, F32) for _ in range(nsub)]
    accs = [jnp.zeros((ATT_SUB, HEAD_PAD), F32) for _ in range(nsub)]
    ss = scores(0)
    for j in range(nk):
        ss_next = scores(j + 1) if j + 1 < nk else None
        vc = v_ref[0, j * tk:(j + 1) * tk, :]
        for i in range(nsub):
            m_new = jnp.maximum(ms[i], jnp.max(ss[i], axis=1, keepdims=True))
            alpha = jnp.exp(ms[i] - m_new)
            p = jnp.exp(ss[i] - m_new).astype(BF16)
            accs[i] = accs[i] * alpha + jnp.dot(p, vc, preferred_element_type=F32)
            ms[i] = m_new
        ss = ss_next
    for i, acc in enumerate(accs):
        o_ref[0, i * ATT_SUB:(i + 1) * ATT_SUB, :] = (acc / acc[:, V_DIM:V_DIM + 1]).astype(BF16)


def attention(q, k, v, lq):
    bsz, lt, hp = q.shape
    tq = next(t for t in ATT_TQS if lq % t == 0)
    tk = next(t for t in ATT_TKS if lt % t == 0)
    kern = functools.partial(_attn_kernel, tk=tk, nk=lt // tk)
    return pl.pallas_call(
        kern,
        grid=(bsz, MLA_HEADS, lq // tq),
        in_specs=[pl.BlockSpec((1, tq, HEAD_PAD), lambda b, h, i: (b, i, h)),
                  pl.BlockSpec((1, lt, HEAD_PAD), lambda b, h, i: (b, 0, h)),
                  pl.BlockSpec((1, lt, HEAD_PAD), lambda b, h, i: (b, 0, h))],
        out_specs=pl.BlockSpec((1, tq, HEAD_PAD), lambda b, h, i: (b, i, h)),
        out_shape=jax.ShapeDtypeStruct((bsz, lq, hp), BF16),
        compiler_params=_cparams(("parallel", "parallel", "parallel")),
        name="attention",
    )(q, k, v)


def _rope_tables(n_lat, n_ctx):
    rows = n_lat // GRID_W
    row = jnp.repeat(jnp.arange(rows, dtype=F32), GRID_W)
    col = jnp.tile(jnp.arange(GRID_W, dtype=F32), rows)
    inv_freq = ROPE_BASE ** (-jnp.arange(0, ROPE_AXIS_DIMS, 2, dtype=F32) / ROPE_AXIS_DIMS)
    ang = jnp.concatenate([row[:, None] * inv_freq, col[:, None] * inv_freq], -1)
    cos = jnp.concatenate([jnp.cos(ang), jnp.ones((n_ctx, ROPE_AXIS_DIMS), F32)], 0)
    sin = jnp.concatenate([jnp.sin(ang), jnp.zeros((n_ctx, ROPE_AXIS_DIMS), F32)], 0)
    lt = n_lat + n_ctx
    one = jnp.ones((lt, QK_NOPE), F32)
    z32 = jnp.zeros((lt, HEAD_PAD - QK_NOPE - QK_ROPE), F32)
    z64 = jnp.zeros((lt, QK_NOPE), F32)
    ct = jnp.concatenate([one, cos, cos, z32], 1)
    st = jnp.concatenate([z64, -sin, sin, z32], 1)
    return ct, st


def _pad_heads(w, width, offset=0):
    k = w.shape[0]
    w = w.reshape(k, MLA_HEADS, width)
    w = jnp.pad(w, ((0, 0), (0, 0), (offset, HEAD_PAD - width - offset)))
    return w.reshape(k, MLA_HEADS * HEAD_PAD)


def mla_layer(h, n_ctx, g, mod, w_in, q_norm, q_up, kv_norm, kv_up, w_out):
    bsz, lt, d = h.shape
    n_lat = lt - n_ctx
    ct, st = _rope_tables(n_lat, n_ctx)
    half = ROPE_AXIS_DIMS
    swap = lambda t: jnp.concatenate([t[..., half:], t[..., :half]], -1)
    qup = _pad_heads(q_up, QK_NOPE + QK_ROPE).astype(BF16)
    q_pe = q_up.reshape(Q_LORA, MLA_HEADS, QK_NOPE + QK_ROPE)[:, :, QK_NOPE:]
    qsw = _pad_heads(swap(q_pe).reshape(Q_LORA, -1), QK_ROPE, QK_NOPE).astype(BF16)
    kvu = kv_up.reshape(KV_LORA, MLA_HEADS, QK_NOPE + V_DIM)
    kup = _pad_heads(kvu[:, :, :QK_NOPE].reshape(KV_LORA, -1), QK_NOPE).astype(BF16)
    vup = _pad_heads(kvu[:, :, QK_NOPE:].reshape(KV_LORA, -1), V_DIM).astype(BF16)
    eye = jnp.eye(QK_ROPE, dtype=F32)
    epl = _pad_heads(jnp.tile(eye, (1, MLA_HEADS)), QK_ROPE, QK_NOPE).astype(BF16)
    esw = _pad_heads(jnp.tile(swap(eye), (1, MLA_HEADS)), QK_ROPE, QK_NOPE).astype(BF16)
    one = _pad_heads(jnp.ones((1, MLA_HEADS), F32), 1, V_DIM)
    q, k, v = mla_proj(h, n_lat, g, mod, w_in.astype(BF16), q_norm[None], qup, qsw, kv_norm[None], kup, vup, epl,
                       esw, one, ct, st)
    o = attention(q, k, v, n_lat)
    wo = w_out.reshape(MLA_HEADS, V_DIM, d)
    wo = jnp.pad(wo, ((0, 0), (0, HEAD_PAD - V_DIM), (0, 0))).reshape(MLA_HEADS * HEAD_PAD, d).astype(BF16)
    return outproj(o, wo, h, n_lat, mod)


def _stream_rows(x_ref, c_ref, mod_ref, b, is_ctx, ks):
    hb = jnp.where(is_ctx, c_ref[b], x_ref[b])
    ms = [jnp.where(is_ctx, mod_ref[b, 1, k:k + 1, :], mod_ref[b, 0, k:k + 1, :]) for k in ks]
    return hb, ms


def _inproj2_kernel(x_ref, c_ref, g_ref, mod_ref, w_ref, p_ref, u_ref, *, nlb):
    is_ctx = pl.program_id(0) >= nlb
    bsz, rows, _ = x_ref.shape
    for b in range(bsz):
        hb, (shift, scale) = _stream_rows(x_ref, c_ref, mod_ref, b, is_ctx, (0, 1))
        a = _norm_mod(hb, g_ref[...], shift, scale)
        o = jnp.dot(a.astype(BF16), w_ref[...], preferred_element_type=F32)
        p_ref[b] = o[:, :RWKV_IN]
        for j in range(S5_TILES):
            u_ref[j, pl.ds(b, rows, stride=bsz), :] = o[:, RWKV_IN + j * LANES:RWKV_IN + (j + 1) * LANES]


def _stream_specs(x, ctx):
    bsz, n_lat, d = x.shape
    nlb = n_lat // SEQ_BLK
    return nlb, [pl.BlockSpec((bsz, SEQ_BLK, d), lambda i: (0, jnp.minimum(i, nlb - 1), 0)),
                 pl.BlockSpec((bsz, SEQ_BLK, d), lambda i: (0, jnp.maximum(i - nlb, 0), 0))]


def inproj2(x, ctx, g, mod, w):
    bsz, n_lat, d = x.shape
    lt = n_lat + ctx.shape[1]
    nlb, sspecs = _stream_specs(x, ctx)
    full = lambda a: pl.BlockSpec(a.shape, lambda i: (0,) * a.ndim)
    return pl.pallas_call(
        functools.partial(_inproj2_kernel, nlb=nlb),
        grid=(lt // SEQ_BLK,),
        in_specs=sspecs + [full(g), full(mod), full(w)],
        out_specs=[pl.BlockSpec((bsz, SEQ_BLK, RWKV_IN), lambda i: (0, i, 0)),
                   pl.BlockSpec((S5_TILES, bsz * SEQ_BLK, LANES), lambda i: (0, i, 0))],
        out_shape=[jax.ShapeDtypeStruct((bsz, lt, RWKV_IN), F32),
                   jax.ShapeDtypeStruct((S5_TILES, lt * bsz, LANES), F32)],
        compiler_params=_cparams(("parallel",)),
        name="inproj2",
    )(x, ctx, g, mod, w)


def _tri_inverse_all(mats, idx_r, idx_c):
    c = mats[0].shape[0]
    eye = (idx_r == idx_c).astype(F32)
    blk8 = (idx_r >> 3) == (idx_c >> 3)
    ns = [jnp.where(blk8, -a, 0.0) for a in mats]
    ts = [eye + n for n in ns]
    n2 = [_dot(n, n) for n in ns]
    ts = [t + _dot(m, t) for m, t in zip(n2, ts)]
    n4 = [_dot(m, m) for m in n2]
    ts = [t + _dot(m, t) for m, t in zip(n4, ts)]
    sh = 3
    while (1 << sh) < c:
        off = jnp.logical_and((idx_r >> (sh + 1)) == (idx_c >> (sh + 1)), (idx_r >> sh) != (idx_c >> sh))
        ms = [_dot(jnp.where(off, a, 0.0), t) for a, t in zip(mats, ts)]
        ts = [t - _dot(t, m) for m, t in zip(ms, ts)]
        sh += 1
    return ts


def _rwkv_conv(pc, prev_row, next_row, cw):
    c = pc.shape[0]
    row = lax.broadcasted_iota(jnp.int32, (c, 1), 0)
    xm1 = jnp.where(row == 0, prev_row, pltpu.roll(pc, 1, 0))
    xp1 = jnp.where(row == c - 1, next_row, pltpu.roll(pc, c - 1, 0))
    return cw[0:1] * xm1 + cw[1:2] * pc + cw[2:3] * xp1


def _rwkv_chunk_inputs(x, z, bi, w0, wup, a0, aup, kkv, kav):
    c = x.shape[0]
    n = RWKV_HEAD_DIM
    r = x[:, 0:RWKV_WIDTH]
    k = x[:, RWKV_WIDTH:2 * RWKV_WIDTH]
    v = x[:, 2 * RWKV_WIDTH:3 * RWKV_WIDTH]
    wd = x[:, 3 * RWKV_WIDTH:3 * RWKV_WIDTH + W_LORA]
    ad = x[:, 3 * RWKV_WIDTH + W_LORA:3 * RWKV_WIDTH + W_LORA + A_LORA]
    lw = -DECAY_SCALE * _sigmoid(w0 + _dot(jnp.tanh(wd), wup, True))
    a = _sigmoid(a0 + _dot(ad, aup, True))
    idx_r = lax.broadcasted_iota(jnp.int32, (c, c), 0)
    idx_c = lax.broadcasted_iota(jnp.int32, (c, c), 1)
    if z == 0:
        incl = idx_c <= idx_r
        strict = idx_c < idx_r
        last = c - 1
    else:
        incl = idx_c >= idx_r
        strict = idx_c > idx_r
        last = 0
    cum = _dot(incl.astype(F32), lw, True)
    tot = cum[last:last + 1]
    e1 = jnp.exp(cum)
    e2 = jnp.exp(cum - lw)
    e3 = jnp.exp(-cum)
    e4 = jnp.exp(tot - cum)
    etot = jnp.exp(tot)
    kkf = k * kkv
    heads = []
    for h in range(RWKV_HEADS):
        sl = slice(h * n, (h + 1) * n)
        kk = kkf[:, sl]
        kk = kk * lax.rsqrt(jnp.sum(kk * kk, -1, keepdims=True) + 1e-12)
        ah = a[:, sl]
        bh = ah * kk
        kd = k[:, sl] * (1.0 + (ah - 1.0) * kav[:, sl])
        heads.append(dict(
            xq=jnp.concatenate([kk * e2[:, sl], r[:, sl] * e1[:, sl]], 0),
            yk=jnp.concatenate([kd * e3[:, sl], bh * e3[:, sl]], 0),
            ke=jnp.concatenate([kd * e4[:, sl], bh * e4[:, sl]], 0),
            v=v[:, sl], etot=etot[:, sl], incl=incl, strict=strict, state=(z, bi, h)))
    return heads


def _rwkv_solve_all(heads, s_sc):
    c = heads[0]['v'].shape[0]
    idx_r = lax.broadcasted_iota(jnp.int32, (c, c), 0)
    idx_c = lax.broadcasted_iota(jnp.int32, (c, c), 1)
    gs = [_dot_nt(hd['xq'], hd['yk']) for hd in heads]
    ss = [s_sc[hd['state']] for hd in heads]
    xs = [_dot_nt(hd['xq'], s) for hd, s in zip(heads, ss)]
    avs = [_dot(jnp.where(hd['strict'], g[:c, :c], 0.0), hd['v']) for hd, g in zip(heads, gs)]
    ts = _tri_inverse_all([jnp.where(hd['strict'], g[:c, c:], 0.0) for hd, g in zip(heads, gs)], idx_r, idx_c)
    sas = [_dot(t, x[:c] + av) for t, x, av in zip(ts, xs, avs)]
    outs = []
    for hd, g, x, sa, s in zip(heads, gs, xs, sas, ss):
        a_r = jnp.concatenate([jnp.where(hd['incl'], g[c:, :c], 0.0), jnp.where(hd['incl'], -g[c:, c:], 0.0)], 1)
        outs.append(x[c:] + _dot(a_r, jnp.concatenate([hd['v'], sa], 0)))
        s_sc[hd['state']] = s * hd['etot'] + _dot_tn(jnp.concatenate([hd['v'], -sa], 0), hd['ke'])
    return outs


def _rwkv_kernel(pa_ref, pap_ref, pan_ref, pb_ref, pbp_ref, pbn_ref, cw_ref, w0_ref, wup_ref, a0_ref, aup_ref,
                 gup_ref, kk_ref, ka_ref, rk_ref, y0_ref, y1_ref, bonus_ref, g_ref, s_sc, *, nl, nc):
    s = pl.program_id(1)

    @pl.when(s == 0)
    def _():
        s_sc[...] = jnp.zeros_like(s_sc)

    ca = (s + nl) % nc
    cb = nc - 1 - s
    cw = cw_ref[...]

    def load(p_ref, pp_ref, pn_ref, cidx, bi):
        first = jnp.logical_or(cidx == 0, cidx == nl)
        lastc = jnp.logical_or(cidx == nl - 1, cidx == nc - 1)
        prev_row = jnp.where(first, 0.0, pp_ref[bi, 7:8, :])
        next_row = jnp.where(lastc, 0.0, pn_ref[bi, 0:1, :])
        return _rwkv_conv(p_ref[bi], prev_row, next_row, cw)

    nb = pa_ref.shape[0]
    heads = []
    xas = []
    for bi in range(nb):
        xa = load(pa_ref, pap_ref, pan_ref, ca, bi)
        xb = load(pb_ref, pbp_ref, pbn_ref, cb, bi)
        xas.append(xa)
        heads += _rwkv_chunk_inputs(xa, 0, bi, w0_ref[0:1], wup_ref[0], a0_ref[0:1], aup_ref[0], kk_ref[...],
                                    ka_ref[...])
        heads += _rwkv_chunk_inputs(xb, 1, bi, w0_ref[1:2], wup_ref[1], a0_ref[1:2], aup_ref[1], kk_ref[...],
                                    ka_ref[...])
    outs = _rwkv_solve_all(heads, s_sc)
    nh = RWKV_HEADS
    for bi in range(nb):
        y0_ref[bi] = jnp.concatenate(outs[2 * nh * bi:2 * nh * bi + nh], 1).astype(BF16)
        y1_ref[bi] = jnp.concatenate(outs[2 * nh * bi + nh:2 * nh * (bi + 1)], 1).astype(BF16)
        xa = xas[bi]
        gd = xa[:, 3 * RWKV_WIDTH + W_LORA + A_LORA:]
        g_ref[bi] = _dot(_sigmoid(gd), gup_ref[...], True).astype(BF16)
        r = xa[:, 0:RWKV_WIDTH]
        k = xa[:, RWKV_WIDTH:2 * RWKV_WIDTH]
        v = xa[:, 2 * RWKV_WIDTH:3 * RWKV_WIDTH]
        rkr = r * k * rk_ref[...]
        bon = []
        for h in range(nh):
            sl = slice(h * RWKV_HEAD_DIM, (h + 1) * RWKV_HEAD_DIM)
            bon.append(jnp.sum(rkr[:, sl], -1, keepdims=True) * v[:, sl])
        bonus_ref[bi] = jnp.concatenate(bon, 1).astype(BF16)


def rwkv_scan(p, n_lat, cw, w0, wup, a0, aup, gup, kkv, kav, rk):
    bsz, lt, _ = p.shape
    nl = n_lat // CHUNK
    nc = lt // CHUNK
    nb8 = lt // 8
    ca = lambda s: (s + nl) % nc
    cb = lambda s: nc - 1 - s
    nb = RWKV_NB if bsz % RWKV_NB == 0 else 1
    pspec = lambda cf: pl.BlockSpec((nb, CHUNK, RWKV_IN), lambda b, s: (b, cf(s), 0))
    prev = lambda cf: pl.BlockSpec((nb, 8, RWKV_IN), lambda b, s: (b, jnp.maximum(cf(s) * (CHUNK // 8) - 1, 0), 0))
    nxt = lambda cf: pl.BlockSpec((nb, 8, RWKV_IN),
                                  lambda b, s: (b, jnp.minimum((cf(s) + 1) * (CHUNK // 8), nb8 - 1), 0))
    full = lambda a: pl.BlockSpec(a.shape, lambda b, s: (0,) * a.ndim)
    ospec = lambda cf: pl.BlockSpec((nb, CHUNK, RWKV_WIDTH), lambda b, s: (b, cf(s), 0))
    oshape = jax.ShapeDtypeStruct((bsz, lt, RWKV_WIDTH), BF16)
    kern = functools.partial(_rwkv_kernel, nl=nl, nc=nc)
    params = (cw, w0, wup, a0, aup, gup, kkv, kav, rk)
    return pl.pallas_call(
        kern,
        grid=(bsz // nb, nc),
        in_specs=[pspec(ca), prev(ca), nxt(ca), pspec(cb), prev(cb), nxt(cb)] + [full(a) for a in params],
        out_specs=[ospec(ca), ospec(cb), ospec(ca), ospec(ca)],
        out_shape=[oshape] * 4,
        scratch_shapes=[pltpu.VMEM((2, nb, RWKV_HEADS, RWKV_HEAD_DIM, RWKV_HEAD_DIM), F32)],
        compiler_params=_cparams(("parallel", "arbitrary")),
        name="rwkv_scan",
    )(p, p, p, p, p, p, *params)


def _s5_kernel(uf_ref, ur_ref, lam_ref, bm_ref, cm_ref, yf_ref, yr_ref, bf_sc, br_sc, h_sc):
    s = pl.program_id(0)

    @pl.when(s == 0)
    def _():
        h_sc[...] = jnp.zeros_like(h_sc)

    rows = bf_sc.shape[0]
    win = S5_WIDTH // 2
    for z, (u_ref, sc) in enumerate(((uf_ref, bf_sc), (ur_ref, br_sc))):
        ub = jnp.concatenate([u_ref[j] for j in range(S5_TILES)], 1).astype(BF16)
        for k in range(2):
            sc[:, 2 * S5_HALF * k:2 * S5_HALF * (k + 1)] = jnp.dot(
                ub[:, win * k:win * (k + 1)], bm_ref[z, k], preferred_element_type=F32)
    ntile = rows // 8
    lo = lax.broadcasted_iota(jnp.int32, (8, 1), 0) < 4

    def body(jj, hs):
        sf = pl.multiple_of(jj * 8, 8)
        sr = pl.multiple_of((ntile - 1 - jj) * 8, 8)
        new = []
        for k in range(2):
            cre = pl.ds(2 * S5_HALF * k, S5_HALF)
            cim = pl.ds(2 * S5_HALF * k + S5_HALF, S5_HALF)
            lr = lam_ref[0, :, S5_HALF * k:S5_HALF * (k + 1)]
            li = lam_ref[1, :, S5_HALF * k:S5_HALF * (k + 1)]
            hr, hi = hs[2 * k], hs[2 * k + 1]
            tfr, tfi = bf_sc[pl.ds(sf, 8), cre], bf_sc[pl.ds(sf, 8), cim]
            trr, tri = br_sc[pl.ds(sr, 8), cre], br_sc[pl.ds(sr, 8), cim]
            xar, xai = jnp.where(lo, tfr, trr), jnp.where(lo, tfi, tri)
            xbr = pltpu.roll(jnp.where(lo, trr, tfr), 4, 0)
            xbi = pltpu.roll(jnp.where(lo, tri, tfi), 4, 0)
            ar, ai = lr * hr - li * hi + xar, lr * hi + li * hr + xai
            br, bi = lr * ar - li * ai + xbr, lr * ai + li * ar + xbi
            rbr, rbi = pltpu.roll(br, 4, 0), pltpu.roll(bi, 4, 0)
            bf_sc[pl.ds(sf, 8), cre] = jnp.where(lo, ar, rbr)
            bf_sc[pl.ds(sf, 8), cim] = jnp.where(lo, ai, rbi)
            br_sc[pl.ds(sr, 8), cre] = jnp.where(lo, rbr, ar)
            br_sc[pl.ds(sr, 8), cim] = jnp.where(lo, rbi, ai)
            new += [br, bi]
        return tuple(new)

    hs = lax.fori_loop(0, ntile, body, tuple(h_sc[i] for i in range(4)))
    for i in range(4):
        h_sc[i] = hs[i]
    for z, (sc, y_ref) in enumerate(((bf_sc, yf_ref), (br_sc, yr_ref))):
        for k in range(2):
            y = jnp.dot(sc[:, 2 * S5_HALF * k:2 * S5_HALF * (k + 1)].astype(BF16), cm_ref[z, k],
                        preferred_element_type=F32)
            for j in range(win // LANES):
                y_ref[k * (win // LANES) + j] = y[:, j * LANES:(j + 1) * LANES]


def s5_scan(u, n_lat, bsz, lam, bmat, cmat):
    assert bsz == 4
    rows_total = u.shape[1]
    lt = rows_total // bsz
    nl = n_lat // CHUNK
    nc = lt // CHUNK
    rb = CHUNK * bsz
    cf = lambda s: (0, (s + nl) % nc, 0)
    cr = lambda s: (0, nc - 1 - s, 0)
    full = lambda a: pl.BlockSpec(a.shape, lambda s: (0,) * a.ndim)
    oshape = jax.ShapeDtypeStruct(u.shape, F32)
    blk = (S5_TILES, rb, LANES)
    return pl.pallas_call(
        _s5_kernel,
        grid=(nc,),
        in_specs=[pl.BlockSpec(blk, cf), pl.BlockSpec(blk, cr), full(lam), full(bmat), full(cmat)],
        out_specs=[pl.BlockSpec(blk, cf), pl.BlockSpec(blk, cr)],
        out_shape=[oshape, oshape],
        scratch_shapes=[pltpu.VMEM((rb, 2 * S5_N), F32), pltpu.VMEM((rb, 2 * S5_N), F32),
                        pltpu.VMEM((4, 8, S5_HALF), F32)],
        compiler_params=_cparams(("arbitrary",)),
        name="s5_scan",
    )(u, u, lam, bmat, cmat)


def _s5_params(lam_re, lam_im, log_dt, b_re, b_im, c_re, c_im):
    dt = jnp.exp(log_dt.astype(F32))[..., None]
    mag = jnp.exp(lam_re * dt)
    lbr = mag * jnp.cos(lam_im * dt)
    lbi = mag * jnp.sin(lam_im * dt)
    den = lam_re * lam_re + lam_im * lam_im
    nr, ni = lbr - 1.0, lbi
    fr = (nr * lam_re + ni * lam_im) / den
    fi = (ni * lam_re - nr * lam_im) / den
    bbr = fr[..., None] * b_re - fi[..., None] * b_im
    bbi = fr[..., None] * b_im + fi[..., None] * b_re
    gh = S5_GROUPS // 2
    eye = jnp.eye(gh, dtype=F32)

    def bd_in(w):
        return jnp.einsum('gpc,gh->gchp', w, eye).reshape(S5_WIDTH // 2, S5_HALF)

    def bd_out(w):
        return jnp.einsum('gcp,gh->gphc', w, eye).reshape(S5_HALF, S5_WIDTH // 2)

    halves = [slice(0, gh), slice(gh, S5_GROUPS)]
    bmat = jnp.stack([jnp.stack([jnp.concatenate([bd_in(bbr[z, hs]), bd_in(bbi[z, hs])], 1) for hs in halves])
                      for z in range(2)])
    cmat = jnp.stack([jnp.stack([jnp.concatenate([bd_out(c_re[z, hs]), -bd_out(c_im[z, hs])], 0) for hs in halves])
                      for z in range(2)])
    rows = lambda t: jnp.concatenate([jnp.tile(t[0].reshape(1, S5_N), (4, 1)),
                                      jnp.tile(t[1].reshape(1, S5_N), (4, 1))], 0)
    lam = jnp.stack([rows(lbr), rows(lbi)])
    return lam, bmat.astype(BF16), cmat.astype(BF16)


def _gelu(x):
    return 0.5 * x * (1.0 + jnp.tanh(math.sqrt(2.0 / math.pi) * (x + 0.044715 * x * x * x)))


def _merge_kernel(y0_ref, y1_ref, bonus_ref, g_ref, s0_ref, s1_ref, u_ref, x_ref, c_ref, mod_ref, lnw_ref, lnb_ref,
                  d_ref, gluw_ref, glub_ref, wout_ref, o_ref, *, nlb):
    is_ctx = pl.program_id(0) >= nlb
    bsz, rows, _ = x_ref.shape
    n = RWKV_HEAD_DIM
    for b in range(bsz):
        y = y0_ref[b].astype(F32) + y1_ref[b].astype(F32)
        parts = []
        for h in range(RWKV_HEADS):
            yh = y[:, h * n:(h + 1) * n]
            mu = jnp.mean(yh, -1, keepdims=True)
            dlt = yh - mu
            var = jnp.mean(dlt * dlt, -1, keepdims=True)
            parts.append(dlt * lax.rsqrt(var + GN_EPS))
        yn = jnp.concatenate(parts, 1) * lnw_ref[...] + lnb_ref[...]
        rw = (yn + bonus_ref[b].astype(F32)) * g_ref[b].astype(F32)
        tm = pl.ds(b, rows, stride=bsz)
        seq = lambda ref: jnp.concatenate([ref[j, tm, :] for j in range(S5_TILES)], 1)
        ys = seq(s0_ref) + seq(s1_ref) + d_ref[...] * seq(u_ref)
        zz = _gelu(ys)
        gate = _sigmoid(_dot(zz, gluw_ref[...]) + glub_ref[...])
        cat = jnp.concatenate([rw, zz * gate], 1).astype(BF16)
        hb, (gate_res,) = _stream_rows(x_ref, c_ref, mod_ref, b, is_ctx, (2,))
        o_ref[b] = hb + gate_res * jnp.dot(cat, wout_ref[...], preferred_element_type=F32)


def merge(y0, y1, bonus, g, s0, s1, u, x, ctx, mod, lnw, lnb, dvec, gluw, glub, wout):
    bsz, n_lat, d = x.shape
    lt = n_lat + ctx.shape[1]
    nlb, sspecs = _stream_specs(x, ctx)
    rspec = pl.BlockSpec((bsz, SEQ_BLK, RWKV_WIDTH), lambda i: (0, i, 0))
    tspec = pl.BlockSpec((S5_TILES, bsz * SEQ_BLK, LANES), lambda i: (0, i, 0))
    full = lambda a: pl.BlockSpec(a.shape, lambda i: (0,) * a.ndim)
    params = (mod, lnw, lnb, dvec, gluw, glub, wout)
    return pl.pallas_call(
        functools.partial(_merge_kernel, nlb=nlb),
        grid=(lt // SEQ_BLK,),
        in_specs=[rspec, rspec, rspec, rspec, tspec, tspec, tspec] + sspecs + [full(a) for a in params],
        out_specs=pl.BlockSpec((bsz, SEQ_BLK, d), lambda i: (0, i, 0)),
        out_shape=jax.ShapeDtypeStruct((bsz, lt, d), F32),
        compiler_params=_cparams(("parallel",)),
        name="merge",
    )(y0, y1, bonus, g, s0, s1, u, x, ctx, *params)


def hybrid_layer(x, ctx, g1, mod, w_in, w_out, conv_w, w0, w_up, a0, a_up, g_up, k_k, k_a, r_k, ln_w, ln_b,
                 lam_re, lam_im, log_dt, b_re, b_im, c_re, c_im, dvec, glu_w, glu_b):
    bsz, n_lat, _ = x.shape
    p, u = inproj2(x, ctx, g1, mod, w_in.astype(BF16))
    y0, y1, bonus, g = rwkv_scan(p, n_lat, conv_w, w0, w_up, a0, a_up, g_up, k_k[None], k_a[None],
                                 r_k.reshape(1, RWKV_WIDTH))
    lam, bmat, cmat = _s5_params(lam_re, lam_im, log_dt, b_re, b_im, c_re, c_im)
    s0, s1 = s5_scan(u, n_lat, bsz, lam, bmat, cmat)
    gluw = jnp.einsum('gce,gh->gche', glu_w, jnp.eye(S5_GROUPS, dtype=F32)).reshape(S5_WIDTH, S5_WIDTH)
    return merge(y0, y1, bonus, g, s0, s1, u, x, ctx, mod, ln_w[None], ln_b[None], dvec[None],
                 gluw.astype(BF16), glu_b[None], w_out.astype(BF16))


def _stack_experts(ex, sh):
    return jnp.concatenate([ex, sh[None]], 0).astype(BF16)


def kernel(x, c, ctx, c_ctx, mod_w, mod_b, norm1_g, norm2_g, final_g, hy_w_in, hy_w_out, rk_conv, rk_w0, rk_w_up, rk_a0, rk_a_up, rk_g_up, rk_k_k, rk_k_a, rk_r_k, rk_ln_w, rk_ln_b, s5_lam_re, s5_lam_im, s5_log_dt, s5_b_re, s5_b_im, s5_c_re, s5_c_im, s5_d, s5_glu_w, s5_glu_b, mla_w_in, mla_q_norm, mla_q_up, mla_kv_norm, mla_kv_up, mla_w_out, router_w, router_b, ex_gate, ex_up, ex_down, sh_gate, sh_up, sh_down):
    bsz, n_lat, d = x.shape
    n_ctx = ctx.shape[1]
    depth = mod_w.shape[0]
    assert n_ctx == ROW_BLK and depth == 2 and bsz <= 7
    cc = jnp.concatenate([c, c_ctx[None], jnp.zeros((8 - bsz - 1, d), F32)], 0)
    rb = router_b[:, None].astype(F32)
    rw_t = router_w.T
    fg = final_g[None]
    for layer in range(depth):
        last = layer == depth - 1
        i = layer // 2
        mv = adaln(cc, mod_w[layer], mod_b[layer][None])
        m_l = mv[:bsz].reshape(bsz, 1, 6, d)
        m_c = jnp.broadcast_to(mv[bsz].reshape(1, 1, 6, d), (bsz, 1, 6, d))
        mod = jnp.concatenate([m_l, m_c], 1)
        g1 = norm1_g[layer][None]
        g2 = norm2_g[layer][None]
        if layer % 2 == 0:
            h = hybrid_layer(x, ctx, g1, mod, hy_w_in[i], hy_w_out[i], rk_conv[i], rk_w0[i], rk_w_up[i], rk_a0[i],
                             rk_a_up[i], rk_g_up[i], rk_k_k[i], rk_k_a[i], rk_r_k[i], rk_ln_w[i], rk_ln_b[i],
                             s5_lam_re[i], s5_lam_im[i], s5_log_dt[i], s5_b_re[i], s5_b_im[i], s5_c_re[i],
                             s5_c_im[i], s5_d[i], s5_glu_w[i], s5_glu_b[i])
        else:
            h = mla_layer(h, n_ctx, g1, mod, mla_w_in[i], mla_q_norm[i], mla_q_up[i], mla_kv_norm[i],
                          mla_kv_up[i], mla_w_out[i])
        wg = _stack_experts(ex_gate[layer], sh_gate[layer])
        wu = _stack_experts(ex_up[layer], sh_up[layer])
        wd = _stack_experts(ex_down[layer], sh_down[layer])
        h = moe(h, n_lat, g2, mod, rw_t, rb, wg, wu, wd, fg, last)
    return h
```

```python
import functools
import math

import jax
import jax.numpy as jnp
from jax import lax
from jax.experimental import pallas as pl
from jax.experimental.pallas import tpu as pltpu

F32 = jnp.float32
BF16 = jnp.bfloat16
HI = lax.Precision.HIGHEST

D_MODEL = 1024
GRID_W = 64
NORM_EPS = 1e-6

RWKV_HEADS = 8
RWKV_HEAD_DIM = 64
RWKV_WIDTH = 512
W_LORA = 64
A_LORA = 64
G_LORA = 128
RWKV_IN = 3 * RWKV_WIDTH + W_LORA + A_LORA + G_LORA
DECAY_SCALE = math.exp(-0.5)
GN_EPS = 64e-5

S5_GROUP = 16
S5_GROUPS = 32
S5_WIDTH = 512
S5_STATE = 64
S5_N = S5_GROUPS * S5_STATE
S5_HALF = S5_N // 2
LANES = 128
S5_TILES = S5_WIDTH // LANES
HYB_IN = RWKV_IN + S5_WIDTH

MLA_HEADS = 16
Q_LORA = 256
KV_LORA = 128
QK_NOPE = 64
QK_ROPE = 32
V_DIM = 64
MLA_IN = Q_LORA + KV_LORA + QK_ROPE
MLA_SCALE = (QK_NOPE + QK_ROPE) ** -0.5
ROPE_AXIS_DIMS = QK_ROPE // 2
ROPE_BASE = 10000.0
HEAD_PAD = 128

N_EXPERTS = 16
N_GROUPS = 4
EXPERTS_PER_GROUP = 4
D_EXPERT = 256

ROW_BLK = 256
SEQ_BLK = 128
MOE_BLKS = (1024, 768, 512, 256)
MOE_EXPERTS_PER_STEP = 4
ATT_TQS = (1024, 512, 256)
ATT_TKS = (1408, 768, 256)
ATT_SUB = 256
CHUNK = 64
RWKV_NB = 2
VMEM_LIMIT = 56 * 1024 * 1024


def _cparams(sem):
    return pltpu.CompilerParams(dimension_semantics=sem, vmem_limit_bytes=VMEM_LIMIT)


def _norm_mod(x, g, shift, scale):
    y = x * lax.rsqrt(jnp.mean(x * x, -1, keepdims=True) + NORM_EPS) * g
    return y * (1.0 + scale) + shift


def _sigmoid(x):
    return 1.0 / (1.0 + jnp.exp(-x))


def _dot(a, b, hi=False):
    if hi:
        return jnp.dot(a, b, precision=HI, preferred_element_type=F32)
    return jnp.dot(a.astype(BF16), b.astype(BF16), preferred_element_type=F32)


def _dot_nt(a, b):
    return lax.dot_general(a.astype(BF16), b.astype(BF16), (((1,), (1,)), ((), ())), preferred_element_type=F32)


def _dot_tn(a, b):
    return jnp.dot(a.T.astype(BF16), b.astype(BF16), preferred_element_type=F32)


def _adaln_kernel(s_ref, w_ref, b_ref, o_ref):
    s = s_ref[...]
    s = s * _sigmoid(s)
    o_ref[...] = jnp.dot(s, w_ref[...], precision=HI, preferred_element_type=F32) + b_ref[...]


def adaln(cc, w, b):
    n = w.shape[1]
    tn = 512
    return pl.pallas_call(
        _adaln_kernel,
        grid=(n // tn,),
        in_specs=[pl.BlockSpec((8, D_MODEL), lambda j: (0, 0)),
                  pl.BlockSpec((D_MODEL, tn), lambda j: (0, j)),
                  pl.BlockSpec((1, tn), lambda j: (0, j))],
        out_specs=pl.BlockSpec((8, tn), lambda j: (0, j)),
        out_shape=jax.ShapeDtypeStruct((8, n), F32),
        compiler_params=_cparams(("parallel",)),
        name="adaln",
    )(cc, w, b)


def _mod_spec(n_lat, blk=ROW_BLK):
    return pl.BlockSpec((1, 1, 6, D_MODEL), lambda b, i, *_: (b, i // (n_lat // blk), 0, 0))


def _outproj_kernel(a_ref, w_ref, h_ref, mod_ref, o_ref):
    m = mod_ref[0, 0]
    o = jnp.dot(a_ref[0].astype(BF16), w_ref[...], preferred_element_type=F32)
    o_ref[0] = h_ref[0] + m[2:3] * o


def outproj(a, w, h, n_lat, mod):
    bsz, la, k = a.shape
    d = h.shape[2]
    return pl.pallas_call(
        _outproj_kernel,
        grid=(bsz, la // ROW_BLK),
        in_specs=[pl.BlockSpec((1, ROW_BLK, k), lambda b, i: (b, i, 0)),
                  pl.BlockSpec((k, d), lambda b, i: (0, 0)),
                  pl.BlockSpec((1, ROW_BLK, d), lambda b, i: (b, i, 0)),
                  _mod_spec(n_lat)],
        out_specs=pl.BlockSpec((1, ROW_BLK, d), lambda b, i: (b, i, 0)),
        out_shape=jax.ShapeDtypeStruct((bsz, la, d), F32),
        compiler_params=_cparams(("parallel", "parallel")),
        name="outproj",
    )(a, w, h, mod)


def _route(scores, rb):
    t = scores.shape[1]
    biased = scores + rb
    col = [biased[e:e + 1, :] for e in range(N_EXPERTS)]
    sc = [scores[e:e + 1, :] for e in range(N_EXPERTS)]
    gscore = []
    for gi in range(N_GROUPS):
        a, b, c, d = col[4 * gi:4 * gi + 4]
        hi1, lo1 = jnp.maximum(a, b), jnp.minimum(a, b)
        hi2, lo2 = jnp.maximum(c, d), jnp.minimum(c, d)
        gscore.append(jnp.maximum(hi1, hi2) + jnp.maximum(jnp.minimum(hi1, hi2), jnp.maximum(lo1, lo2)))
    gsel = []
    taken = None
    for gi in range(N_GROUPS):
        best = None
        for gj in range(gi + 1, N_GROUPS):
            best = gscore[gj] if best is None else jnp.maximum(best, gscore[gj])
        s = (gscore[gi] >= best) if best is not None else jnp.full((1, t), True)
        if taken is not None:
            s = jnp.logical_and(s, jnp.logical_not(taken))
        taken = s if taken is None else jnp.logical_or(taken, s)
        gsel.append(s)
    masks = []
    for gi in range(N_GROUPS):
        v = col[4 * gi:4 * gi + 4]
        for j in range(4):
            rank = jnp.zeros((1, t), F32)
            for i in range(4):
                if i == j:
                    continue
                ahead = (v[i] >= v[j]) if i < j else (v[i] > v[j])
                rank = rank + ahead.astype(F32)
            masks.append(jnp.logical_and(gsel[gi], rank < 2.0))
    wsel = [jnp.where(masks[e], sc[e], 0.0) for e in range(N_EXPERTS)]
    denom = wsel[0]
    for e in range(1, N_EXPERTS):
        denom = denom + wsel[e]
    return [w / denom for w in wsel]


def _moe_kernel(h_ref, g_ref, mod_ref, rw_ref, rb_ref, wg_ref, wu_ref, wd_ref, sg_ref, su_ref, sd_ref, fg_ref, o_ref,
                f_sc, comb_sc, acc_sc, *, final_norm, n_lat, has_ctx):
    i = pl.program_id(1)
    e = pl.program_id(2)
    n_e = pl.num_programs(2)
    tm = f_sc.shape[0]

    def mod_row(k):
        lat = mod_ref[0, 0, k:k + 1, :]
        if not has_ctx:
            return lat
        row = i * tm + lax.broadcasted_iota(jnp.int32, (tm, 1), 0)
        return jnp.where(row >= n_lat, mod_ref[0, 1, k:k + 1, :], lat)

    @pl.when(e == 0)
    def _():
        f = _norm_mod(h_ref[0], g_ref[...], mod_row(3), mod_row(4))
        f_sc[...] = f.astype(BF16)
        logits = lax.dot_general(rw_ref[...], f, (((1,), (1,)), ((), ())), precision=HI,
                                 preferred_element_type=F32)
        cw = _route(_sigmoid(logits), rb_ref[...])
        sub = lax.broadcasted_iota(jnp.int32, (128, tm), 0)
        comb = jnp.zeros((128, tm), F32)
        for ei in range(N_EXPERTS):
            comb = jnp.where(sub == ei, cw[ei], comb)
        comb_sc[...] = comb.T

    fb = f_sc[...]

    @pl.when(e == 0)
    def _():
        gt = jnp.dot(fb, sg_ref[...], preferred_element_type=F32)
        up = jnp.dot(fb, su_ref[...], preferred_element_type=F32)
        act = gt * _sigmoid(gt) * up
        acc_sc[...] = jnp.dot(act.astype(BF16), sd_ref[...], preferred_element_type=F32)

    eps = wg_ref.shape[0]
    lane = lax.broadcasted_iota(jnp.int32, (tm, 128), 1)
    comb = comb_sc[...]
    gts = [jnp.dot(fb, wg_ref[j], preferred_element_type=F32) for j in range(eps)]
    ups = [jnp.dot(fb, wu_ref[j], preferred_element_type=F32) for j in range(eps)]
    tot = None
    for j in range(eps):
        cw = jnp.sum(jnp.where(lane == e * eps + j, comb, 0.0), axis=1, keepdims=True)
        act = gts[j] * _sigmoid(gts[j]) * ups[j] * cw
        dn = jnp.dot(act.astype(BF16), wd_ref[j], preferred_element_type=F32)
        tot = dn if tot is None else tot + dn
    acc_sc[...] += tot

    @pl.when(e == n_e - 1)
    def _():
        y = h_ref[0] + mod_row(5) * acc_sc[...]
        if final_norm:
            y = y * lax.rsqrt(jnp.mean(y * y, -1, keepdims=True) + NORM_EPS) * fg_ref[...]
        o_ref[0] = y


def moe(h, n_lat, g, mod, rw, rb, wg, wu, wd, sg, su, sd, fg, final_norm):
    bsz, lo, d = h.shape
    eps = MOE_EXPERTS_PER_STEP
    tm = next(t for t in MOE_BLKS if lo % t == 0)
    kern = functools.partial(_moe_kernel, final_norm=final_norm, n_lat=n_lat, has_ctx=lo > n_lat)
    return pl.pallas_call(
        kern,
        grid=(bsz, lo // tm, N_EXPERTS // eps),
        in_specs=[pl.BlockSpec((1, tm, d), lambda b, i, e: (b, i, 0)),
                  pl.BlockSpec((1, d), lambda b, i, e: (0, 0)),
                  pl.BlockSpec((1, 2, 6, d), lambda b, i, e: (b, 0, 0, 0)),
                  pl.BlockSpec((N_EXPERTS, d), lambda b, i, e: (0, 0)),
                  pl.BlockSpec((N_EXPERTS, 1), lambda b, i, e: (0, 0)),
                  pl.BlockSpec((eps, d, D_EXPERT), lambda b, i, e: (e, 0, 0)),
                  pl.BlockSpec((eps, d, D_EXPERT), lambda b, i, e: (e, 0, 0)),
                  pl.BlockSpec((eps, D_EXPERT, d), lambda b, i, e: (e, 0, 0)),
                  pl.BlockSpec((d, D_EXPERT), lambda b, i, e: (0, 0)),
                  pl.BlockSpec((d, D_EXPERT), lambda b, i, e: (0, 0)),
                  pl.BlockSpec((D_EXPERT, d), lambda b, i, e: (0, 0)),
                  pl.BlockSpec((1, d), lambda b, i, e: (0, 0))],
        out_specs=pl.BlockSpec((1, tm, d), lambda b, i, e: (b, i, 0)),
        out_shape=jax.ShapeDtypeStruct((bsz, lo, d), F32),
        scratch_shapes=[pltpu.VMEM((tm, d), BF16),
                        pltpu.VMEM((tm, 128), F32),
                        pltpu.VMEM((tm, d), F32)],
        compiler_params=_cparams(("parallel", "parallel", "arbitrary")),
        name="moe",
    )(h, g, mod, rw, rb, wg, wu, wd, sg, su, sd, fg)


def _mla_proj_kernel(h_ref, g_ref, mod_ref, win_ref, qn_ref, qup_ref, qsw_ref, kvn_ref, kup_ref, vup_ref, epl_ref,
                     esw_ref, one_ref, ct_ref, st_ref, q_ref, k_ref, v_ref):
    m = mod_ref[0, 0]
    a = _norm_mod(h_ref[0], g_ref[...], m[0:1], m[1:2])
    p = jnp.dot(a.astype(BF16), win_ref[...], preferred_element_type=F32)
    qd = p[:, :Q_LORA]
    kvd = p[:, Q_LORA:Q_LORA + KV_LORA]
    kpe = p[:, Q_LORA + KV_LORA:]
    qn = qd * lax.rsqrt(jnp.mean(qd * qd, -1, keepdims=True) + NORM_EPS) * qn_ref[...]
    kvn = kvd * lax.rsqrt(jnp.mean(kvd * kvd, -1, keepdims=True) + NORM_EPS) * kvn_ref[...]
    kvb = kvn.astype(BF16)
    ct = jnp.concatenate([ct_ref[...]] * MLA_HEADS, axis=1)
    st = jnp.concatenate([st_ref[...]] * MLA_HEADS, axis=1)
    qb = qn.astype(BF16)
    q = jnp.dot(qb, qup_ref[...], preferred_element_type=F32)
    q_sw = jnp.dot(qb, qsw_ref[...], preferred_element_type=F32)
    q_ref[0] = ((q * ct + q_sw * st) * MLA_SCALE).astype(BF16)
    kpb = kpe.astype(BF16)
    k = (jnp.dot(kvb, kup_ref[...], preferred_element_type=F32)
         + jnp.dot(kpb, epl_ref[...], preferred_element_type=F32))
    k_sw = jnp.dot(kpb, esw_ref[...], preferred_element_type=F32)
    k_ref[0] = (k * ct + k_sw * st).astype(BF16)
    v = jnp.dot(kvb, vup_ref[...], preferred_element_type=F32) + one_ref[...]
    v_ref[0] = v.astype(BF16)


def mla_proj(h, n_lat, g, mod, win, qn, qup, qsw, kvn, kup, vup, epl, esw, one, ct, st):
    bsz, lt, d = h.shape
    hp = MLA_HEADS * HEAD_PAD
    full = lambda shape: pl.BlockSpec(shape, lambda b, i: (0,) * len(shape))
    tab = pl.BlockSpec((ROW_BLK, HEAD_PAD), lambda b, i: (i, 0))
    out = pl.BlockSpec((1, ROW_BLK, hp), lambda b, i: (b, i, 0))
    return pl.pallas_call(
        _mla_proj_kernel,
        grid=(bsz, lt // ROW_BLK),
        in_specs=[pl.BlockSpec((1, ROW_BLK, d), lambda b, i: (b, i, 0)),
                  full((1, d)), _mod_spec(n_lat), full((d, MLA_IN)), full((1, Q_LORA)), full((Q_LORA, hp)),
                  full((Q_LORA, hp)), full((1, KV_LORA)), full((KV_LORA, hp)), full((KV_LORA, hp)),
                  full((QK_ROPE, hp)), full((QK_ROPE, hp)), full((1, hp)), tab, tab],
        out_specs=[out, out, out],
        out_shape=[jax.ShapeDtypeStruct((bsz, lt, hp), BF16)] * 3,
        compiler_params=_cparams(("parallel", "parallel")),
        name="mla_proj",
    )(h, g, mod, win, qn, qup, qsw, kvn, kup, vup, epl, esw, one, ct, st)


def _attn_kernel(q_ref, k_ref, v_ref, o_ref, *, tk, nk):
    tq = q_ref.shape[1]
    nsub = tq // ATT_SUB
    qs = [q_ref[0, i * ATT_SUB:(i + 1) * ATT_SUB, :] for i in range(nsub)]

    def scores(j):
        kc = k_ref[0, j * tk:(j + 1) * tk, :]
        return [lax.dot_general(q, kc, (((1,), (1,)), ((), ())), preferred_element_type=F32) for q in qs]

    ms = [jnp.full((ATT_SUB, 1), -1e30, F32) for _ in range(nsub)]
    accs = [jnp.zeros((ATT_SUB, HEAD_PAD), F32) for _ in range(nsub)]
    ss = scores(0)
    for j in range(nk):
        ss_next = scores(j + 1) if j + 1 < nk else None
        vc = v_ref[0, j * tk:(j + 1) * tk, :]
        for i in range(nsub):
            m_new = jnp.maximum(ms[i], jnp.max(ss[i], axis=1, keepdims=True))
            alpha = jnp.exp(ms[i] - m_new)
            p = jnp.exp(ss[i] - m_new).astype(BF16)
            accs[i] = accs[i] * alpha + jnp.dot(p, vc, preferred_element_type=F32)
            ms[i] = m_new
        ss = ss_next
    for i, acc in enumerate(accs):
        o_ref[0, i * ATT_SUB:(i + 1) * ATT_SUB, :] = (acc / acc[:, V_DIM:V_DIM + 1]).astype(BF16)


def attention(q, k, v, lq):
    bsz, lt, hp = q.shape
    tq = next(t for t in ATT_TQS if lq % t == 0)
    tk = next(t for t in ATT_TKS if lt % t == 0)
    kern = functools.partial(_attn_kernel, tk=tk, nk=lt // tk)
    return pl.pallas_call(
        kern,
        grid=(bsz, MLA_HEADS, lq // tq),
        in_specs=[pl.BlockSpec((1, tq, HEAD_PAD), lambda b, h, i: (b, i, h)),
                  pl.BlockSpec((1, lt, HEAD_PAD), lambda b, h, i: (b, 0, h)),
                  pl.BlockSpec((1, lt, HEAD_PAD), lambda b, h, i: (b, 0, h))],
        out_specs=pl.BlockSpec((1, tq, HEAD_PAD), lambda b, h, i: (b, i, h)),
        out_shape=jax.ShapeDtypeStruct((bsz, lq, hp), BF16),
        compiler_params=_cparams(("parallel", "parallel", "parallel")),
        name="attention",
    )(q, k, v)


def _rope_tables(n_lat, n_ctx):
    rows = n_lat // GRID_W
    row = jnp.repeat(jnp.arange(rows, dtype=F32), GRID_W)
    col = jnp.tile(jnp.arange(GRID_W, dtype=F32), rows)
    inv_freq = ROPE_BASE ** (-jnp.arange(0, ROPE_AXIS_DIMS, 2, dtype=F32) / ROPE_AXIS_DIMS)
    ang = jnp.concatenate([row[:, None] * inv_freq, col[:, None] * inv_freq], -1)
    cos = jnp.concatenate([jnp.cos(ang), jnp.ones((n_ctx, ROPE_AXIS_DIMS), F32)], 0)
    sin = jnp.concatenate([jnp.sin(ang), jnp.zeros((n_ctx, ROPE_AXIS_DIMS), F32)], 0)
    lt = n_lat + n_ctx
    one = jnp.ones((lt, QK_NOPE), F32)
    z32 = jnp.zeros((lt, HEAD_PAD - QK_NOPE - QK_ROPE), F32)
    z64 = jnp.zeros((lt, QK_NOPE), F32)
    ct = jnp.concatenate([one, cos, cos, z32], 1)
    st = jnp.concatenate([z64, -sin, sin, z32], 1)
    return ct, st


def _pad_heads(w, width, offset=0):
    k = w.shape[0]
    w = w.reshape(k, MLA_HEADS, width)
    w = jnp.pad(w, ((0, 0), (0, 0), (offset, HEAD_PAD - width - offset)))
    return w.reshape(k, MLA_HEADS * HEAD_PAD)


def mla_layer(h, n_ctx, g, mod, w_in, q_norm, q_up, kv_norm, kv_up, w_out):
    bsz, lt, d = h.shape
    n_lat = lt - n_ctx
    ct, st = _rope_tables(n_lat, n_ctx)
    half = ROPE_AXIS_DIMS
    swap = lambda t: jnp.concatenate([t[..., half:], t[..., :half]], -1)
    qup = _pad_heads(q_up, QK_NOPE + QK_ROPE).astype(BF16)
    q_pe = q_up.reshape(Q_LORA, MLA_HEADS, QK_NOPE + QK_ROPE)[:, :, QK_NOPE:]
    qsw = _pad_heads(swap(q_pe).reshape(Q_LORA, -1), QK_ROPE, QK_NOPE).astype(BF16)
    kvu = kv_up.reshape(KV_LORA, MLA_HEADS, QK_NOPE + V_DIM)
    kup = _pad_heads(kvu[:, :, :QK_NOPE].reshape(KV_LORA, -1), QK_NOPE).astype(BF16)
    vup = _pad_heads(kvu[:, :, QK_NOPE:].reshape(KV_LORA, -1), V_DIM).astype(BF16)
    eye = jnp.eye(QK_ROPE, dtype=F32)
    epl = _pad_heads(jnp.tile(eye, (1, MLA_HEADS)), QK_ROPE, QK_NOPE).astype(BF16)
    esw = _pad_heads(jnp.tile(swap(eye), (1, MLA_HEADS)), QK_ROPE, QK_NOPE).astype(BF16)
    one = _pad_heads(jnp.ones((1, MLA_HEADS), F32), 1, V_DIM)
    q, k, v = mla_proj(h, n_lat, g, mod, w_in.astype(BF16), q_norm[None], qup, qsw, kv_norm[None], kup, vup, epl,
                       esw, one, ct, st)
    o = attention(q, k, v, n_lat)
    wo = w_out.reshape(MLA_HEADS, V_DIM, d)
    wo = jnp.pad(wo, ((0, 0), (0, HEAD_PAD - V_DIM), (0, 0))).reshape(MLA_HEADS * HEAD_PAD, d).astype(BF16)
    return outproj(o, wo, h, n_lat, mod)


def _stream_rows(x_ref, c_ref, mod_ref, b, is_ctx, ks):
    rows, d = x_ref.shape[1:]
    lat_rows = jnp.where(is_ctx, 0, rows)
    hb = jnp.where(lax.broadcasted_iota(jnp.int32, (rows, d), 0) < lat_rows, x_ref[b], c_ref[b])
    lat1 = lax.broadcasted_iota(jnp.int32, (1, d), 0) < lat_rows
    ms = [jnp.where(lat1, mod_ref[b, 0, k:k + 1, :], mod_ref[b, 1, k:k + 1, :]) for k in ks]
    return hb, ms


def _inproj2_kernel(x_ref, c_ref, g_ref, mod_ref, w_ref, p_ref, u_ref, *, nlb):
    is_ctx = pl.program_id(0) >= nlb
    bsz, rows, _ = x_ref.shape
    for b in range(bsz):
        hb, (shift, scale) = _stream_rows(x_ref, c_ref, mod_ref, b, is_ctx, (0, 1))
        a = _norm_mod(hb, g_ref[...], shift, scale)
        o = jnp.dot(a.astype(BF16), w_ref[...], preferred_element_type=F32)
        p_ref[b] = o[:, :RWKV_IN]
        for j in range(S5_TILES):
            u_ref[j, pl.ds(b, rows, stride=bsz), :] = o[:, RWKV_IN + j * LANES:RWKV_IN + (j + 1) * LANES]


def _stream_specs(x, ctx):
    bsz, n_lat, d = x.shape
    nlb = n_lat // SEQ_BLK
    return nlb, [pl.BlockSpec((bsz, SEQ_BLK, d), lambda i: (0, jnp.minimum(i, nlb - 1), 0)),
                 pl.BlockSpec((bsz, SEQ_BLK, d), lambda i: (0, jnp.maximum(i - nlb, 0), 0))]


def inproj2(x, ctx, g, mod, w):
    bsz, n_lat, d = x.shape
    lt = n_lat + ctx.shape[1]
    nlb, sspecs = _stream_specs(x, ctx)
    full = lambda a: pl.BlockSpec(a.shape, lambda i: (0,) * a.ndim)
    return pl.pallas_call(
        functools.partial(_inproj2_kernel, nlb=nlb),
        grid=(lt // SEQ_BLK,),
        in_specs=sspecs + [full(g), full(mod), full(w)],
        out_specs=[pl.BlockSpec((bsz, SEQ_BLK, RWKV_IN), lambda i: (0, i, 0)),
                   pl.BlockSpec((S5_TILES, bsz * SEQ_BLK, LANES), lambda i: (0, i, 0))],
        out_shape=[jax.ShapeDtypeStruct((bsz, lt, RWKV_IN), F32),
                   jax.ShapeDtypeStruct((S5_TILES, lt * bsz, LANES), F32)],
        compiler_params=_cparams(("parallel",)),
        name="inproj2",
    )(x, ctx, g, mod, w)


def _tri_inverse_all(mats, idx_r, idx_c):
    c = mats[0].shape[0]
    eye = (idx_r == idx_c).astype(F32)
    blk8 = (idx_r >> 3) == (idx_c >> 3)
    ns = [jnp.where(blk8, -a, 0.0) for a in mats]
    ts = [eye + n for n in ns]
    n2 = [_dot(n, n) for n in ns]
    ts = [t + _dot(m, t) for m, t in zip(n2, ts)]
    n4 = [_dot(m, m) for m in n2]
    ts = [t + _dot(m, t) for m, t in zip(n4, ts)]
    sh = 3
    while (1 << sh) < c:
        off = jnp.logical_and((idx_r >> (sh + 1)) == (idx_c >> (sh + 1)), (idx_r >> sh) != (idx_c >> sh))
        ms = [_dot(jnp.where(off, a, 0.0), t) for a, t in zip(mats, ts)]
        ts = [t - _dot(t, m) for m, t in zip(ms, ts)]
        sh += 1
    return ts


def _rwkv_conv(pc, prev_row, next_row, cw):
    c = pc.shape[0]
    row = lax.broadcasted_iota(jnp.int32, (c, 1), 0)
    xm1 = jnp.where(row == 0, prev_row, pltpu.roll(pc, 1, 0))
    xp1 = jnp.where(row == c - 1, next_row, pltpu.roll(pc, c - 1, 0))
    return cw[0:1] * xm1 + cw[1:2] * pc + cw[2:3] * xp1


def _rwkv_chunk_inputs(x, z, bi, w0, wup, a0, aup, kkv, kav):
    c = x.shape[0]
    n = RWKV_HEAD_DIM
    r = x[:, 0:RWKV_WIDTH]
    k = x[:, RWKV_WIDTH:2 * RWKV_WIDTH]
    v = x[:, 2 * RWKV_WIDTH:3 * RWKV_WIDTH]
    wd = x[:, 3 * RWKV_WIDTH:3 * RWKV_WIDTH + W_LORA]
    ad = x[:, 3 * RWKV_WIDTH + W_LORA:3 * RWKV_WIDTH + W_LORA + A_LORA]
    lw = -DECAY_SCALE * _sigmoid(w0 + _dot(jnp.tanh(wd), wup, True))
    a = _sigmoid(a0 + _dot(ad, aup, True))
    idx_r = lax.broadcasted_iota(jnp.int32, (c, c), 0)
    idx_c = lax.broadcasted_iota(jnp.int32, (c, c), 1)
    if z == 0:
        incl = idx_c <= idx_r
        strict = idx_c < idx_r
        last = c - 1
    else:
        incl = idx_c >= idx_r
        strict = idx_c > idx_r
        last = 0
    cum = _dot(incl.astype(F32), lw, True)
    tot = cum[last:last + 1]
    e1 = jnp.exp(cum)
    e2 = jnp.exp(cum - lw)
    e3 = jnp.exp(-cum)
    e4 = jnp.exp(tot - cum)
    etot = jnp.exp(tot)
    kkf = k * kkv
    heads = []
    for h in range(RWKV_HEADS):
        sl = slice(h * n, (h + 1) * n)
        kk = kkf[:, sl]
        kk = kk * lax.rsqrt(jnp.sum(kk * kk, -1, keepdims=True) + 1e-12)
        ah = a[:, sl]
        bh = ah * kk
        kd = k[:, sl] * (1.0 + (ah - 1.0) * kav[:, sl])
        heads.append(dict(
            xq=jnp.concatenate([kk * e2[:, sl], r[:, sl] * e1[:, sl]], 0),
            yk=jnp.concatenate([kd * e3[:, sl], bh * e3[:, sl]], 0),
            ke=jnp.concatenate([kd * e4[:, sl], bh * e4[:, sl]], 0),
            v=v[:, sl], etot=etot[:, sl], incl=incl, strict=strict, state=(z, bi, h)))
    return heads


def _rwkv_solve_all(heads, s_sc):
    c = heads[0]['v'].shape[0]
    idx_r = lax.broadcasted_iota(jnp.int32, (c, c), 0)
    idx_c = lax.broadcasted_iota(jnp.int32, (c, c), 1)
    gs = [_dot_nt(hd['xq'], hd['yk']) for hd in heads]
    ss = [s_sc[hd['state']] for hd in heads]
    xs = [_dot_nt(hd['xq'], s) for hd, s in zip(heads, ss)]
    avs = [_dot(jnp.where(hd['strict'], g[:c, :c], 0.0), hd['v']) for hd, g in zip(heads, gs)]
    ts = _tri_inverse_all([jnp.where(hd['strict'], g[:c, c:], 0.0) for hd, g in zip(heads, gs)], idx_r, idx_c)
    sas = [_dot(t, x[:c] + av) for t, x, av in zip(ts, xs, avs)]
    outs = []
    for hd, g, x, sa, s in zip(heads, gs, xs, sas, ss):
        a_r = jnp.concatenate([jnp.where(hd['incl'], g[c:, :c], 0.0), jnp.where(hd['incl'], -g[c:, c:], 0.0)], 1)
        outs.append(x[c:] + _dot(a_r, jnp.concatenate([hd['v'], sa], 0)))
        s_sc[hd['state']] = s * hd['etot'] + _dot_tn(jnp.concatenate([hd['v'], -sa], 0), hd['ke'])
    return outs


def _rwkv_kernel(pa_ref, pap_ref, pan_ref, pb_ref, pbp_ref, pbn_ref, cw_ref, w0_ref, wup_ref, a0_ref, aup_ref,
                 gup_ref, kk_ref, ka_ref, rk_ref, y0_ref, y1_ref, bonus_ref, g_ref, s_sc, *, nl, nc):
    s = pl.program_id(1)

    @pl.when(s == 0)
    def _():
        s_sc[...] = jnp.zeros_like(s_sc)

    ca = (s + nl) % nc
    cb = nc - 1 - s
    cw = cw_ref[...]

    def load(p_ref, pp_ref, pn_ref, cidx, bi):
        first = jnp.logical_or(cidx == 0, cidx == nl)
        lastc = jnp.logical_or(cidx == nl - 1, cidx == nc - 1)
        prev_row = jnp.where(first, 0.0, pp_ref[bi, 7:8, :])
        next_row = jnp.where(lastc, 0.0, pn_ref[bi, 0:1, :])
        return _rwkv_conv(p_ref[bi], prev_row, next_row, cw)

    nb = pa_ref.shape[0]
    heads = []
    xas = []
    for bi in range(nb):
        xa = load(pa_ref, pap_ref, pan_ref, ca, bi)
        xb = load(pb_ref, pbp_ref, pbn_ref, cb, bi)
        xas.append(xa)
        heads += _rwkv_chunk_inputs(xa, 0, bi, w0_ref[0:1], wup_ref[0], a0_ref[0:1], aup_ref[0], kk_ref[...],
                                    ka_ref[...])
        heads += _rwkv_chunk_inputs(xb, 1, bi, w0_ref[1:2], wup_ref[1], a0_ref[1:2], aup_ref[1], kk_ref[...],
                                    ka_ref[...])
    outs = _rwkv_solve_all(heads, s_sc)
    nh = RWKV_HEADS
    for bi in range(nb):
        y0_ref[bi] = jnp.concatenate(outs[2 * nh * bi:2 * nh * bi + nh], 1).astype(BF16)
        y1_ref[bi] = jnp.concatenate(outs[2 * nh * bi + nh:2 * nh * (bi + 1)], 1).astype(BF16)
        xa = xas[bi]
        gd = xa[:, 3 * RWKV_WIDTH + W_LORA + A_LORA:]
        g_ref[bi] = _dot(_sigmoid(gd), gup_ref[...], True).astype(BF16)
        r = xa[:, 0:RWKV_WIDTH]
        k = xa[:, RWKV_WIDTH:2 * RWKV_WIDTH]
        v = xa[:, 2 * RWKV_WIDTH:3 * RWKV_WIDTH]
        rkr = r * k * rk_ref[...]
        bon = []
        for h in range(nh):
            sl = slice(h * RWKV_HEAD_DIM, (h + 1) * RWKV_HEAD_DIM)
            bon.append(jnp.sum(rkr[:, sl], -1, keepdims=True) * v[:, sl])
        bonus_ref[bi] = jnp.concatenate(bon, 1).astype(BF16)


def rwkv_scan(p, n_lat, cw, w0, wup, a0, aup, gup, kkv, kav, rk):
    bsz, lt, _ = p.shape
    nl = n_lat // CHUNK
    nc = lt // CHUNK
    nb8 = lt // 8
    ca = lambda s: (s + nl) % nc
    cb = lambda s: nc - 1 - s
    nb = RWKV_NB if bsz % RWKV_NB == 0 else 1
    pspec = lambda cf: pl.BlockSpec((nb, CHUNK, RWKV_IN), lambda b, s: (b, cf(s), 0))
    prev = lambda cf: pl.BlockSpec((nb, 8, RWKV_IN), lambda b, s: (b, jnp.maximum(cf(s) * (CHUNK // 8) - 1, 0), 0))
    nxt = lambda cf: pl.BlockSpec((nb, 8, RWKV_IN),
                                  lambda b, s: (b, jnp.minimum((cf(s) + 1) * (CHUNK // 8), nb8 - 1), 0))
    full = lambda a: pl.BlockSpec(a.shape, lambda b, s: (0,) * a.ndim)
    ospec = lambda cf: pl.BlockSpec((nb, CHUNK, RWKV_WIDTH), lambda b, s: (b, cf(s), 0))
    oshape = jax.ShapeDtypeStruct((bsz, lt, RWKV_WIDTH), BF16)
    kern = functools.partial(_rwkv_kernel, nl=nl, nc=nc)
    params = (cw, w0, wup, a0, aup, gup, kkv, kav, rk)
    return pl.pallas_call(
        kern,
        grid=(bsz // nb, nc),
        in_specs=[pspec(ca), prev(ca), nxt(ca), pspec(cb), prev(cb), nxt(cb)] + [full(a) for a in params],
        out_specs=[ospec(ca), ospec(cb), ospec(ca), ospec(ca)],
        out_shape=[oshape] * 4,
        scratch_shapes=[pltpu.VMEM((2, nb, RWKV_HEADS, RWKV_HEAD_DIM, RWKV_HEAD_DIM), F32)],
        compiler_params=_cparams(("parallel", "arbitrary")),
        name="rwkv_scan",
    )(p, p, p, p, p, p, *params)


def _s5_kernel(uf_ref, ur_ref, lam_ref, bm_ref, cm_ref, yf_ref, yr_ref, bf_sc, br_sc, h_sc):
    s = pl.program_id(0)

    @pl.when(s == 0)
    def _():
        h_sc[...] = jnp.zeros_like(h_sc)

    rows = bf_sc.shape[0]
    win = S5_WIDTH // 2
    ubs = [jnp.concatenate([u_ref[j] for j in range(S5_TILES)], 1).astype(BF16) for u_ref in (uf_ref, ur_ref)]
    for k in range(2):
        for z, sc in enumerate((bf_sc, br_sc)):
            sc[:, 2 * S5_HALF * k:2 * S5_HALF * (k + 1)] = jnp.dot(
                ubs[z][:, win * k:win * (k + 1)], bm_ref[z, k], preferred_element_type=F32)
    ntile = rows // 8
    lo = lax.broadcasted_iota(jnp.int32, (8, 1), 0) < 4

    for k in range(2):
        cre = slice(2 * S5_HALF * k, 2 * S5_HALF * k + S5_HALF)
        cim = slice(2 * S5_HALF * k + S5_HALF, 2 * S5_HALF * (k + 1))
        lr = lam_ref[0, :, S5_HALF * k:S5_HALF * (k + 1)]
        li = lam_ref[1, :, S5_HALF * k:S5_HALF * (k + 1)]
        hr, hi = h_sc[2 * k], h_sc[2 * k + 1]
        for jj in range(ntile):
            sf = slice(jj * 8, jj * 8 + 8)
            sr = slice((ntile - 1 - jj) * 8, (ntile - jj) * 8)
            tfr, tfi = bf_sc[sf, cre], bf_sc[sf, cim]
            trr, tri = br_sc[sr, cre], br_sc[sr, cim]
            xar, xai = jnp.where(lo, tfr, trr), jnp.where(lo, tfi, tri)
            xbr = pltpu.roll(jnp.where(lo, trr, tfr), 4, 0)
            xbi = pltpu.roll(jnp.where(lo, tri, tfi), 4, 0)
            ar, ai = lr * hr - li * hi + xar, lr * hi + li * hr + xai
            hr, hi = lr * ar - li * ai + xbr, lr * ai + li * ar + xbi
            rbr, rbi = pltpu.roll(hr, 4, 0), pltpu.roll(hi, 4, 0)
            bf_sc[sf, cre] = jnp.where(lo, ar, rbr)
            bf_sc[sf, cim] = jnp.where(lo, ai, rbi)
            br_sc[sr, cre] = jnp.where(lo, rbr, ar)
            br_sc[sr, cim] = jnp.where(lo, rbi, ai)
        h_sc[2 * k] = hr
        h_sc[2 * k + 1] = hi
        for z, (sc, y_ref) in enumerate(((bf_sc, yf_ref), (br_sc, yr_ref))):
            y = jnp.dot(sc[:, 2 * S5_HALF * k:2 * S5_HALF * (k + 1)].astype(BF16), cm_ref[z, k],
                        preferred_element_type=F32)
            for j in range(win // LANES):
                y_ref[k * (win // LANES) + j] = y[:, j * LANES:(j + 1) * LANES]


def s5_scan(u, n_lat, bsz, lam, bmat, cmat):
    assert bsz == 4
    rows_total = u.shape[1]
    lt = rows_total // bsz
    nl = n_lat // CHUNK
    nc = lt // CHUNK
    rb = CHUNK * bsz
    cf = lambda s: (0, (s + nl) % nc, 0)
    cr = lambda s: (0, nc - 1 - s, 0)
    full = lambda a: pl.BlockSpec(a.shape, lambda s: (0,) * a.ndim)
    oshape = jax.ShapeDtypeStruct(u.shape, F32)
    blk = (S5_TILES, rb, LANES)
    return pl.pallas_call(
        _s5_kernel,
        grid=(nc,),
        in_specs=[pl.BlockSpec(blk, cf), pl.BlockSpec(blk, cr), full(lam), full(bmat), full(cmat)],
        out_specs=[pl.BlockSpec(blk, cf), pl.BlockSpec(blk, cr)],
        out_shape=[oshape, oshape],
        scratch_shapes=[pltpu.VMEM((rb, 2 * S5_N), F32), pltpu.VMEM((rb, 2 * S5_N), F32),
                        pltpu.VMEM((4, 8, S5_HALF), F32)],
        compiler_params=_cparams(("arbitrary",)),
        name="s5_scan",
    )(u, u, lam, bmat, cmat)


def _s5_params(lam_re, lam_im, log_dt, b_re, b_im, c_re, c_im):
    dt = jnp.exp(log_dt.astype(F32))[..., None]
    mag = jnp.exp(lam_re * dt)
    lbr = mag * jnp.cos(lam_im * dt)
    lbi = mag * jnp.sin(lam_im * dt)
    den = lam_re * lam_re + lam_im * lam_im
    nr, ni = lbr - 1.0, lbi
    fr = (nr * lam_re + ni * lam_im) / den
    fi = (ni * lam_re - nr * lam_im) / den
    bbr = fr[..., None] * b_re - fi[..., None] * b_im
    bbi = fr[..., None] * b_im + fi[..., None] * b_re
    gh = S5_GROUPS // 2
    eye = jnp.eye(gh, dtype=F32)

    def bd_in(w):
        return jnp.einsum('gpc,gh->gchp', w, eye).reshape(S5_WIDTH // 2, S5_HALF)

    def bd_out(w):
        return jnp.einsum('gcp,gh->gphc', w, eye).reshape(S5_HALF, S5_WIDTH // 2)

    halves = [slice(0, gh), slice(gh, S5_GROUPS)]
    bmat = jnp.stack([jnp.stack([jnp.concatenate([bd_in(bbr[z, hs]), bd_in(bbi[z, hs])], 1) for hs in halves])
                      for z in range(2)])
    cmat = jnp.stack([jnp.stack([jnp.concatenate([bd_out(c_re[z, hs]), -bd_out(c_im[z, hs])], 0) for hs in halves])
                      for z in range(2)])
    rows = lambda t: jnp.concatenate([jnp.tile(t[0].reshape(1, S5_N), (4, 1)),
                                      jnp.tile(t[1].reshape(1, S5_N), (4, 1))], 0)
    lam = jnp.stack([rows(lbr), rows(lbi)])
    return lam, bmat.astype(BF16), cmat.astype(BF16)


def _gelu(x):
    return 0.5 * x * (1.0 + jnp.tanh(math.sqrt(2.0 / math.pi) * (x + 0.044715 * x * x * x)))


def _merge_kernel(y0_ref, y1_ref, bonus_ref, g_ref, s0_ref, s1_ref, u_ref, x_ref, c_ref, mod_ref, lnw_ref, lnb_ref,
                  d_ref, gluw_ref, glub_ref, wout_ref, o_ref, *, nlb):
    is_ctx = pl.program_id(0) >= nlb
    bsz, rows, _ = x_ref.shape
    n = RWKV_HEAD_DIM
    for b in range(bsz):
        y = y0_ref[b].astype(F32) + y1_ref[b].astype(F32)
        parts = []
        for h in range(RWKV_HEADS):
            yh = y[:, h * n:(h + 1) * n]
            mu = jnp.mean(yh, -1, keepdims=True)
            dlt = yh - mu
            var = jnp.mean(dlt * dlt, -1, keepdims=True)
            parts.append(dlt * lax.rsqrt(var + GN_EPS))
        yn = jnp.concatenate(parts, 1) * lnw_ref[...] + lnb_ref[...]
        rw = (yn + bonus_ref[b].astype(F32)) * g_ref[b].astype(F32)
        tm = pl.ds(b, rows, stride=bsz)
        seq = lambda ref: jnp.concatenate([ref[j, tm, :] for j in range(S5_TILES)], 1)
        ys = seq(s0_ref) + seq(s1_ref) + d_ref[...] * seq(u_ref)
        zz = _gelu(ys)
        gate = _sigmoid(_dot(zz, gluw_ref[...]) + glub_ref[...])
        cat = jnp.concatenate([rw, zz * gate], 1).astype(BF16)
        hb, (gate_res,) = _stream_rows(x_ref, c_ref, mod_ref, b, is_ctx, (2,))
        o_ref[b] = hb + gate_res * jnp.dot(cat, wout_ref[...], preferred_element_type=F32)


def merge(y0, y1, bonus, g, s0, s1, u, x, ctx, mod, lnw, lnb, dvec, gluw, glub, wout):
    bsz, n_lat, d = x.shape
    lt = n_lat + ctx.shape[1]
    nlb, sspecs = _stream_specs(x, ctx)
    rspec = pl.BlockSpec((bsz, SEQ_BLK, RWKV_WIDTH), lambda i: (0, i, 0))
    tspec = pl.BlockSpec((S5_TILES, bsz * SEQ_BLK, LANES), lambda i: (0, i, 0))
    full = lambda a: pl.BlockSpec(a.shape, lambda i: (0,) * a.ndim)
    params = (mod, lnw, lnb, dvec, gluw, glub, wout)
    return pl.pallas_call(
        functools.partial(_merge_kernel, nlb=nlb),
        grid=(lt // SEQ_BLK,),
        in_specs=[rspec, rspec, rspec, rspec, tspec, tspec, tspec] + sspecs + [full(a) for a in params],
        out_specs=pl.BlockSpec((bsz, SEQ_BLK, d), lambda i: (0, i, 0)),
        out_shape=jax.ShapeDtypeStruct((bsz, lt, d), F32),
        compiler_params=_cparams(("parallel",)),
        name="merge",
    )(y0, y1, bonus, g, s0, s1, u, x, ctx, *params)


def hybrid_layer(x, ctx, g1, mod, w_in, w_out, conv_w, w0, w_up, a0, a_up, g_up, k_k, k_a, r_k, ln_w, ln_b,
                 lam_re, lam_im, log_dt, b_re, b_im, c_re, c_im, dvec, glu_w, glu_b):
    bsz, n_lat, _ = x.shape
    p, u = inproj2(x, ctx, g1, mod, w_in.astype(BF16))
    y0, y1, bonus, g = rwkv_scan(p, n_lat, conv_w, w0, w_up, a0, a_up, g_up, k_k[None], k_a[None],
                                 r_k.reshape(1, RWKV_WIDTH))
    lam, bmat, cmat = _s5_params(lam_re, lam_im, log_dt, b_re, b_im, c_re, c_im)
    s0, s1 = s5_scan(u, n_lat, bsz, lam, bmat, cmat)
    gluw = jnp.einsum('gce,gh->gche', glu_w, jnp.eye(S5_GROUPS, dtype=F32)).reshape(S5_WIDTH, S5_WIDTH)
    return merge(y0, y1, bonus, g, s0, s1, u, x, ctx, mod, ln_w[None], ln_b[None], dvec[None],
                 gluw.astype(BF16), glu_b[None], w_out.astype(BF16))


def kernel(x, c, ctx, c_ctx, mod_w, mod_b, norm1_g, norm2_g, final_g, hy_w_in, hy_w_out, rk_conv, rk_w0, rk_w_up, rk_a0, rk_a_up, rk_g_up, rk_k_k, rk_k_a, rk_r_k, rk_ln_w, rk_ln_b, s5_lam_re, s5_lam_im, s5_log_dt, s5_b_re, s5_b_im, s5_c_re, s5_c_im, s5_d, s5_glu_w, s5_glu_b, mla_w_in, mla_q_norm, mla_q_up, mla_kv_norm, mla_kv_up, mla_w_out, router_w, router_b, ex_gate, ex_up, ex_down, sh_gate, sh_up, sh_down):
    bsz, n_lat, d = x.shape
    n_ctx = ctx.shape[1]
    depth = mod_w.shape[0]
    assert n_ctx == ROW_BLK and depth == 2 and bsz <= 7
    cc = jnp.concatenate([c, c_ctx[None], jnp.zeros((8 - bsz - 1, d), F32)], 0)
    rb = router_b[:, None].astype(F32)
    rw_t = router_w.T
    fg = final_g[None]
    for layer in range(depth):
        last = layer == depth - 1
        i = layer // 2
        mv = adaln(cc, mod_w[layer], mod_b[layer][None])
        m_l = mv[:bsz].reshape(bsz, 1, 6, d)
        m_c = jnp.broadcast_to(mv[bsz].reshape(1, 1, 6, d), (bsz, 1, 6, d))
        mod = jnp.concatenate([m_l, m_c], 1)
        g1 = norm1_g[layer][None]
        g2 = norm2_g[layer][None]
        if layer % 2 == 0:
            h = hybrid_layer(x, ctx, g1, mod, hy_w_in[i], hy_w_out[i], rk_conv[i], rk_w0[i], rk_w_up[i], rk_a0[i],
                             rk_a_up[i], rk_g_up[i], rk_k_k[i], rk_k_a[i], rk_r_k[i], rk_ln_w[i], rk_ln_b[i],
                             s5_lam_re[i], s5_lam_im[i], s5_log_dt[i], s5_b_re[i], s5_b_im[i], s5_c_re[i],
                             s5_c_im[i], s5_d[i], s5_glu_w[i], s5_glu_b[i])
        else:
            h = mla_layer(h, n_ctx, g1, mod, mla_w_in[i], mla_q_norm[i], mla_q_up[i], mla_kv_norm[i],
                          mla_kv_up[i], mla_w_out[i])
        bf = lambda t: t[layer].astype(BF16)
        h = moe(h, n_lat, g2, mod, rw_t, rb, bf(ex_gate), bf(ex_up), bf(ex_down), bf(sh_gate), bf(sh_up),
                bf(sh_down), fg, last)
    return h
```

```python
import functools
import math

import jax
import jax.numpy as jnp
from jax import lax
from jax.experimental import pallas as pl
from jax.experimental.pallas import tpu as pltpu

F32 = jnp.float32
BF16 = jnp.bfloat16
HI = lax.Precision.HIGHEST

D_MODEL = 1024
GRID_W = 64
NORM_EPS = 1e-6

RWKV_HEADS = 8
RWKV_HEAD_DIM = 64
RWKV_WIDTH = 512
W_LORA = 64
A_LORA = 64
G_LORA = 128
RWKV_IN = 3 * RWKV_WIDTH + W_LORA + A_LORA + G_LORA
DECAY_SCALE = math.exp(-0.5)
GN_EPS = 64e-5

S5_GROUP = 16
S5_GROUPS = 32
S5_WIDTH = 512
S5_STATE = 64
S5_N = S5_GROUPS * S5_STATE
S5_HALF = S5_N // 2
LANES = 128
S5_TILES = S5_WIDTH // LANES
HYB_IN = RWKV_IN + S5_WIDTH

MLA_HEADS = 16
Q_LORA = 256
KV_LORA = 128
QK_NOPE = 64
QK_ROPE = 32
V_DIM = 64
MLA_IN = Q_LORA + KV_LORA + QK_ROPE
MLA_SCALE = (QK_NOPE + QK_ROPE) ** -0.5
ROPE_AXIS_DIMS = QK_ROPE // 2
ROPE_BASE = 10000.0
HEAD_PAD = 128

N_EXPERTS = 16
N_GROUPS = 4
EXPERTS_PER_GROUP = 4
D_EXPERT = 256

ROW_BLK = 256
SEQ_BLK = 128
MOE_BLKS = (1024, 768, 512, 256)
MOE_EXPERTS_PER_STEP = 4
ATT_TQS = (1024, 512, 256)
ATT_TKS = (1408, 768, 256)
ATT_SUB = 256
CHUNK = 64
RWKV_NB = 4
VMEM_LIMIT = 56 * 1024 * 1024


def _cparams(sem):
    return pltpu.CompilerParams(dimension_semantics=sem, vmem_limit_bytes=VMEM_LIMIT)


def _norm_mod(x, g, shift, scale):
    y = x * lax.rsqrt(jnp.mean(x * x, -1, keepdims=True) + NORM_EPS) * g
    return y * (1.0 + scale) + shift


def _sigmoid(x):
    return 1.0 / (1.0 + jnp.exp(-x))


def _dot(a, b, hi=False):
    if hi:
        return jnp.dot(a, b, precision=HI, preferred_element_type=F32)
    return jnp.dot(a.astype(BF16), b.astype(BF16), preferred_element_type=F32)


def _split3(x):
    hi = x.astype(BF16)
    r1 = x - hi.astype(F32)
    mid = r1.astype(BF16)
    return hi, mid, (r1 - mid.astype(F32)).astype(BF16)


def _dot_nt(a, b):
    return lax.dot_general(a.astype(BF16), b.astype(BF16), (((1,), (1,)), ((), ())), preferred_element_type=F32)


def _dot_tn(a, b):
    return jnp.dot(a.T.astype(BF16), b.astype(BF16), preferred_element_type=F32)


def _adaln_kernel(s_ref, w_ref, b_ref, o_ref):
    s = s_ref[...]
    s = s * _sigmoid(s)
    o_ref[...] = jnp.dot(s, w_ref[...], precision=HI, preferred_element_type=F32) + b_ref[...]


def adaln(cc, w, b):
    n = w.shape[1]
    tn = 512
    return pl.pallas_call(
        _adaln_kernel,
        grid=(n // tn,),
        in_specs=[pl.BlockSpec((8, D_MODEL), lambda j: (0, 0)),
                  pl.BlockSpec((D_MODEL, tn), lambda j: (0, j)),
                  pl.BlockSpec((1, tn), lambda j: (0, j))],
        out_specs=pl.BlockSpec((8, tn), lambda j: (0, j)),
        out_shape=jax.ShapeDtypeStruct((8, n), F32),
        compiler_params=_cparams(("parallel",)),
        name="adaln",
    )(cc, w, b)


def _mod_spec(n_lat, blk=ROW_BLK):
    return pl.BlockSpec((1, 1, 6, D_MODEL), lambda b, i, *_: (b, i // (n_lat // blk), 0, 0))


def _outproj_kernel(a_ref, w_ref, h_ref, mod_ref, o_ref):
    m = mod_ref[0, 0]
    o = jnp.dot(a_ref[0].astype(BF16), w_ref[...], preferred_element_type=F32)
    o_ref[0] = h_ref[0] + m[2:3] * o


def outproj(a, w, h, n_lat, mod):
    bsz, la, k = a.shape
    d = h.shape[2]
    return pl.pallas_call(
        _outproj_kernel,
        grid=(bsz, la // ROW_BLK),
        in_specs=[pl.BlockSpec((1, ROW_BLK, k), lambda b, i: (b, i, 0)),
                  pl.BlockSpec((k, d), lambda b, i: (0, 0)),
                  pl.BlockSpec((1, ROW_BLK, d), lambda b, i: (b, i, 0)),
                  _mod_spec(n_lat)],
        out_specs=pl.BlockSpec((1, ROW_BLK, d), lambda b, i: (b, i, 0)),
        out_shape=jax.ShapeDtypeStruct((bsz, la, d), F32),
        compiler_params=_cparams(("parallel", "parallel")),
        name="outproj",
    )(a, w, h, mod)


def _route(scores, rb):
    t = scores.shape[1]
    biased = scores + rb
    col = [biased[e:e + 1, :] for e in range(N_EXPERTS)]
    sc = [scores[e:e + 1, :] for e in range(N_EXPERTS)]
    gscore = []
    for gi in range(N_GROUPS):
        a, b, c, d = col[4 * gi:4 * gi + 4]
        hi1, lo1 = jnp.maximum(a, b), jnp.minimum(a, b)
        hi2, lo2 = jnp.maximum(c, d), jnp.minimum(c, d)
        gscore.append(jnp.maximum(hi1, hi2) + jnp.maximum(jnp.minimum(hi1, hi2), jnp.maximum(lo1, lo2)))
    gsel = []
    taken = None
    for gi in range(N_GROUPS):
        best = None
        for gj in range(gi + 1, N_GROUPS):
            best = gscore[gj] if best is None else jnp.maximum(best, gscore[gj])
        s = (gscore[gi] >= best) if best is not None else jnp.full((1, t), True)
        if taken is not None:
            s = jnp.logical_and(s, jnp.logical_not(taken))
        taken = s if taken is None else jnp.logical_or(taken, s)
        gsel.append(s)
    masks = []
    for gi in range(N_GROUPS):
        v = col[4 * gi:4 * gi + 4]
        for j in range(4):
            rank = jnp.zeros((1, t), F32)
            for i in range(4):
                if i == j:
                    continue
                ahead = (v[i] >= v[j]) if i < j else (v[i] > v[j])
                rank = rank + ahead.astype(F32)
            masks.append(jnp.logical_and(gsel[gi], rank < 2.0))
    wsel = [jnp.where(masks[e], sc[e], 0.0) for e in range(N_EXPERTS)]
    denom = wsel[0]
    for e in range(1, N_EXPERTS):
        denom = denom + wsel[e]
    return [w / denom for w in wsel]


def _moe_kernel(h_ref, g_ref, mod_ref, rw_ref, rb_ref, wg_ref, wu_ref, wd_ref, sg_ref, su_ref, sd_ref, fg_ref, o_ref,
                f_sc, comb_sc, acc_sc, *, final_norm, n_lat, has_ctx):
    i = pl.program_id(1)
    e = pl.program_id(2)
    n_e = pl.num_programs(2)
    tm = f_sc.shape[0]

    def mod_row(k):
        lat = mod_ref[0, 0, k:k + 1, :]
        if not has_ctx:
            return lat
        row = i * tm + lax.broadcasted_iota(jnp.int32, (tm, 1), 0)
        return jnp.where(row >= n_lat, mod_ref[0, 1, k:k + 1, :], lat)

    @pl.when(e == 0)
    def _():
        f = _norm_mod(h_ref[0], g_ref[...], mod_row(3), mod_row(4))
        f16 = f.astype(BF16)
        f_sc[...] = f16
        gt = jnp.dot(f16, sg_ref[...], preferred_element_type=F32)
        up = jnp.dot(f16, su_ref[...], preferred_element_type=F32)
        act = gt * _sigmoid(gt) * up
        acc_sc[...] = jnp.dot(act.astype(BF16), sd_ref[...], preferred_element_type=F32)
        logits = lax.dot_general(rw_ref[...], f, (((1,), (1,)), ((), ())), precision=HI,
                                 preferred_element_type=F32)
        cw = _route(_sigmoid(logits), rb_ref[...])
        sub = lax.broadcasted_iota(jnp.int32, (128, tm), 0)
        comb = jnp.zeros((128, tm), F32)
        for ei in range(N_EXPERTS):
            comb = jnp.where(sub == ei, cw[ei], comb)
        comb_sc[...] = comb.T

    fb = f_sc[...]
    eps = wg_ref.shape[0]
    lane = lax.broadcasted_iota(jnp.int32, (tm, 128), 1)
    comb = comb_sc[...]
    gts = [jnp.dot(fb, wg_ref[j], preferred_element_type=F32) for j in range(eps)]
    ups = [jnp.dot(fb, wu_ref[j], preferred_element_type=F32) for j in range(eps)]
    tot = None
    for j in range(eps):
        cw = jnp.sum(jnp.where(lane == e * eps + j, comb, 0.0), axis=1, keepdims=True)
        act = gts[j] * _sigmoid(gts[j]) * ups[j] * cw
        dn = jnp.dot(act.astype(BF16), wd_ref[j], preferred_element_type=F32)
        tot = dn if tot is None else tot + dn
    acc_sc[...] += tot

    @pl.when(e == n_e - 1)
    def _():
        y = h_ref[0] + mod_row(5) * acc_sc[...]
        if final_norm:
            y = y * lax.rsqrt(jnp.mean(y * y, -1, keepdims=True) + NORM_EPS) * fg_ref[...]
        o_ref[0] = y


def moe(h, n_lat, g, mod, rw, rb, wg, wu, wd, sg, su, sd, fg, final_norm):
    bsz, lo, d = h.shape
    eps = MOE_EXPERTS_PER_STEP
    tm = next(t for t in MOE_BLKS if lo % t == 0)
    kern = functools.partial(_moe_kernel, final_norm=final_norm, n_lat=n_lat, has_ctx=lo > n_lat)
    return pl.pallas_call(
        kern,
        grid=(bsz, lo // tm, N_EXPERTS // eps),
        in_specs=[pl.BlockSpec((1, tm, d), lambda b, i, e: (b, i, 0)),
                  pl.BlockSpec((1, d), lambda b, i, e: (0, 0)),
                  pl.BlockSpec((1, 2, 6, d), lambda b, i, e: (b, 0, 0, 0)),
                  pl.BlockSpec((N_EXPERTS, d), lambda b, i, e: (0, 0)),
                  pl.BlockSpec((N_EXPERTS, 1), lambda b, i, e: (0, 0)),
                  pl.BlockSpec((eps, d, D_EXPERT), lambda b, i, e: (e, 0, 0)),
                  pl.BlockSpec((eps, d, D_EXPERT), lambda b, i, e: (e, 0, 0)),
                  pl.BlockSpec((eps, D_EXPERT, d), lambda b, i, e: (e, 0, 0)),
                  pl.BlockSpec((d, D_EXPERT), lambda b, i, e: (0, 0)),
                  pl.BlockSpec((d, D_EXPERT), lambda b, i, e: (0, 0)),
                  pl.BlockSpec((D_EXPERT, d), lambda b, i, e: (0, 0)),
                  pl.BlockSpec((1, d), lambda b, i, e: (0, 0))],
        out_specs=pl.BlockSpec((1, tm, d), lambda b, i, e: (b, i, 0)),
        out_shape=jax.ShapeDtypeStruct((bsz, lo, d), F32),
        scratch_shapes=[pltpu.VMEM((tm, d), BF16),
                        pltpu.VMEM((tm, 128), F32),
                        pltpu.VMEM((tm, d), F32)],
        compiler_params=_cparams(("parallel", "parallel", "arbitrary")),
        name="moe",
    )(h, g, mod, rw, rb, wg, wu, wd, sg, su, sd, fg)


def _mla_proj_kernel(h_ref, g_ref, mod_ref, win_ref, qn_ref, qup_ref, qsw_ref, kvn_ref, kup_ref, vup_ref, epl_ref,
                     esw_ref, one_ref, ct_ref, st_ref, q_ref, k_ref, v_ref):
    m = mod_ref[0, 0]
    a = _norm_mod(h_ref[0], g_ref[...], m[0:1], m[1:2])
    p = jnp.dot(a.astype(BF16), win_ref[...], preferred_element_type=F32)
    qd = p[:, :Q_LORA]
    kvd = p[:, Q_LORA:Q_LORA + KV_LORA]
    kpe = p[:, Q_LORA + KV_LORA:]
    qn = qd * lax.rsqrt(jnp.mean(qd * qd, -1, keepdims=True) + NORM_EPS) * qn_ref[...]
    kvn = kvd * lax.rsqrt(jnp.mean(kvd * kvd, -1, keepdims=True) + NORM_EPS) * kvn_ref[...]
    kvb = kvn.astype(BF16)
    ct = jnp.concatenate([ct_ref[...]] * MLA_HEADS, axis=1)
    st = jnp.concatenate([st_ref[...]] * MLA_HEADS, axis=1)
    qb = qn.astype(BF16)
    q = jnp.dot(qb, qup_ref[...], preferred_element_type=F32)
    q_sw = jnp.dot(qb, qsw_ref[...], preferred_element_type=F32)
    q_ref[0] = ((q * ct + q_sw * st) * MLA_SCALE).astype(BF16)
    kpb = kpe.astype(BF16)
    k = (jnp.dot(kvb, kup_ref[...], preferred_element_type=F32)
         + jnp.dot(kpb, epl_ref[...], preferred_element_type=F32))
    k_sw = jnp.dot(kpb, esw_ref[...], preferred_element_type=F32)
    k_ref[0] = (k * ct + k_sw * st).astype(BF16)
    v = jnp.dot(kvb, vup_ref[...], preferred_element_type=F32) + one_ref[...]
    v_ref[0] = v.astype(BF16)


def mla_proj(h, n_lat, g, mod, win, qn, qup, qsw, kvn, kup, vup, epl, esw, one, ct, st):
    bsz, lt, d = h.shape
    hp = MLA_HEADS * HEAD_PAD
    full = lambda shape: pl.BlockSpec(shape, lambda b, i: (0,) * len(shape))
    tab = pl.BlockSpec((ROW_BLK, HEAD_PAD), lambda b, i: (i, 0))
    out = pl.BlockSpec((1, ROW_BLK, hp), lambda b, i: (b, i, 0))
    return pl.pallas_call(
        _mla_proj_kernel,
        grid=(bsz, lt // ROW_BLK),
        in_specs=[pl.BlockSpec((1, ROW_BLK, d), lambda b, i: (b, i, 0)),
                  full((1, d)), _mod_spec(n_lat), full((d, MLA_IN)), full((1, Q_LORA)), full((Q_LORA, hp)),
                  full((Q_LORA, hp)), full((1, KV_LORA)), full((KV_LORA, hp)), full((KV_LORA, hp)),
                  full((QK_ROPE, hp)), full((QK_ROPE, hp)), full((1, hp)), tab, tab],
        out_specs=[out, out, out],
        out_shape=[jax.ShapeDtypeStruct((bsz, lt, hp), BF16)] * 3,
        compiler_params=_cparams(("parallel", "parallel")),
        name="mla_proj",
    )(h, g, mod, win, qn, qup, qsw, kvn, kup, vup, epl, esw, one, ct, st)


def _attn_kernel(q_ref, k_ref, v_ref, o_ref, *, tk, nk):
    tq = q_ref.shape[1]
    nsub = tq // ATT_SUB
    qs = [q_ref[0, i * ATT_SUB:(i + 1) * ATT_SUB, :] for i in range(nsub)]

    def scores(j):
        kc = k_ref[0, j * tk:(j + 1) * tk, :]
        return [lax.dot_general(q, kc, (((1,), (1,)), ((), ())), preferred_element_type=F32) for q in qs]

    ms = [jnp.full((ATT_SUB, 1), -1e30, F32) for _ in range(nsub)]
    accs = [jnp.zeros((ATT_SUB, HEAD_PAD), F32) for _ in range(nsub)]
    ss = scores(0)
    for j in range(nk):
        ss_next = scores(j + 1) if j + 1 < nk else None
        vc = v_ref[0, j * tk:(j + 1) * tk, :]
        for i in range(nsub):
            m_new = jnp.maximum(ms[i], jnp.max(ss[i], axis=1, keepdims=True))
            alpha = jnp.exp(ms[i] - m_new)
            p = jnp.exp(ss[i] - m_new).astype(BF16)
            accs[i] = accs[i] * alpha + jnp.dot(p, vc, preferred_element_type=F32)
            ms[i] = m_new
        ss = ss_next
    for i, acc in enumerate(accs):
        o_ref[0, i * ATT_SUB:(i + 1) * ATT_SUB, :] = (acc / acc[:, V_DIM:V_DIM + 1]).astype(BF16)


def attention(q, k, v, lq):
    bsz, lt, hp = q.shape
    tq = next(t for t in ATT_TQS if lq % t == 0)
    tk = next(t for t in ATT_TKS if lt % t == 0)
    kern = functools.partial(_attn_kernel, tk=tk, nk=lt // tk)
    return pl.pallas_call(
        kern,
        grid=(bsz, MLA_HEADS, lq // tq),
        in_specs=[pl.BlockSpec((1, tq, HEAD_PAD), lambda b, h, i: (b, i, h)),
                  pl.BlockSpec((1, lt, HEAD_PAD), lambda b, h, i: (b, 0, h)),
                  pl.BlockSpec((1, lt, HEAD_PAD), lambda b, h, i: (b, 0, h))],
        out_specs=pl.BlockSpec((1, tq, HEAD_PAD), lambda b, h, i: (b, i, h)),
        out_shape=jax.ShapeDtypeStruct((bsz, lq, hp), BF16),
        compiler_params=_cparams(("parallel", "parallel", "parallel")),
        name="attention",
    )(q, k, v)


def _rope_tables(n_lat, n_ctx):
    rows = n_lat // GRID_W
    row = jnp.repeat(jnp.arange(rows, dtype=F32), GRID_W)
    col = jnp.tile(jnp.arange(GRID_W, dtype=F32), rows)
    inv_freq = ROPE_BASE ** (-jnp.arange(0, ROPE_AXIS_DIMS, 2, dtype=F32) / ROPE_AXIS_DIMS)
    ang = jnp.concatenate([row[:, None] * inv_freq, col[:, None] * inv_freq], -1)
    cos = jnp.concatenate([jnp.cos(ang), jnp.ones((n_ctx, ROPE_AXIS_DIMS), F32)], 0)
    sin = jnp.concatenate([jnp.sin(ang), jnp.zeros((n_ctx, ROPE_AXIS_DIMS), F32)], 0)
    lt = n_lat + n_ctx
    one = jnp.ones((lt, QK_NOPE), F32)
    z32 = jnp.zeros((lt, HEAD_PAD - QK_NOPE - QK_ROPE), F32)
    z64 = jnp.zeros((lt, QK_NOPE), F32)
    ct = jnp.concatenate([one, cos, cos, z32], 1)
    st = jnp.concatenate([z64, -sin, sin, z32], 1)
    return ct, st


def _pad_heads(w, width, offset=0):
    k = w.shape[0]
    w = w.reshape(k, MLA_HEADS, width)
    w = jnp.pad(w, ((0, 0), (0, 0), (offset, HEAD_PAD - width - offset)))
    return w.reshape(k, MLA_HEADS * HEAD_PAD)


def mla_layer(h, n_ctx, g, mod, w_in, q_norm, q_up, kv_norm, kv_up, w_out):
    bsz, lt, d = h.shape
    n_lat = lt - n_ctx
    ct, st = _rope_tables(n_lat, n_ctx)
    half = ROPE_AXIS_DIMS
    swap = lambda t: jnp.concatenate([t[..., half:], t[..., :half]], -1)
    qup = _pad_heads(q_up, QK_NOPE + QK_ROPE).astype(BF16)
    q_pe = q_up.reshape(Q_LORA, MLA_HEADS, QK_NOPE + QK_ROPE)[:, :, QK_NOPE:]
    qsw = _pad_heads(swap(q_pe).reshape(Q_LORA, -1), QK_ROPE, QK_NOPE).astype(BF16)
    kvu = kv_up.reshape(KV_LORA, MLA_HEADS, QK_NOPE + V_DIM)
    kup = _pad_heads(kvu[:, :, :QK_NOPE].reshape(KV_LORA, -1), QK_NOPE).astype(BF16)
    vup = _pad_heads(kvu[:, :, QK_NOPE:].reshape(KV_LORA, -1), V_DIM).astype(BF16)
    eye = jnp.eye(QK_ROPE, dtype=F32)
    epl = _pad_heads(jnp.tile(eye, (1, MLA_HEADS)), QK_ROPE, QK_NOPE).astype(BF16)
    esw = _pad_heads(jnp.tile(swap(eye), (1, MLA_HEADS)), QK_ROPE, QK_NOPE).astype(BF16)
    one = _pad_heads(jnp.ones((1, MLA_HEADS), F32), 1, V_DIM)
    q, k, v = mla_proj(h, n_lat, g, mod, w_in.astype(BF16), q_norm[None], qup, qsw, kv_norm[None], kup, vup, epl,
                       esw, one, ct, st)
    o = attention(q, k, v, n_lat)
    wo = w_out.reshape(MLA_HEADS, V_DIM, d)
    wo = jnp.pad(wo, ((0, 0), (0, HEAD_PAD - V_DIM), (0, 0))).reshape(MLA_HEADS * HEAD_PAD, d).astype(BF16)
    return outproj(o, wo, h, n_lat, mod)


def _stream_rows(x_ref, c_ref, mod_ref, b, is_ctx, ks):
    rows, d = x_ref.shape[1:]
    lat_rows = jnp.where(is_ctx, 0, rows)
    hb = jnp.where(lax.broadcasted_iota(jnp.int32, (rows, d), 0) < lat_rows, x_ref[b], c_ref[b])
    lat1 = lax.broadcasted_iota(jnp.int32, (1, d), 0) < lat_rows
    ms = [jnp.where(lat1, mod_ref[b, 0, k:k + 1, :], mod_ref[b, 1, k:k + 1, :]) for k in ks]
    return hb, ms


def _inproj2_kernel(x_ref, c_ref, g_ref, mod_ref, w_ref, p_ref, u_ref, *, nlb):
    is_ctx = pl.program_id(0) >= nlb
    bsz, rows, _ = x_ref.shape
    for b in range(bsz):
        hb, (shift, scale) = _stream_rows(x_ref, c_ref, mod_ref, b, is_ctx, (0, 1))
        a = _norm_mod(hb, g_ref[...], shift, scale)
        o = jnp.dot(a.astype(BF16), w_ref[...], preferred_element_type=F32)
        p_ref[b] = o[:, :RWKV_IN]
        for j in range(S5_TILES):
            u_ref[j, pl.ds(b, rows, stride=bsz), :] = o[:, RWKV_IN + j * LANES:RWKV_IN + (j + 1) * LANES]


def _stream_specs(x, ctx):
    bsz, n_lat, d = x.shape
    nlb = n_lat // SEQ_BLK
    return nlb, [pl.BlockSpec((bsz, SEQ_BLK, d), lambda i: (0, jnp.minimum(i, nlb - 1), 0)),
                 pl.BlockSpec((bsz, SEQ_BLK, d), lambda i: (0, jnp.maximum(i - nlb, 0), 0))]


def inproj2(x, ctx, g, mod, w):
    bsz, n_lat, d = x.shape
    lt = n_lat + ctx.shape[1]
    nlb, sspecs = _stream_specs(x, ctx)
    full = lambda a: pl.BlockSpec(a.shape, lambda i: (0,) * a.ndim)
    return pl.pallas_call(
        functools.partial(_inproj2_kernel, nlb=nlb),
        grid=(lt // SEQ_BLK,),
        in_specs=sspecs + [full(g), full(mod), full(w)],
        out_specs=[pl.BlockSpec((bsz, SEQ_BLK, RWKV_IN), lambda i: (0, i, 0)),
                   pl.BlockSpec((S5_TILES, bsz * SEQ_BLK, LANES), lambda i: (0, i, 0))],
        out_shape=[jax.ShapeDtypeStruct((bsz, lt, RWKV_IN), F32),
                   jax.ShapeDtypeStruct((S5_TILES, lt * bsz, LANES), F32)],
        compiler_params=_cparams(("parallel",)),
        name="inproj2",
    )(x, ctx, g, mod, w)


def _tri_inverse_all(mats, idx_r, idx_c):
    c = mats[0].shape[0]
    eye = (idx_r == idx_c).astype(F32)
    blk8 = (idx_r >> 3) == (idx_c >> 3)
    ns = [jnp.where(blk8, -a, 0.0) for a in mats]
    ts = [eye + n for n in ns]
    n2 = [_dot(n, n) for n in ns]
    ts = [t + _dot(m, t) for m, t in zip(n2, ts)]
    n4 = [_dot(m, m) for m in n2]
    ts = [t + _dot(m, t) for m, t in zip(n4, ts)]
    sh = 3
    while (1 << sh) < c:
        off = jnp.logical_and((idx_r >> (sh + 1)) == (idx_c >> (sh + 1)), (idx_r >> sh) != (idx_c >> sh))
        ms = [_dot(jnp.where(off, a, 0.0), t) for a, t in zip(mats, ts)]
        ts = [t - _dot(t, m) for m, t in zip(ms, ts)]
        sh += 1
    return ts


def _rwkv_conv(pc, prev_row, next_row, cw):
    c = pc.shape[0]
    row = lax.broadcasted_iota(jnp.int32, (c, 1), 0)
    xm1 = jnp.where(row == 0, prev_row, pltpu.roll(pc, 1, 0))
    xp1 = jnp.where(row == c - 1, next_row, pltpu.roll(pc, c - 1, 0))
    return cw[0:1] * xm1 + cw[1:2] * pc + cw[2:3] * xp1


def _rwkv_chunk_inputs(x, z, bi, w0, wup, a0, aup, kkv, kav, hsum):
    c = x.shape[0]
    n = RWKV_HEAD_DIM
    r = x[:, 0:RWKV_WIDTH]
    k = x[:, RWKV_WIDTH:2 * RWKV_WIDTH]
    v = x[:, 2 * RWKV_WIDTH:3 * RWKV_WIDTH]
    wd = x[:, 3 * RWKV_WIDTH:3 * RWKV_WIDTH + W_LORA]
    ad = x[:, 3 * RWKV_WIDTH + W_LORA:3 * RWKV_WIDTH + W_LORA + A_LORA]
    lw = -DECAY_SCALE * _sigmoid(w0 + _dot(jnp.tanh(wd), wup, True))
    a = _sigmoid(a0 + _dot(ad, aup, True))
    idx_r = lax.broadcasted_iota(jnp.int32, (c, c), 0)
    idx_c = lax.broadcasted_iota(jnp.int32, (c, c), 1)
    if z == 0:
        incl = idx_c <= idx_r
        strict = idx_c < idx_r
        last = c - 1
    else:
        incl = idx_c >= idx_r
        strict = idx_c > idx_r
        last = 0
    tri = jnp.where(incl, 1.0, 0.0).astype(BF16)
    cum = sum(jnp.dot(tri, part, preferred_element_type=F32) for part in _split3(lw))
    tot = cum[last:last + 1]
    e1 = jnp.exp(cum)
    e2 = jnp.exp(cum - lw)
    e3 = jnp.exp(-cum)
    e4 = jnp.exp(tot - cum)
    etot = jnp.exp(tot)
    kkf = k * kkv
    hsq = sum(jnp.dot(part, hsum, preferred_element_type=F32) for part in _split3(kkf * kkf))
    kk = kkf * lax.rsqrt(hsq + 1e-12)
    b = a * kk
    kd = k * (1.0 + (a - 1.0) * kav)
    kkg, rg = kk * e2, r * e1
    kdd, bd = kd * e3, b * e3
    kend, bend = kd * e4, b * e4
    heads = []
    for h in range(RWKV_HEADS):
        sl = slice(h * n, (h + 1) * n)
        heads.append(dict(
            xq=jnp.concatenate([kkg[:, sl], rg[:, sl]], 0),
            yk=jnp.concatenate([kdd[:, sl], bd[:, sl]], 0),
            ke=jnp.concatenate([kend[:, sl], bend[:, sl]], 0),
            v=v[:, sl], etot=etot[:, sl], incl=incl, strict=strict, state=(z, bi, h)))
    return heads


def _rwkv_solve_all(heads, s_sc):
    c = heads[0]['v'].shape[0]
    idx_r = lax.broadcasted_iota(jnp.int32, (c, c), 0)
    idx_c = lax.broadcasted_iota(jnp.int32, (c, c), 1)
    gs = [_dot_nt(hd['xq'], hd['yk']) for hd in heads]
    ss = [s_sc[hd['state']] for hd in heads]
    xs = [_dot_nt(hd['xq'], s) for hd, s in zip(heads, ss)]
    avs = [_dot(jnp.where(hd['strict'], g[:c, :c], 0.0), hd['v']) for hd, g in zip(heads, gs)]
    ts = _tri_inverse_all([jnp.where(hd['strict'], g[:c, c:], 0.0) for hd, g in zip(heads, gs)], idx_r, idx_c)
    sas = [_dot(t, x[:c] + av) for t, x, av in zip(ts, xs, avs)]
    outs = []
    for hd, g, x, sa, s in zip(heads, gs, xs, sas, ss):
        a_r = jnp.concatenate([jnp.where(hd['incl'], g[c:, :c], 0.0), jnp.where(hd['incl'], -g[c:, c:], 0.0)], 1)
        outs.append(x[c:] + _dot(a_r, jnp.concatenate([hd['v'], sa], 0)))
        s_sc[hd['state']] = s * hd['etot'] + _dot_tn(jnp.concatenate([hd['v'], -sa], 0), hd['ke'])
    return outs


def _rwkv_kernel(pa_ref, pap_ref, pan_ref, pb_ref, pbp_ref, pbn_ref, cw_ref, w0_ref, wup_ref, a0_ref, aup_ref,
                 gup_ref, kk_ref, ka_ref, rk_ref, hsum_ref, y0_ref, y1_ref, bonus_ref, g_ref, s_sc, *, nl, nc):
    s = pl.program_id(1)

    @pl.when(s == 0)
    def _():
        s_sc[...] = jnp.zeros_like(s_sc)

    ca = (s + nl) % nc
    cb = nc - 1 - s
    cw = cw_ref[...]

    def load(p_ref, pp_ref, pn_ref, cidx, bi):
        first = jnp.logical_or(cidx == 0, cidx == nl)
        lastc = jnp.logical_or(cidx == nl - 1, cidx == nc - 1)
        prev_row = jnp.where(first, 0.0, pp_ref[bi, 7:8, :])
        next_row = jnp.where(lastc, 0.0, pn_ref[bi, 0:1, :])
        return _rwkv_conv(p_ref[bi], prev_row, next_row, cw)

    nb = pa_ref.shape[0]
    heads = []
    xas = []
    for bi in range(nb):
        xa = load(pa_ref, pap_ref, pan_ref, ca, bi)
        xb = load(pb_ref, pbp_ref, pbn_ref, cb, bi)
        xas.append(xa)
        heads += _rwkv_chunk_inputs(xa, 0, bi, w0_ref[0:1], wup_ref[0], a0_ref[0:1], aup_ref[0], kk_ref[...],
                                    ka_ref[...], hsum_ref[...])
        heads += _rwkv_chunk_inputs(xb, 1, bi, w0_ref[1:2], wup_ref[1], a0_ref[1:2], aup_ref[1], kk_ref[...],
                                    ka_ref[...], hsum_ref[...])
    outs = _rwkv_solve_all(heads, s_sc)
    nh = RWKV_HEADS
    for bi in range(nb):
        y0_ref[bi] = jnp.concatenate(outs[2 * nh * bi:2 * nh * bi + nh], 1).astype(BF16)
        y1_ref[bi] = jnp.concatenate(outs[2 * nh * bi + nh:2 * nh * (bi + 1)], 1).astype(BF16)
        xa = xas[bi]
        gd = xa[:, 3 * RWKV_WIDTH + W_LORA + A_LORA:]
        g_ref[bi] = _dot(_sigmoid(gd), gup_ref[...], True).astype(BF16)
        r = xa[:, 0:RWKV_WIDTH]
        k = xa[:, RWKV_WIDTH:2 * RWKV_WIDTH]
        v = xa[:, 2 * RWKV_WIDTH:3 * RWKV_WIDTH]
        rkr = r * k * rk_ref[...]
        bon = []
        for h in range(nh):
            sl = slice(h * RWKV_HEAD_DIM, (h + 1) * RWKV_HEAD_DIM)
            bon.append(jnp.sum(rkr[:, sl], -1, keepdims=True) * v[:, sl])
        bonus_ref[bi] = jnp.concatenate(bon, 1).astype(BF16)


def rwkv_scan(p, n_lat, cw, w0, wup, a0, aup, gup, kkv, kav, rk):
    bsz, lt, _ = p.shape
    nl = n_lat // CHUNK
    nc = lt // CHUNK
    nb8 = lt // 8
    ca = lambda s: (s + nl) % nc
    cb = lambda s: nc - 1 - s
    nb = RWKV_NB if bsz % RWKV_NB == 0 else 1
    pspec = lambda cf: pl.BlockSpec((nb, CHUNK, RWKV_IN), lambda b, s: (b, cf(s), 0))
    prev = lambda cf: pl.BlockSpec((nb, 8, RWKV_IN), lambda b, s: (b, jnp.maximum(cf(s) * (CHUNK // 8) - 1, 0), 0))
    nxt = lambda cf: pl.BlockSpec((nb, 8, RWKV_IN),
                                  lambda b, s: (b, jnp.minimum((cf(s) + 1) * (CHUNK // 8), nb8 - 1), 0))
    full = lambda a: pl.BlockSpec(a.shape, lambda b, s: (0,) * a.ndim)
    ospec = lambda cf: pl.BlockSpec((nb, CHUNK, RWKV_WIDTH), lambda b, s: (b, cf(s), 0))
    oshape = jax.ShapeDtypeStruct((bsz, lt, RWKV_WIDTH), BF16)
    kern = functools.partial(_rwkv_kernel, nl=nl, nc=nc)
    head_of = jnp.arange(RWKV_WIDTH) // RWKV_HEAD_DIM
    hsum = (head_of[:, None] == head_of[None, :]).astype(BF16)
    params = (cw, w0, wup, a0, aup, gup, kkv, kav, rk, hsum)
    return pl.pallas_call(
        kern,
        grid=(bsz // nb, nc),
        in_specs=[pspec(ca), prev(ca), nxt(ca), pspec(cb), prev(cb), nxt(cb)] + [full(a) for a in params],
        out_specs=[ospec(ca), ospec(cb), ospec(ca), ospec(ca)],
        out_shape=[oshape] * 4,
        scratch_shapes=[pltpu.VMEM((2, nb, RWKV_HEADS, RWKV_HEAD_DIM, RWKV_HEAD_DIM), F32)],
        compiler_params=_cparams(("parallel", "arbitrary")),
        name="rwkv_scan",
    )(p, p, p, p, p, p, *params)


def _s5_kernel(uf_ref, ur_ref, lam_ref, bm_ref, cm_ref, yf_ref, yr_ref, bf_sc, br_sc, h_sc):
    s = pl.program_id(0)

    @pl.when(s == 0)
    def _():
        h_sc[...] = jnp.zeros_like(h_sc)

    rows = bf_sc.shape[0]
    win = S5_WIDTH // 2
    ubs = [jnp.concatenate([u_ref[j] for j in range(S5_TILES)], 1).astype(BF16) for u_ref in (uf_ref, ur_ref)]
    for k in range(2):
        for z, sc in enumerate((bf_sc, br_sc)):
            sc[:, 2 * S5_HALF * k:2 * S5_HALF * (k + 1)] = jnp.dot(
                ubs[z][:, win * k:win * (k + 1)], bm_ref[z, k], preferred_element_type=F32)
    ntile = rows // 8
    lo = lax.broadcasted_iota(jnp.int32, (8, 1), 0) < 4

    for k in range(2):
        cre = slice(2 * S5_HALF * k, 2 * S5_HALF * k + S5_HALF)
        cim = slice(2 * S5_HALF * k + S5_HALF, 2 * S5_HALF * (k + 1))
        lr = lam_ref[0, :, S5_HALF * k:S5_HALF * (k + 1)]
        li = lam_ref[1, :, S5_HALF * k:S5_HALF * (k + 1)]
        hr, hi = h_sc[2 * k], h_sc[2 * k + 1]
        for jj in range(ntile):
            sf = slice(jj * 8, jj * 8 + 8)
            sr = slice((ntile - 1 - jj) * 8, (ntile - jj) * 8)
            tfr, tfi = bf_sc[sf, cre], bf_sc[sf, cim]
            trr, tri = br_sc[sr, cre], br_sc[sr, cim]
            xar, xai = jnp.where(lo, tfr, trr), jnp.where(lo, tfi, tri)
            xbr = pltpu.roll(jnp.where(lo, trr, tfr), 4, 0)
            xbi = pltpu.roll(jnp.where(lo, tri, tfi), 4, 0)
            ar, ai = lr * hr - li * hi + xar, lr * hi + li * hr + xai
            hr, hi = lr * ar - li * ai + xbr, lr * ai + li * ar + xbi
            rbr, rbi = pltpu.roll(hr, 4, 0), pltpu.roll(hi, 4, 0)
            bf_sc[sf, cre] = jnp.where(lo, ar, rbr)
            bf_sc[sf, cim] = jnp.where(lo, ai, rbi)
            br_sc[sr, cre] = jnp.where(lo, rbr, ar)
            br_sc[sr, cim] = jnp.where(lo, rbi, ai)
        h_sc[2 * k] = hr
        h_sc[2 * k + 1] = hi
        for z, (sc, y_ref) in enumerate(((bf_sc, yf_ref), (br_sc, yr_ref))):
            y = jnp.dot(sc[:, 2 * S5_HALF * k:2 * S5_HALF * (k + 1)].astype(BF16), cm_ref[z, k],
                        preferred_element_type=F32)
            for j in range(win // LANES):
                y_ref[k * (win // LANES) + j] = y[:, j * LANES:(j + 1) * LANES]


def s5_scan(u, n_lat, bsz, lam, bmat, cmat):
    assert bsz == 4
    rows_total = u.shape[1]
    lt = rows_total // bsz
    nl = n_lat // CHUNK
    nc = lt // CHUNK
    rb = CHUNK * bsz
    cf = lambda s: (0, (s + nl) % nc, 0)
    cr = lambda s: (0, nc - 1 - s, 0)
    full = lambda a: pl.BlockSpec(a.shape, lambda s: (0,) * a.ndim)
    oshape = jax.ShapeDtypeStruct(u.shape, F32)
    blk = (S5_TILES, rb, LANES)
    return pl.pallas_call(
        _s5_kernel,
        grid=(nc,),
        in_specs=[pl.BlockSpec(blk, cf), pl.BlockSpec(blk, cr), full(lam), full(bmat), full(cmat)],
        out_specs=[pl.BlockSpec(blk, cf), pl.BlockSpec(blk, cr)],
        out_shape=[oshape, oshape],
        scratch_shapes=[pltpu.VMEM((rb, 2 * S5_N), F32), pltpu.VMEM((rb, 2 * S5_N), F32),
                        pltpu.VMEM((4, 8, S5_HALF), F32)],
        compiler_params=_cparams(("arbitrary",)),
        name="s5_scan",
    )(u, u, lam, bmat, cmat)


def _s5_params(lam_re, lam_im, log_dt, b_re, b_im, c_re, c_im):
    dt = jnp.exp(log_dt.astype(F32))[..., None]
    mag = jnp.exp(lam_re * dt)
    lbr = mag * jnp.cos(lam_im * dt)
    lbi = mag * jnp.sin(lam_im * dt)
    den = lam_re * lam_re + lam_im * lam_im
    nr, ni = lbr - 1.0, lbi
    fr = (nr * lam_re + ni * lam_im) / den
    fi = (ni * lam_re - nr * lam_im) / den
    bbr = fr[..., None] * b_re - fi[..., None] * b_im
    bbi = fr[..., None] * b_im + fi[..., None] * b_re
    gh = S5_GROUPS // 2
    eye = jnp.eye(gh, dtype=F32)

    def bd_in(w):
        return jnp.einsum('gpc,gh->gchp', w, eye).reshape(S5_WIDTH // 2, S5_HALF)

    def bd_out(w):
        return jnp.einsum('gcp,gh->gphc', w, eye).reshape(S5_HALF, S5_WIDTH // 2)

    halves = [slice(0, gh), slice(gh, S5_GROUPS)]
    bmat = jnp.stack([jnp.stack([jnp.concatenate([bd_in(bbr[z, hs]), bd_in(bbi[z, hs])], 1) for hs in halves])
                      for z in range(2)])
    cmat = jnp.stack([jnp.stack([jnp.concatenate([bd_out(c_re[z, hs]), -bd_out(c_im[z, hs])], 0) for hs in halves])
                      for z in range(2)])
    rows = lambda t: jnp.concatenate([jnp.tile(t[0].reshape(1, S5_N), (4, 1)),
                                      jnp.tile(t[1].reshape(1, S5_N), (4, 1))], 0)
    lam = jnp.stack([rows(lbr), rows(lbi)])
    return lam, bmat.astype(BF16), cmat.astype(BF16)


def _gelu(x):
    return 0.5 * x * (1.0 + jnp.tanh(math.sqrt(2.0 / math.pi) * (x + 0.044715 * x * x * x)))


def _merge_kernel(y0_ref, y1_ref, bonus_ref, g_ref, s0_ref, s1_ref, u_ref, x_ref, c_ref, mod_ref, lnw_ref, lnb_ref,
                  d_ref, gluw_ref, glub_ref, wout_ref, o_ref, *, nlb):
    is_ctx = pl.program_id(0) >= nlb
    bsz, rows, _ = x_ref.shape
    n = RWKV_HEAD_DIM
    for b in range(bsz):
        y = y0_ref[b].astype(F32) + y1_ref[b].astype(F32)
        parts = []
        for h in range(RWKV_HEADS):
            yh = y[:, h * n:(h + 1) * n]
            mu = jnp.mean(yh, -1, keepdims=True)
            dlt = yh - mu
            var = jnp.mean(dlt * dlt, -1, keepdims=True)
            parts.append(dlt * lax.rsqrt(var + GN_EPS))
        yn = jnp.concatenate(parts, 1) * lnw_ref[...] + lnb_ref[...]
        rw = (yn + bonus_ref[b].astype(F32)) * g_ref[b].astype(F32)
        tm = pl.ds(b, rows, stride=bsz)
        seq = lambda ref: jnp.concatenate([ref[j, tm, :] for j in range(S5_TILES)], 1)
        ys = seq(s0_ref) + seq(s1_ref) + d_ref[...] * seq(u_ref)
        zz = _gelu(ys)
        gate = _sigmoid(_dot(zz, gluw_ref[...]) + glub_ref[...])
        cat = jnp.concatenate([rw, zz * gate], 1).astype(BF16)
        hb, (gate_res,) = _stream_rows(x_ref, c_ref, mod_ref, b, is_ctx, (2,))
        o_ref[b] = hb + gate_res * jnp.dot(cat, wout_ref[...], preferred_element_type=F32)


def merge(y0, y1, bonus, g, s0, s1, u, x, ctx, mod, lnw, lnb, dvec, gluw, glub, wout):
    bsz, n_lat, d = x.shape
    lt = n_lat + ctx.shape[1]
    nlb, sspecs = _stream_specs(x, ctx)
    rspec = pl.BlockSpec((bsz, SEQ_BLK, RWKV_WIDTH), lambda i: (0, i, 0))
    tspec = pl.BlockSpec((S5_TILES, bsz * SEQ_BLK, LANES), lambda i: (0, i, 0))
    full = lambda a: pl.BlockSpec(a.shape, lambda i: (0,) * a.ndim)
    params = (mod, lnw, lnb, dvec, gluw, glub, wout)
    return pl.pallas_call(
        functools.partial(_merge_kernel, nlb=nlb),
        grid=(lt // SEQ_BLK,),
        in_specs=[rspec, rspec, rspec, rspec, tspec, tspec, tspec] + sspecs + [full(a) for a in params],
        out_specs=pl.BlockSpec((bsz, SEQ_BLK, d), lambda i: (0, i, 0)),
        out_shape=jax.ShapeDtypeStruct((bsz, lt, d), F32),
        compiler_params=_cparams(("parallel",)),
        name="merge",
    )(y0, y1, bonus, g, s0, s1, u, x, ctx, *params)


def hybrid_layer(x, ctx, g1, mod, w_in, w_out, conv_w, w0, w_up, a0, a_up, g_up, k_k, k_a, r_k, ln_w, ln_b,
                 lam_re, lam_im, log_dt, b_re, b_im, c_re, c_im, dvec, glu_w, glu_b):
    bsz, n_lat, _ = x.shape
    p, u = inproj2(x, ctx, g1, mod, w_in.astype(BF16))
    y0, y1, bonus, g = rwkv_scan(p, n_lat, conv_w, w0, w_up, a0, a_up, g_up, k_k[None], k_a[None],
                                 r_k.reshape(1, RWKV_WIDTH))
    lam, bmat, cmat = _s5_params(lam_re, lam_im, log_dt, b_re, b_im, c_re, c_im)
    s0, s1 = s5_scan(u, n_lat, bsz, lam, bmat, cmat)
    gluw = jnp.einsum('gce,gh->gche', glu_w, jnp.eye(S5_GROUPS, dtype=F32)).reshape(S5_WIDTH, S5_WIDTH)
    return merge(y0, y1, bonus, g, s0, s1, u, x, ctx, mod, ln_w[None], ln_b[None], dvec[None],
                 gluw.astype(BF16), glu_b[None], w_out.astype(BF16))


def kernel(x, c, ctx, c_ctx, mod_w, mod_b, norm1_g, norm2_g, final_g, hy_w_in, hy_w_out, rk_conv, rk_w0, rk_w_up, rk_a0, rk_a_up, rk_g_up, rk_k_k, rk_k_a, rk_r_k, rk_ln_w, rk_ln_b, s5_lam_re, s5_lam_im, s5_log_dt, s5_b_re, s5_b_im, s5_c_re, s5_c_im, s5_d, s5_glu_w, s5_glu_b, mla_w_in, mla_q_norm, mla_q_up, mla_kv_norm, mla_kv_up, mla_w_out, router_w, router_b, ex_gate, ex_up, ex_down, sh_gate, sh_up, sh_down):
    bsz, n_lat, d = x.shape
    n_ctx = ctx.shape[1]
    depth = mod_w.shape[0]
    assert n_ctx == ROW_BLK and depth == 2 and bsz <= 7
    cc = jnp.concatenate([c, c_ctx[None], jnp.zeros((8 - bsz - 1, d), F32)], 0)
    rb = router_b[:, None].astype(F32)
    rw_t = router_w.T
    fg = final_g[None]
    for layer in range(depth):
        last = layer == depth - 1
        i = layer // 2
        mv = adaln(cc, mod_w[layer], mod_b[layer][None])
        m_l = mv[:bsz].reshape(bsz, 1, 6, d)
        m_c = jnp.broadcast_to(mv[bsz].reshape(1, 1, 6, d), (bsz, 1, 6, d))
        mod = jnp.concatenate([m_l, m_c], 1)
        g1 = norm1_g[layer][None]
        g2 = norm2_g[layer][None]
        if layer % 2 == 0:
            h = hybrid_layer(x, ctx, g1, mod, hy_w_in[i], hy_w_out[i], rk_conv[i], rk_w0[i], rk_w_up[i], rk_a0[i],
                             rk_a_up[i], rk_g_up[i], rk_k_k[i], rk_k_a[i], rk_r_k[i], rk_ln_w[i], rk_ln_b[i],
                             s5_lam_re[i], s5_lam_im[i], s5_log_dt[i], s5_b_re[i], s5_b_im[i], s5_c_re[i],
                             s5_c_im[i], s5_d[i], s5_glu_w[i], s5_glu_b[i])
        else:
            h = mla_layer(h, n_ctx, g1, mod, mla_w_in[i], mla_q_norm[i], mla_q_up[i], mla_kv_norm[i],
                          mla_kv_up[i], mla_w_out[i])
        bf = lambda t: t[layer].astype(BF16)
        h = moe(h, n_lat, g2, mod, rw_t, rb, bf(ex_gate), bf(ex_up), bf(ex_down), bf(sh_gate), bf(sh_up),
                bf(sh_down), fg, last)
    return h
```

```python
import functools
import math

import jax
import jax.numpy as jnp
from jax import lax
from jax.experimental import pallas as pl
from jax.experimental.pallas import tpu as pltpu

F32 = jnp.float32
BF16 = jnp.bfloat16
HI = lax.Precision.HIGHEST

D_MODEL = 1024
GRID_W = 64
NORM_EPS = 1e-6

RWKV_HEADS = 8
RWKV_HEAD_DIM = 64
RWKV_WIDTH = 512
W_LORA = 64
A_LORA = 64
G_LORA = 128
RWKV_IN = 3 * RWKV_WIDTH + W_LORA + A_LORA + G_LORA
DECAY_SCALE = math.exp(-0.5)
GN_EPS = 64e-5

S5_GROUP = 16
S5_GROUPS = 32
S5_WIDTH = 512
S5_STATE = 64
S5_N = S5_GROUPS * S5_STATE
S5_HALF = S5_N // 2
LANES = 128
S5_TILES = S5_WIDTH // LANES
HYB_IN = RWKV_IN + S5_WIDTH

MLA_HEADS = 16
Q_LORA = 256
KV_LORA = 128
QK_NOPE = 64
QK_ROPE = 32
V_DIM = 64
MLA_IN = Q_LORA + KV_LORA + QK_ROPE
MLA_SCALE = (QK_NOPE + QK_ROPE) ** -0.5
ROPE_AXIS_DIMS = QK_ROPE // 2
ROPE_BASE = 10000.0
HEAD_PAD = 128

N_EXPERTS = 16
N_GROUPS = 4
EXPERTS_PER_GROUP = 4
D_EXPERT = 256

ROW_BLK = 256
SEQ_BLK = 256
OUT_BLKS = (1024, 512, 256)
MOE_BLKS = (1024, 768, 512, 256)
MOE_EXPERTS_PER_STEP = 4
ATT_TQS = (1024, 512, 256)
ATT_TKS = (1408, 768, 256)
ATT_SUB = 256
CHUNK = 64
RWKV_NB = 2
VMEM_LIMIT = 56 * 1024 * 1024


def _cparams(sem):
    return pltpu.CompilerParams(dimension_semantics=sem, vmem_limit_bytes=VMEM_LIMIT)


def _norm_mod(x, g, shift, scale):
    y = x * lax.rsqrt(jnp.mean(x * x, -1, keepdims=True) + NORM_EPS) * g
    return y * (1.0 + scale) + shift


def _sigmoid(x):
    return 1.0 / (1.0 + jnp.exp(-x))


def _dot(a, b, hi=False):
    if hi:
        return jnp.dot(a, b, precision=HI, preferred_element_type=F32)
    return jnp.dot(a.astype(BF16), b.astype(BF16), preferred_element_type=F32)


def _dot_nt(a, b):
    return lax.dot_general(a.astype(BF16), b.astype(BF16), (((1,), (1,)), ((), ())), preferred_element_type=F32)


def _dot_tn(a, b):
    return jnp.dot(a.T.astype(BF16), b.astype(BF16), preferred_element_type=F32)


def _adaln_kernel(s_ref, w_ref, b_ref, o_ref):
    s = s_ref[...]
    s = s * _sigmoid(s)
    o_ref[...] = jnp.dot(s, w_ref[...], precision=HI, preferred_element_type=F32) + b_ref[...]


def adaln(cc, w, b):
    n = w.shape[1]
    tn = 512
    return pl.pallas_call(
        _adaln_kernel,
        grid=(n // tn,),
        in_specs=[pl.BlockSpec((8, D_MODEL), lambda j: (0, 0)),
                  pl.BlockSpec((D_MODEL, tn), lambda j: (0, j)),
                  pl.BlockSpec((1, tn), lambda j: (0, j))],
        out_specs=pl.BlockSpec((8, tn), lambda j: (0, j)),
        out_shape=jax.ShapeDtypeStruct((8, n), F32),
        compiler_params=_cparams(("parallel",)),
        name="adaln",
    )(cc, w, b)


def _mod_spec(n_lat, blk=ROW_BLK):
    return pl.BlockSpec((1, 1, 6, D_MODEL), lambda b, i, *_: (b, i // (n_lat // blk), 0, 0))


def _outproj_kernel(a_ref, w_ref, h_ref, mod_ref, o_ref):
    m = mod_ref[0, 0]
    o = jnp.dot(a_ref[0].astype(BF16), w_ref[...], preferred_element_type=F32)
    o_ref[0] = h_ref[0] + m[2:3] * o


def outproj(a, w, h, n_lat, mod):
    bsz, la, k = a.shape
    d = h.shape[2]
    blk = next(t for t in OUT_BLKS if la % t == 0 and n_lat % t == 0)
    return pl.pallas_call(
        _outproj_kernel,
        grid=(bsz, la // blk),
        in_specs=[pl.BlockSpec((1, blk, k), lambda b, i: (b, i, 0)),
                  pl.BlockSpec((k, d), lambda b, i: (0, 0)),
                  pl.BlockSpec((1, blk, d), lambda b, i: (b, i, 0)),
                  _mod_spec(n_lat, blk)],
        out_specs=pl.BlockSpec((1, blk, d), lambda b, i: (b, i, 0)),
        out_shape=jax.ShapeDtypeStruct((bsz, la, d), F32),
        compiler_params=_cparams(("parallel", "parallel")),
        name="outproj",
    )(a, w, h, mod)


def _route(scores, rb):
    t = scores.shape[1]
    biased = scores + rb
    col = [biased[e:e + 1, :] for e in range(N_EXPERTS)]
    sc = [scores[e:e + 1, :] for e in range(N_EXPERTS)]
    gscore = []
    for gi in range(N_GROUPS):
        a, b, c, d = col[4 * gi:4 * gi + 4]
        hi1, lo1 = jnp.maximum(a, b), jnp.minimum(a, b)
        hi2, lo2 = jnp.maximum(c, d), jnp.minimum(c, d)
        gscore.append(jnp.maximum(hi1, hi2) + jnp.maximum(jnp.minimum(hi1, hi2), jnp.maximum(lo1, lo2)))
    gsel = []
    taken = None
    for gi in range(N_GROUPS):
        best = None
        for gj in range(gi + 1, N_GROUPS):
            best = gscore[gj] if best is None else jnp.maximum(best, gscore[gj])
        s = (gscore[gi] >= best) if best is not None else jnp.full((1, t), True)
        if taken is not None:
            s = jnp.logical_and(s, jnp.logical_not(taken))
        taken = s if taken is None else jnp.logical_or(taken, s)
        gsel.append(s)
    masks = []
    for gi in range(N_GROUPS):
        v = col[4 * gi:4 * gi + 4]
        for j in range(4):
            rank = jnp.zeros((1, t), F32)
            for i in range(4):
                if i == j:
                    continue
                ahead = (v[i] >= v[j]) if i < j else (v[i] > v[j])
                rank = rank + ahead.astype(F32)
            masks.append(jnp.logical_and(gsel[gi], rank < 2.0))
    wsel = [jnp.where(masks[e], sc[e], 0.0) for e in range(N_EXPERTS)]
    denom = wsel[0]
    for e in range(1, N_EXPERTS):
        denom = denom + wsel[e]
    return [w / denom for w in wsel]


def _moe_kernel(h_ref, g_ref, mod_ref, rw_ref, rb_ref, wg_ref, wu_ref, wd_ref, sg_ref, su_ref, sd_ref, fg_ref, o_ref,
                f_sc, comb_sc, acc_sc, *, final_norm, n_lat, has_ctx):
    i = pl.program_id(1)
    e = pl.program_id(2)
    n_e = pl.num_programs(2)
    tm = f_sc.shape[0]

    def mod_row(k):
        lat = mod_ref[0, 0, k:k + 1, :]
        if not has_ctx:
            return lat
        row = i * tm + lax.broadcasted_iota(jnp.int32, (tm, 1), 0)
        return jnp.where(row >= n_lat, mod_ref[0, 1, k:k + 1, :], lat)

    @pl.when(e == 0)
    def _():
        f = _norm_mod(h_ref[0], g_ref[...], mod_row(3), mod_row(4))
        f_sc[...] = f.astype(BF16)
        logits = lax.dot_general(rw_ref[...], f, (((1,), (1,)), ((), ())), precision=HI,
                                 preferred_element_type=F32)
        cw = _route(_sigmoid(logits), rb_ref[...])
        sub = lax.broadcasted_iota(jnp.int32, (128, tm), 0)
        comb = jnp.zeros((128, tm), F32)
        for ei in range(N_EXPERTS):
            comb = jnp.where(sub == ei, cw[ei], comb)
        comb_sc[...] = comb.T

    fb = f_sc[...]

    @pl.when(e == 0)
    def _():
        gt = jnp.dot(fb, sg_ref[...], preferred_element_type=F32)
        up = jnp.dot(fb, su_ref[...], preferred_element_type=F32)
        act = gt * _sigmoid(gt) * up
        acc_sc[...] = jnp.dot(act.astype(BF16), sd_ref[...], preferred_element_type=F32)

    eps = wg_ref.shape[0]
    lane = lax.broadcasted_iota(jnp.int32, (tm, 128), 1)
    comb = comb_sc[...]
    gts = [jnp.dot(fb, wg_ref[j], preferred_element_type=F32) for j in range(eps)]
    ups = [jnp.dot(fb, wu_ref[j], preferred_element_type=F32) for j in range(eps)]
    tot = None
    for j in range(eps):
        cw = jnp.sum(jnp.where(lane == e * eps + j, comb, 0.0), axis=1, keepdims=True)
        act = gts[j] * _sigmoid(gts[j]) * ups[j] * cw
        dn = jnp.dot(act.astype(BF16), wd_ref[j], preferred_element_type=F32)
        tot = dn if tot is None else tot + dn
    acc_sc[...] += tot

    @pl.when(e == n_e - 1)
    def _():
        y = h_ref[0] + mod_row(5) * acc_sc[...]
        if final_norm:
            y = y * lax.rsqrt(jnp.mean(y * y, -1, keepdims=True) + NORM_EPS) * fg_ref[...]
        o_ref[0] = y


def moe(h, n_lat, g, mod, rw, rb, wg, wu, wd, sg, su, sd, fg, final_norm):
    bsz, lo, d = h.shape
    eps = MOE_EXPERTS_PER_STEP
    tm = next(t for t in MOE_BLKS if lo % t == 0)
    kern = functools.partial(_moe_kernel, final_norm=final_norm, n_lat=n_lat, has_ctx=lo > n_lat)
    return pl.pallas_call(
        kern,
        grid=(bsz, lo // tm, N_EXPERTS // eps),
        in_specs=[pl.BlockSpec((1, tm, d), lambda b, i, e: (b, i, 0)),
                  pl.BlockSpec((1, d), lambda b, i, e: (0, 0)),
                  pl.BlockSpec((1, 2, 6, d), lambda b, i, e: (b, 0, 0, 0)),
                  pl.BlockSpec((N_EXPERTS, d), lambda b, i, e: (0, 0)),
                  pl.BlockSpec((N_EXPERTS, 1), lambda b, i, e: (0, 0)),
                  pl.BlockSpec((eps, d, D_EXPERT), lambda b, i, e: (e, 0, 0)),
                  pl.BlockSpec((eps, d, D_EXPERT), lambda b, i, e: (e, 0, 0)),
                  pl.BlockSpec((eps, D_EXPERT, d), lambda b, i, e: (e, 0, 0)),
                  pl.BlockSpec((d, D_EXPERT), lambda b, i, e: (0, 0)),
                  pl.BlockSpec((d, D_EXPERT), lambda b, i, e: (0, 0)),
                  pl.BlockSpec((D_EXPERT, d), lambda b, i, e: (0, 0)),
                  pl.BlockSpec((1, d), lambda b, i, e: (0, 0))],
        out_specs=pl.BlockSpec((1, tm, d), lambda b, i, e: (b, i, 0)),
        out_shape=jax.ShapeDtypeStruct((bsz, lo, d), F32),
        scratch_shapes=[pltpu.VMEM((tm, d), BF16),
                        pltpu.VMEM((tm, 128), F32),
                        pltpu.VMEM((tm, d), F32)],
        compiler_params=_cparams(("parallel", "parallel", "arbitrary")),
        name="moe",
    )(h, g, mod, rw, rb, wg, wu, wd, sg, su, sd, fg)


def _mla_proj_kernel(h_ref, g_ref, mod_ref, win_ref, qn_ref, qup_ref, qsw_ref, kvn_ref, kup_ref, vup_ref, epl_ref,
                     esw_ref, one_ref, ct_ref, st_ref, q_ref, k_ref, v_ref):
    m = mod_ref[0, 0]
    a = _norm_mod(h_ref[0], g_ref[...], m[0:1], m[1:2])
    p = jnp.dot(a.astype(BF16), win_ref[...], preferred_element_type=F32)
    qd = p[:, :Q_LORA]
    kvd = p[:, Q_LORA:Q_LORA + KV_LORA]
    kpe = p[:, Q_LORA + KV_LORA:]
    qn = qd * lax.rsqrt(jnp.mean(qd * qd, -1, keepdims=True) + NORM_EPS) * qn_ref[...]
    kvn = kvd * lax.rsqrt(jnp.mean(kvd * kvd, -1, keepdims=True) + NORM_EPS) * kvn_ref[...]
    kvb = kvn.astype(BF16)
    ct = jnp.concatenate([ct_ref[...]] * MLA_HEADS, axis=1)
    st = jnp.concatenate([st_ref[...]] * MLA_HEADS, axis=1)
    qb = qn.astype(BF16)
    q = jnp.dot(qb, qup_ref[...], preferred_element_type=F32)
    q_sw = jnp.dot(qb, qsw_ref[...], preferred_element_type=F32)
    q_ref[0] = ((q * ct + q_sw * st) * MLA_SCALE).astype(BF16)
    kpb = kpe.astype(BF16)
    k = (jnp.dot(kvb, kup_ref[...], preferred_element_type=F32)
         + jnp.dot(kpb, epl_ref[...], preferred_element_type=F32))
    k_sw = jnp.dot(kpb, esw_ref[...], preferred_element_type=F32)
    k_ref[0] = (k * ct + k_sw * st).astype(BF16)
    v = jnp.dot(kvb, vup_ref[...], preferred_element_type=F32) + one_ref[...]
    v_ref[0] = v.astype(BF16)


def mla_proj(h, n_lat, g, mod, win, qn, qup, qsw, kvn, kup, vup, epl, esw, one, ct, st):
    bsz, lt, d = h.shape
    hp = MLA_HEADS * HEAD_PAD
    full = lambda shape: pl.BlockSpec(shape, lambda b, i: (0,) * len(shape))
    tab = pl.BlockSpec((ROW_BLK, HEAD_PAD), lambda b, i: (i, 0))
    out = pl.BlockSpec((1, ROW_BLK, hp), lambda b, i: (b, i, 0))
    return pl.pallas_call(
        _mla_proj_kernel,
        grid=(bsz, lt // ROW_BLK),
        in_specs=[pl.BlockSpec((1, ROW_BLK, d), lambda b, i: (b, i, 0)),
                  full((1, d)), _mod_spec(n_lat), full((d, MLA_IN)), full((1, Q_LORA)), full((Q_LORA, hp)),
                  full((Q_LORA, hp)), full((1, KV_LORA)), full((KV_LORA, hp)), full((KV_LORA, hp)),
                  full((QK_ROPE, hp)), full((QK_ROPE, hp)), full((1, hp)), tab, tab],
        out_specs=[out, out, out],
        out_shape=[jax.ShapeDtypeStruct((bsz, lt, hp), BF16)] * 3,
        compiler_params=_cparams(("parallel", "parallel")),
        name="mla_proj",
    )(h, g, mod, win, qn, qup, qsw, kvn, kup, vup, epl, esw, one, ct, st)


def _attn_kernel(q_ref, k_ref, v_ref, o_ref, *, tk, nk):
    tq = q_ref.shape[1]
    nsub = tq // ATT_SUB
    qs = [q_ref[0, i * ATT_SUB:(i + 1) * ATT_SUB, :] for i in range(nsub)]

    def scores(j):
        kc = k_ref[0, j * tk:(j + 1) * tk, :]
        return [lax.dot_general(q, kc, (((1,), (1,)), ((), ())), preferred_element_type=F32) for q in qs]

    ms = [jnp.full((ATT_SUB, 1), -1e30, F32) for _ in range(nsub)]
    accs = [jnp.zeros((ATT_SUB, HEAD_PAD), F32) for _ in range(nsub)]
    ss = scores(0)
    for j in range(nk):
        ss_next = scores(j + 1) if j + 1 < nk else None
        vc = v_ref[0, j * tk:(j + 1) * tk, :]
        for i in range(nsub):
            m_new = jnp.maximum(ms[i], jnp.max(ss[i], axis=1, keepdims=True))
            alpha = jnp.exp(ms[i] - m_new)
            p = jnp.exp(ss[i] - m_new).astype(BF16)
            accs[i] = accs[i] * alpha + jnp.dot(p, vc, preferred_element_type=F32)
            ms[i] = m_new
        ss = ss_next
    for i, acc in enumerate(accs):
        o_ref[0, i * ATT_SUB:(i + 1) * ATT_SUB, :] = (acc / acc[:, V_DIM:V_DIM + 1]).astype(BF16)


def attention(q, k, v, lq):
    bsz, lt, hp = q.shape
    tq = next(t for t in ATT_TQS if lq % t == 0)
    tk = next(t for t in ATT_TKS if lt % t == 0)
    kern = functools.partial(_attn_kernel, tk=tk, nk=lt // tk)
    return pl.pallas_call(
        kern,
        grid=(bsz, MLA_HEADS, lq // tq),
        in_specs=[pl.BlockSpec((1, tq, HEAD_PAD), lambda b, h, i: (b, i, h)),
                  pl.BlockSpec((1, lt, HEAD_PAD), lambda b, h, i: (b, 0, h)),
                  pl.BlockSpec((1, lt, HEAD_PAD), lambda b, h, i: (b, 0, h))],
        out_specs=pl.BlockSpec((1, tq, HEAD_PAD), lambda b, h, i: (b, i, h)),
        out_shape=jax.ShapeDtypeStruct((bsz, lq, hp), BF16),
        compiler_params=_cparams(("parallel", "parallel", "parallel")),
        name="attention",
    )(q, k, v)


def _rope_tables(n_lat, n_ctx):
    rows = n_lat // GRID_W
    row = jnp.repeat(jnp.arange(rows, dtype=F32), GRID_W)
    col = jnp.tile(jnp.arange(GRID_W, dtype=F32), rows)
    inv_freq = ROPE_BASE ** (-jnp.arange(0, ROPE_AXIS_DIMS, 2, dtype=F32) / ROPE_AXIS_DIMS)
    ang = jnp.concatenate([row[:, None] * inv_freq, col[:, None] * inv_freq], -1)
    cos = jnp.concatenate([jnp.cos(ang), jnp.ones((n_ctx, ROPE_AXIS_DIMS), F32)], 0)
    sin = jnp.concatenate([jnp.sin(ang), jnp.zeros((n_ctx, ROPE_AXIS_DIMS), F32)], 0)
    lt = n_lat + n_ctx
    one = jnp.ones((lt, QK_NOPE), F32)
    z32 = jnp.zeros((lt, HEAD_PAD - QK_NOPE - QK_ROPE), F32)
    z64 = jnp.zeros((lt, QK_NOPE), F32)
    ct = jnp.concatenate([one, cos, cos, z32], 1)
    st = jnp.concatenate([z64, -sin, sin, z32], 1)
    return ct, st


def _pad_heads(w, width, offset=0):
    k = w.shape[0]
    w = w.reshape(k, MLA_HEADS, width)
    w = jnp.pad(w, ((0, 0), (0, 0), (offset, HEAD_PAD - width - offset)))
    return w.reshape(k, MLA_HEADS * HEAD_PAD)


def mla_layer(h, n_ctx, g, mod, w_in, q_norm, q_up, kv_norm, kv_up, w_out):
    bsz, lt, d = h.shape
    n_lat = lt - n_ctx
    ct, st = _rope_tables(n_lat, n_ctx)
    half = ROPE_AXIS_DIMS
    swap = lambda t: jnp.concatenate([t[..., half:], t[..., :half]], -1)
    qup = _pad_heads(q_up, QK_NOPE + QK_ROPE).astype(BF16)
    q_pe = q_up.reshape(Q_LORA, MLA_HEADS, QK_NOPE + QK_ROPE)[:, :, QK_NOPE:]
    qsw = _pad_heads(swap(q_pe).reshape(Q_LORA, -1), QK_ROPE, QK_NOPE).astype(BF16)
    kvu = kv_up.reshape(KV_LORA, MLA_HEADS, QK_NOPE + V_DIM)
    kup = _pad_heads(kvu[:, :, :QK_NOPE].reshape(KV_LORA, -1), QK_NOPE).astype(BF16)
    vup = _pad_heads(kvu[:, :, QK_NOPE:].reshape(KV_LORA, -1), V_DIM).astype(BF16)
    eye = jnp.eye(QK_ROPE, dtype=F32)
    epl = _pad_heads(jnp.tile(eye, (1, MLA_HEADS)), QK_ROPE, QK_NOPE).astype(BF16)
    esw = _pad_heads(jnp.tile(swap(eye), (1, MLA_HEADS)), QK_ROPE, QK_NOPE).astype(BF16)
    one = _pad_heads(jnp.ones((1, MLA_HEADS), F32), 1, V_DIM)
    q, k, v = mla_proj(h, n_lat, g, mod, w_in.astype(BF16), q_norm[None], qup, qsw, kv_norm[None], kup, vup, epl,
                       esw, one, ct, st)
    o = attention(q, k, v, n_lat)
    wo = w_out.reshape(MLA_HEADS, V_DIM, d)
    wo = jnp.pad(wo, ((0, 0), (0, HEAD_PAD - V_DIM), (0, 0))).reshape(MLA_HEADS * HEAD_PAD, d).astype(BF16)
    return outproj(o, wo, h, n_lat, mod)


def _stream_rows(x_ref, c_ref, mod_ref, b, is_ctx, ks):
    rows, d = x_ref.shape[1:]
    lat_rows = jnp.where(is_ctx, 0, rows)
    hb = jnp.where(lax.broadcasted_iota(jnp.int32, (rows, d), 0) < lat_rows, x_ref[b], c_ref[b])
    lat1 = lax.broadcasted_iota(jnp.int32, (1, d), 0) < lat_rows
    ms = [jnp.where(lat1, mod_ref[b, 0, k:k + 1, :], mod_ref[b, 1, k:k + 1, :]) for k in ks]
    return hb, ms


def _inproj2_kernel(x_ref, c_ref, g_ref, mod_ref, w_ref, p_ref, u_ref, *, nlb):
    is_ctx = pl.program_id(0) >= nlb
    bsz, rows, _ = x_ref.shape
    for b in range(bsz):
        hb, (shift, scale) = _stream_rows(x_ref, c_ref, mod_ref, b, is_ctx, (0, 1))
        a = _norm_mod(hb, g_ref[...], shift, scale)
        o = jnp.dot(a.astype(BF16), w_ref[...], preferred_element_type=F32)
        p_ref[b] = o[:, :RWKV_IN]
        for j in range(S5_TILES):
            u_ref[j, pl.ds(b, rows, stride=bsz), :] = o[:, RWKV_IN + j * LANES:RWKV_IN + (j + 1) * LANES]


def _stream_specs(x, ctx):
    bsz, n_lat, d = x.shape
    nlb = n_lat // SEQ_BLK
    return nlb, [pl.BlockSpec((bsz, SEQ_BLK, d), lambda i: (0, jnp.minimum(i, nlb - 1), 0)),
                 pl.BlockSpec((bsz, SEQ_BLK, d), lambda i: (0, jnp.maximum(i - nlb, 0), 0))]


def inproj2(x, ctx, g, mod, w):
    bsz, n_lat, d = x.shape
    lt = n_lat + ctx.shape[1]
    nlb, sspecs = _stream_specs(x, ctx)
    full = lambda a: pl.BlockSpec(a.shape, lambda i: (0,) * a.ndim)
    return pl.pallas_call(
        functools.partial(_inproj2_kernel, nlb=nlb),
        grid=(lt // SEQ_BLK,),
        in_specs=sspecs + [full(g), full(mod), full(w)],
        out_specs=[pl.BlockSpec((bsz, SEQ_BLK, RWKV_IN), lambda i: (0, i, 0)),
                   pl.BlockSpec((S5_TILES, bsz * SEQ_BLK, LANES), lambda i: (0, i, 0))],
        out_shape=[jax.ShapeDtypeStruct((bsz, lt, RWKV_IN), F32),
                   jax.ShapeDtypeStruct((S5_TILES, lt * bsz, LANES), F32)],
        compiler_params=_cparams(("parallel",)),
        name="inproj2",
    )(x, ctx, g, mod, w)


def _tri_inverse_all(mats, idx_r, idx_c):
    c = mats[0].shape[0]
    eye = (idx_r == idx_c).astype(F32)
    blk8 = (idx_r >> 3) == (idx_c >> 3)
    ns = [jnp.where(blk8, -a, 0.0) for a in mats]
    ts = [eye + n for n in ns]
    n2 = [_dot(n, n) for n in ns]
    ts = [t + _dot(m, t) for m, t in zip(n2, ts)]
    n4 = [_dot(m, m) for m in n2]
    ts = [t + _dot(m, t) for m, t in zip(n4, ts)]
    sh = 3
    while (1 << sh) < c:
        off = jnp.logical_and((idx_r >> (sh + 1)) == (idx_c >> (sh + 1)), (idx_r >> sh) != (idx_c >> sh))
        ms = [_dot(jnp.where(off, a, 0.0), t) for a, t in zip(mats, ts)]
        ts = [t - _dot(t, m) for m, t in zip(ms, ts)]
        sh += 1
    return ts


def _rwkv_conv(pc, prev_row, next_row, cw):
    c = pc.shape[0]
    row = lax.broadcasted_iota(jnp.int32, (c, 1), 0)
    xm1 = jnp.where(row == 0, prev_row, pltpu.roll(pc, 1, 0))
    xp1 = jnp.where(row == c - 1, next_row, pltpu.roll(pc, c - 1, 0))
    return cw[0:1] * xm1 + cw[1:2] * pc + cw[2:3] * xp1


def _rwkv_chunk_inputs(x, z, bi, w0, wup, a0, aup, kkv, kav):
    c = x.shape[0]
    n = RWKV_HEAD_DIM
    r = x[:, 0:RWKV_WIDTH]
    k = x[:, RWKV_WIDTH:2 * RWKV_WIDTH]
    v = x[:, 2 * RWKV_WIDTH:3 * RWKV_WIDTH]
    wd = x[:, 3 * RWKV_WIDTH:3 * RWKV_WIDTH + W_LORA]
    ad = x[:, 3 * RWKV_WIDTH + W_LORA:3 * RWKV_WIDTH + W_LORA + A_LORA]
    lw = -DECAY_SCALE * _sigmoid(w0 + _dot(jnp.tanh(wd), wup, True))
    a = _sigmoid(a0 + _dot(ad, aup, True))
    idx_r = lax.broadcasted_iota(jnp.int32, (c, c), 0)
    idx_c = lax.broadcasted_iota(jnp.int32, (c, c), 1)
    if z == 0:
        incl = idx_c <= idx_r
        strict = idx_c < idx_r
        last = c - 1
    else:
        incl = idx_c >= idx_r
        strict = idx_c > idx_r
        last = 0
    cum = _dot(incl.astype(F32), lw, True)
    tot = cum[last:last + 1]
    e1 = jnp.exp(cum)
    e2 = jnp.exp(cum - lw)
    e3 = jnp.exp(-cum)
    e4 = jnp.exp(tot - cum)
    etot = jnp.exp(tot)
    kkf = k * kkv
    heads = []
    for h in range(RWKV_HEADS):
        sl = slice(h * n, (h + 1) * n)
        kk = kkf[:, sl]
        kk = kk * lax.rsqrt(jnp.sum(kk * kk, -1, keepdims=True) + 1e-12)
        ah = a[:, sl]
        bh = ah * kk
        kd = k[:, sl] * (1.0 + (ah - 1.0) * kav[:, sl])
        heads.append(dict(
            xq=jnp.concatenate([kk * e2[:, sl], r[:, sl] * e1[:, sl]], 0),
            yk=jnp.concatenate([kd * e3[:, sl], bh * e3[:, sl]], 0),
            ke=jnp.concatenate([kd * e4[:, sl], bh * e4[:, sl]], 0),
            v=v[:, sl], etot=etot[:, sl], incl=incl, strict=strict, state=(z, bi, h)))
    return heads


def _rwkv_solve_all(heads, s_sc):
    c = heads[0]['v'].shape[0]
    idx_r = lax.broadcasted_iota(jnp.int32, (c, c), 0)
    idx_c = lax.broadcasted_iota(jnp.int32, (c, c), 1)
    gs = [_dot_nt(hd['xq'], hd['yk']) for hd in heads]
    ss = [s_sc[hd['state']] for hd in heads]
    xs = [_dot_nt(hd['xq'], s) for hd, s in zip(heads, ss)]
    avs = [_dot(jnp.where(hd['strict'], g[:c, :c], 0.0), hd['v']) for hd, g in zip(heads, gs)]
    ts = _tri_inverse_all([jnp.where(hd['strict'], g[:c, c:], 0.0) for hd, g in zip(heads, gs)], idx_r, idx_c)
    sas = [_dot(t, x[:c] + av) for t, x, av in zip(ts, xs, avs)]
    outs = []
    for hd, g, x, sa, s in zip(heads, gs, xs, sas, ss):
        a_r = jnp.concatenate([jnp.where(hd['incl'], g[c:, :c], 0.0), jnp.where(hd['incl'], -g[c:, c:], 0.0)], 1)
        outs.append(x[c:] + _dot(a_r, jnp.concatenate([hd['v'], sa], 0)))
        s_sc[hd['state']] = s * hd['etot'] + _dot_tn(jnp.concatenate([hd['v'], -sa], 0), hd['ke'])
    return outs


def _rwkv_kernel(pa_ref, pap_ref, pan_ref, pb_ref, pbp_ref, pbn_ref, cw_ref, w0_ref, wup_ref, a0_ref, aup_ref,
                 gup_ref, kk_ref, ka_ref, rk_ref, y0_ref, y1_ref, bonus_ref, g_ref, s_sc, *, nl, nc):
    s = pl.program_id(1)

    @pl.when(s == 0)
    def _():
        s_sc[...] = jnp.zeros_like(s_sc)

    ca = (s + nl) % nc
    cb = nc - 1 - s
    cw = cw_ref[...]

    def load(p_ref, pp_ref, pn_ref, cidx, bi):
        first = jnp.logical_or(cidx == 0, cidx == nl)
        lastc = jnp.logical_or(cidx == nl - 1, cidx == nc - 1)
        prev_row = jnp.where(first, 0.0, pp_ref[bi, 7:8, :])
        next_row = jnp.where(lastc, 0.0, pn_ref[bi, 0:1, :])
        return _rwkv_conv(p_ref[bi], prev_row, next_row, cw)

    nb = pa_ref.shape[0]
    heads = []
    xas = []
    for bi in range(nb):
        xa = load(pa_ref, pap_ref, pan_ref, ca, bi)
        xb = load(pb_ref, pbp_ref, pbn_ref, cb, bi)
        xas.append(xa)
        heads += _rwkv_chunk_inputs(xa, 0, bi, w0_ref[0:1], wup_ref[0], a0_ref[0:1], aup_ref[0], kk_ref[...],
                                    ka_ref[...])
        heads += _rwkv_chunk_inputs(xb, 1, bi, w0_ref[1:2], wup_ref[1], a0_ref[1:2], aup_ref[1], kk_ref[...],
                                    ka_ref[...])
    outs = _rwkv_solve_all(heads, s_sc)
    nh = RWKV_HEADS
    for bi in range(nb):
        y0_ref[bi] = jnp.concatenate(outs[2 * nh * bi:2 * nh * bi + nh], 1).astype(BF16)
        y1_ref[bi] = jnp.concatenate(outs[2 * nh * bi + nh:2 * nh * (bi + 1)], 1).astype(BF16)
        xa = xas[bi]
        gd = xa[:, 3 * RWKV_WIDTH + W_LORA + A_LORA:]
        g_ref[bi] = _dot(_sigmoid(gd), gup_ref[...], True).astype(BF16)
        r = xa[:, 0:RWKV_WIDTH]
        k = xa[:, RWKV_WIDTH:2 * RWKV_WIDTH]
        v = xa[:, 2 * RWKV_WIDTH:3 * RWKV_WIDTH]
        rkr = r * k * rk_ref[...]
        bon = []
        for h in range(nh):
            sl = slice(h * RWKV_HEAD_DIM, (h + 1) * RWKV_HEAD_DIM)
            bon.append(jnp.sum(rkr[:, sl], -1, keepdims=True) * v[:, sl])
        bonus_ref[bi] = jnp.concatenate(bon, 1).astype(BF16)


def rwkv_scan(p, n_lat, cw, w0, wup, a0, aup, gup, kkv, kav, rk):
    bsz, lt, _ = p.shape
    nl = n_lat // CHUNK
    nc = lt // CHUNK
    nb8 = lt // 8
    ca = lambda s: (s + nl) % nc
    cb = lambda s: nc - 1 - s
    nb = RWKV_NB if bsz % RWKV_NB == 0 else 1
    pspec = lambda cf: pl.BlockSpec((nb, CHUNK, RWKV_IN), lambda b, s: (b, cf(s), 0))
    prev = lambda cf: pl.BlockSpec((nb, 8, RWKV_IN), lambda b, s: (b, jnp.maximum(cf(s) * (CHUNK // 8) - 1, 0), 0))
    nxt = lambda cf: pl.BlockSpec((nb, 8, RWKV_IN),
                                  lambda b, s: (b, jnp.minimum((cf(s) + 1) * (CHUNK // 8), nb8 - 1), 0))
    full = lambda a: pl.BlockSpec(a.shape, lambda b, s: (0,) * a.ndim)
    ospec = lambda cf: pl.BlockSpec((nb, CHUNK, RWKV_WIDTH), lambda b, s: (b, cf(s), 0))
    oshape = jax.ShapeDtypeStruct((bsz, lt, RWKV_WIDTH), BF16)
    kern = functools.partial(_rwkv_kernel, nl=nl, nc=nc)
    params = (cw, w0, wup, a0, aup, gup, kkv, kav, rk)
    return pl.pallas_call(
        kern,
        grid=(bsz // nb, nc),
        in_specs=[pspec(ca), prev(ca), nxt(ca), pspec(cb), prev(cb), nxt(cb)] + [full(a) for a in params],
        out_specs=[ospec(ca), ospec(cb), ospec(ca), ospec(ca)],
        out_shape=[oshape] * 4,
        scratch_shapes=[pltpu.VMEM((2, nb, RWKV_HEADS, RWKV_HEAD_DIM, RWKV_HEAD_DIM), F32)],
        compiler_params=_cparams(("parallel", "arbitrary")),
        name="rwkv_scan",
    )(p, p, p, p, p, p, *params)


def _s5_kernel(uf_ref, ur_ref, lam_ref, bm_ref, cm_ref, yf_ref, yr_ref, bf_sc, br_sc, h_sc):
    s = pl.program_id(0)

    @pl.when(s == 0)
    def _():
        h_sc[...] = jnp.zeros_like(h_sc)

    rows = bf_sc.shape[0]
    win = S5_WIDTH // 2
    ubs = [jnp.concatenate([u_ref[j] for j in range(S5_TILES)], 1).astype(BF16) for u_ref in (uf_ref, ur_ref)]
    for k in range(2):
        for z, sc in enumerate((bf_sc, br_sc)):
            sc[:, 2 * S5_HALF * k:2 * S5_HALF * (k + 1)] = jnp.dot(
                ubs[z][:, win * k:win * (k + 1)], bm_ref[z, k], preferred_element_type=F32)
    ntile = rows // 8
    lo = lax.broadcasted_iota(jnp.int32, (8, 1), 0) < 4

    for k in range(2):
        cre = slice(2 * S5_HALF * k, 2 * S5_HALF * k + S5_HALF)
        cim = slice(2 * S5_HALF * k + S5_HALF, 2 * S5_HALF * (k + 1))
        lr = lam_ref[0, :, S5_HALF * k:S5_HALF * (k + 1)]
        li = lam_ref[1, :, S5_HALF * k:S5_HALF * (k + 1)]
        hr, hi = h_sc[2 * k], h_sc[2 * k + 1]
        for jj in range(ntile):
            sf = slice(jj * 8, jj * 8 + 8)
            sr = slice((ntile - 1 - jj) * 8, (ntile - jj) * 8)
            tfr, tfi = bf_sc[sf, cre], bf_sc[sf, cim]
            trr, tri = br_sc[sr, cre], br_sc[sr, cim]
            xar, xai = jnp.where(lo, tfr, trr), jnp.where(lo, tfi, tri)
            xbr = pltpu.roll(jnp.where(lo, trr, tfr), 4, 0)
            xbi = pltpu.roll(jnp.where(lo, tri, tfi), 4, 0)
            ar, ai = lr * hr - li * hi + xar, lr * hi + li * hr + xai
            hr, hi = lr * ar - li * ai + xbr, lr * ai + li * ar + xbi
            rbr, rbi = pltpu.roll(hr, 4, 0), pltpu.roll(hi, 4, 0)
            bf_sc[sf, cre] = jnp.where(lo, ar, rbr)
            bf_sc[sf, cim] = jnp.where(lo, ai, rbi)
            br_sc[sr, cre] = jnp.where(lo, rbr, ar)
            br_sc[sr, cim] = jnp.where(lo, rbi, ai)
        h_sc[2 * k] = hr
        h_sc[2 * k + 1] = hi
        for z, (sc, y_ref) in enumerate(((bf_sc, yf_ref), (br_sc, yr_ref))):
            y = jnp.dot(sc[:, 2 * S5_HALF * k:2 * S5_HALF * (k + 1)].astype(BF16), cm_ref[z, k],
                        preferred_element_type=F32)
            for j in range(win // LANES):
                y_ref[k * (win // LANES) + j] = y[:, j * LANES:(j + 1) * LANES]


def s5_scan(u, n_lat, bsz, lam, bmat, cmat):
    assert bsz == 4
    rows_total = u.shape[1]
    lt = rows_total // bsz
    nl = n_lat // CHUNK
    nc = lt // CHUNK
    rb = CHUNK * bsz
    cf = lambda s: (0, (s + nl) % nc, 0)
    cr = lambda s: (0, nc - 1 - s, 0)
    full = lambda a: pl.BlockSpec(a.shape, lambda s: (0,) * a.ndim)
    oshape = jax.ShapeDtypeStruct(u.shape, F32)
    blk = (S5_TILES, rb, LANES)
    return pl.pallas_call(
        _s5_kernel,
        grid=(nc,),
        in_specs=[pl.BlockSpec(blk, cf), pl.BlockSpec(blk, cr), full(lam), full(bmat), full(cmat)],
        out_specs=[pl.BlockSpec(blk, cf), pl.BlockSpec(blk, cr)],
        out_shape=[oshape, oshape],
        scratch_shapes=[pltpu.VMEM((rb, 2 * S5_N), F32), pltpu.VMEM((rb, 2 * S5_N), F32),
                        pltpu.VMEM((4, 8, S5_HALF), F32)],
        compiler_params=_cparams(("arbitrary",)),
        name="s5_scan",
    )(u, u, lam, bmat, cmat)


def _s5_params(lam_re, lam_im, log_dt, b_re, b_im, c_re, c_im):
    dt = jnp.exp(log_dt.astype(F32))[..., None]
    mag = jnp.exp(lam_re * dt)
    lbr = mag * jnp.cos(lam_im * dt)
    lbi = mag * jnp.sin(lam_im * dt)
    den = lam_re * lam_re + lam_im * lam_im
    nr, ni = lbr - 1.0, lbi
    fr = (nr * lam_re + ni * lam_im) / den
    fi = (ni * lam_re - nr * lam_im) / den
    bbr = fr[..., None] * b_re - fi[..., None] * b_im
    bbi = fr[..., None] * b_im + fi[..., None] * b_re
    gh = S5_GROUPS // 2
    eye = jnp.eye(gh, dtype=F32)

    def bd_in(w):
        return jnp.einsum('gpc,gh->gchp', w, eye).reshape(S5_WIDTH // 2, S5_HALF)

    def bd_out(w):
        return jnp.einsum('gcp,gh->gphc', w, eye).reshape(S5_HALF, S5_WIDTH // 2)

    halves = [slice(0, gh), slice(gh, S5_GROUPS)]
    bmat = jnp.stack([jnp.stack([jnp.concatenate([bd_in(bbr[z, hs]), bd_in(bbi[z, hs])], 1) for hs in halves])
                      for z in range(2)])
    cmat = jnp.stack([jnp.stack([jnp.concatenate([bd_out(c_re[z, hs]), -bd_out(c_im[z, hs])], 0) for hs in halves])
                      for z in range(2)])
    rows = lambda t: jnp.concatenate([jnp.tile(t[0].reshape(1, S5_N), (4, 1)),
                                      jnp.tile(t[1].reshape(1, S5_N), (4, 1))], 0)
    lam = jnp.stack([rows(lbr), rows(lbi)])
    return lam, bmat.astype(BF16), cmat.astype(BF16)


def _gelu(x):
    return 0.5 * x * (1.0 + jnp.tanh(math.sqrt(2.0 / math.pi) * (x + 0.044715 * x * x * x)))


def _merge_kernel(y0_ref, y1_ref, bonus_ref, g_ref, s0_ref, s1_ref, u_ref, x_ref, c_ref, mod_ref, lnw_ref, lnb_ref,
                  d_ref, gluw_ref, glub_ref, wout_ref, o_ref, *, nlb):
    is_ctx = pl.program_id(0) >= nlb
    bsz, rows, _ = x_ref.shape
    n = RWKV_HEAD_DIM
    for b in range(bsz):
        y = y0_ref[b].astype(F32) + y1_ref[b].astype(F32)
        parts = []
        for h in range(RWKV_HEADS):
            yh = y[:, h * n:(h + 1) * n]
            mu = jnp.mean(yh, -1, keepdims=True)
            dlt = yh - mu
            var = jnp.mean(dlt * dlt, -1, keepdims=True)
            parts.append(dlt * lax.rsqrt(var + GN_EPS))
        yn = jnp.concatenate(parts, 1) * lnw_ref[...] + lnb_ref[...]
        rw = (yn + bonus_ref[b].astype(F32)) * g_ref[b].astype(F32)
        tm = pl.ds(b, rows, stride=bsz)
        seq = lambda ref: jnp.concatenate([ref[j, tm, :] for j in range(S5_TILES)], 1)
        ys = seq(s0_ref) + seq(s1_ref) + d_ref[...] * seq(u_ref)
        zz = _gelu(ys)
        gate = _sigmoid(_dot(zz, gluw_ref[...]) + glub_ref[...])
        cat = jnp.concatenate([rw, zz * gate], 1).astype(BF16)
        hb, (gate_res,) = _stream_rows(x_ref, c_ref, mod_ref, b, is_ctx, (2,))
        o_ref[b] = hb + gate_res * jnp.dot(cat, wout_ref[...], preferred_element_type=F32)


def merge(y0, y1, bonus, g, s0, s1, u, x, ctx, mod, lnw, lnb, dvec, gluw, glub, wout):
    bsz, n_lat, d = x.shape
    lt = n_lat + ctx.shape[1]
    nlb, sspecs = _stream_specs(x, ctx)
    rspec = pl.BlockSpec((bsz, SEQ_BLK, RWKV_WIDTH), lambda i: (0, i, 0))
    tspec = pl.BlockSpec((S5_TILES, bsz * SEQ_BLK, LANES), lambda i: (0, i, 0))
    full = lambda a: pl.BlockSpec(a.shape, lambda i: (0,) * a.ndim)
    params = (mod, lnw, lnb, dvec, gluw, glub, wout)
    return pl.pallas_call(
        functools.partial(_merge_kernel, nlb=nlb),
        grid=(lt // SEQ_BLK,),
        in_specs=[rspec, rspec, rspec, rspec, tspec, tspec, tspec] + sspecs + [full(a) for a in params],
        out_specs=pl.BlockSpec((bsz, SEQ_BLK, d), lambda i: (0, i, 0)),
        out_shape=jax.ShapeDtypeStruct((bsz, lt, d), F32),
        compiler_params=_cparams(("parallel",)),
        name="merge",
    )(y0, y1, bonus, g, s0, s1, u, x, ctx, *params)


def hybrid_layer(x, ctx, g1, mod, w_in, w_out, conv_w, w0, w_up, a0, a_up, g_up, k_k, k_a, r_k, ln_w, ln_b,
                 lam_re, lam_im, log_dt, b_re, b_im, c_re, c_im, dvec, glu_w, glu_b):
    bsz, n_lat, _ = x.shape
    p, u = inproj2(x, ctx, g1, mod, w_in.astype(BF16))
    y0, y1, bonus, g = rwkv_scan(p, n_lat, conv_w, w0, w_up, a0, a_up, g_up, k_k[None], k_a[None],
                                 r_k.reshape(1, RWKV_WIDTH))
    lam, bmat, cmat = _s5_params(lam_re, lam_im, log_dt, b_re, b_im, c_re, c_im)
    s0, s1 = s5_scan(u, n_lat, bsz, lam, bmat, cmat)
    gluw = jnp.einsum('gce,gh->gche', glu_w, jnp.eye(S5_GROUPS, dtype=F32)).reshape(S5_WIDTH, S5_WIDTH)
    return merge(y0, y1, bonus, g, s0, s1, u, x, ctx, mod, ln_w[None], ln_b[None], dvec[None],
                 gluw.astype(BF16), glu_b[None], w_out.astype(BF16))


def kernel(x, c, ctx, c_ctx, mod_w, mod_b, norm1_g, norm2_g, final_g, hy_w_in, hy_w_out, rk_conv, rk_w0, rk_w_up, rk_a0, rk_a_up, rk_g_up, rk_k_k, rk_k_a, rk_r_k, rk_ln_w, rk_ln_b, s5_lam_re, s5_lam_im, s5_log_dt, s5_b_re, s5_b_im, s5_c_re, s5_c_im, s5_d, s5_glu_w, s5_glu_b, mla_w_in, mla_q_norm, mla_q_up, mla_kv_norm, mla_kv_up, mla_w_out, router_w, router_b, ex_gate, ex_up, ex_down, sh_gate, sh_up, sh_down):
    bsz, n_lat, d = x.shape
    n_ctx = ctx.shape[1]
    depth = mod_w.shape[0]
    assert n_ctx == ROW_BLK and depth == 2 and bsz <= 7
    cc = jnp.concatenate([c, c_ctx[None], jnp.zeros((8 - bsz - 1, d), F32)], 0)
    rb = router_b[:, None].astype(F32)
    rw_t = router_w.T
    fg = final_g[None]
    for layer in range(depth):
        last = layer == depth - 1
        i = layer // 2
        mv = adaln(cc, mod_w[layer], mod_b[layer][None])
        m_l = mv[:bsz].reshape(bsz, 1, 6, d)
        m_c = jnp.broadcast_to(mv[bsz].reshape(1, 1, 6, d), (bsz, 1, 6, d))
        mod = jnp.concatenate([m_l, m_c], 1)
        g1 = norm1_g[layer][None]
        g2 = norm2_g[layer][None]
        if layer % 2 == 0:
            h = hybrid_layer(x, ctx, g1, mod, hy_w_in[i], hy_w_out[i], rk_conv[i], rk_w0[i], rk_w_up[i], rk_a0[i],
                             rk_a_up[i], rk_g_up[i], rk_k_k[i], rk_k_a[i], rk_r_k[i], rk_ln_w[i], rk_ln_b[i],
                             s5_lam_re[i], s5_lam_im[i], s5_log_dt[i], s5_b_re[i], s5_b_im[i], s5_c_re[i],
                             s5_c_im[i], s5_d[i], s5_glu_w[i], s5_glu_b[i])
        else:
            h = mla_layer(h, n_ctx, g1, mod, mla_w_in[i], mla_q_norm[i], mla_q_up[i], mla_kv_norm[i],
                          mla_kv_up[i], mla_w_out[i])
        bf = lambda t: t[layer].astype(BF16)
        h = moe(h, n_lat, g2, mod, rw_t, rb, bf(ex_gate), bf(ex_up), bf(ex_down), bf(sh_gate), bf(sh_up),
                bf(sh_down), fg, last)
    return h
```

```python
import functools
import math

import jax
import jax.numpy as jnp
from jax import lax
from jax.experimental import pallas as pl
from jax.experimental.pallas import tpu as pltpu

F32 = jnp.float32
BF16 = jnp.bfloat16
HI = lax.Precision.HIGHEST

D_MODEL = 1024
GRID_W = 64
NORM_EPS = 1e-6

RWKV_HEADS = 8
RWKV_HEAD_DIM = 64
RWKV_WIDTH = 512
W_LORA = 64
A_LORA = 64
G_LORA = 128
RWKV_IN = 3 * RWKV_WIDTH + W_LORA + A_LORA + G_LORA
DECAY_SCALE = math.exp(-0.5)
GN_EPS = 64e-5

S5_GROUP = 16
S5_GROUPS = 32
S5_WIDTH = 512
S5_STATE = 64
S5_N = S5_GROUPS * S5_STATE
S5_HALF = S5_N // 2
LANES = 128
S5_TILES = S5_WIDTH // LANES
HYB_IN = RWKV_IN + S5_WIDTH

MLA_HEADS = 16
Q_LORA = 256
KV_LORA = 128
QK_NOPE = 64
QK_ROPE = 32
V_DIM = 64
MLA_IN = Q_LORA + KV_LORA + QK_ROPE
MLA_SCALE = (QK_NOPE + QK_ROPE) ** -0.5
ROPE_AXIS_DIMS = QK_ROPE // 2
ROPE_BASE = 10000.0
HEAD_PAD = 128

N_EXPERTS = 16
N_GROUPS = 4
EXPERTS_PER_GROUP = 4
D_EXPERT = 256

ROW_BLK = 256
SEQ_BLK = 256
OUT_BLKS = (1024, 512, 256)
MLA_BLKS = (768, 256)
MOE_BLKS = (1024, 768, 512, 256)
MOE_EXPERTS_PER_STEP = 4
ATT_TQS = (1024, 512, 256)
ATT_TKS = (1408, 768, 256)
ATT_SUB = 256
CHUNK = 64
RWKV_NB = 2
VMEM_LIMIT = 56 * 1024 * 1024


def _cparams(sem):
    return pltpu.CompilerParams(dimension_semantics=sem, vmem_limit_bytes=VMEM_LIMIT)


def _norm_mod(x, g, shift, scale):
    y = x * lax.rsqrt(jnp.mean(x * x, -1, keepdims=True) + NORM_EPS) * g
    return y * (1.0 + scale) + shift


def _sigmoid(x):
    return 1.0 / (1.0 + jnp.exp(-x))


def _dot(a, b, hi=False):
    if hi:
        return jnp.dot(a, b, precision=HI, preferred_element_type=F32)
    return jnp.dot(a.astype(BF16), b.astype(BF16), preferred_element_type=F32)


def _dot_nt(a, b):
    return lax.dot_general(a.astype(BF16), b.astype(BF16), (((1,), (1,)), ((), ())), preferred_element_type=F32)


def _dot_tn(a, b):
    return jnp.dot(a.T.astype(BF16), b.astype(BF16), preferred_element_type=F32)


def _adaln_kernel(s_ref, w_ref, b_ref, o_ref):
    s = s_ref[...]
    s = s * _sigmoid(s)
    o_ref[...] = jnp.dot(s, w_ref[...], precision=HI, preferred_element_type=F32) + b_ref[...]


def adaln(cc, w, b):
    n = w.shape[1]
    tn = 512
    return pl.pallas_call(
        _adaln_kernel,
        grid=(n // tn,),
        in_specs=[pl.BlockSpec((8, D_MODEL), lambda j: (0, 0)),
                  pl.BlockSpec((D_MODEL, tn), lambda j: (0, j)),
                  pl.BlockSpec((1, tn), lambda j: (0, j))],
        out_specs=pl.BlockSpec((8, tn), lambda j: (0, j)),
        out_shape=jax.ShapeDtypeStruct((8, n), F32),
        compiler_params=_cparams(("parallel",)),
        name="adaln",
    )(cc, w, b)


def _mod_spec(n_lat, blk=ROW_BLK):
    return pl.BlockSpec((1, 1, 6, D_MODEL), lambda b, i, *_: (b, i // (n_lat // blk), 0, 0))


def _outproj_kernel(a_ref, w_ref, h_ref, mod_ref, o_ref):
    m = mod_ref[0, 0]
    o = jnp.dot(a_ref[0].astype(BF16), w_ref[...], preferred_element_type=F32)
    o_ref[0] = h_ref[0] + m[2:3] * o


def outproj(a, w, h, n_lat, mod):
    bsz, la, k = a.shape
    d = h.shape[2]
    blk = next(t for t in OUT_BLKS if la % t == 0 and n_lat % t == 0)
    return pl.pallas_call(
        _outproj_kernel,
        grid=(bsz, la // blk),
        in_specs=[pl.BlockSpec((1, blk, k), lambda b, i: (b, i, 0)),
                  pl.BlockSpec((k, d), lambda b, i: (0, 0)),
                  pl.BlockSpec((1, blk, d), lambda b, i: (b, i, 0)),
                  _mod_spec(n_lat, blk)],
        out_specs=pl.BlockSpec((1, blk, d), lambda b, i: (b, i, 0)),
        out_shape=jax.ShapeDtypeStruct((bsz, la, d), F32),
        compiler_params=_cparams(("parallel", "parallel")),
        name="outproj",
    )(a, w, h, mod)


def _route(scores, rb):
    t = scores.shape[1]
    biased = scores + rb
    col = [biased[e:e + 1, :] for e in range(N_EXPERTS)]
    sc = [scores[e:e + 1, :] for e in range(N_EXPERTS)]
    gscore = []
    for gi in range(N_GROUPS):
        a, b, c, d = col[4 * gi:4 * gi + 4]
        hi1, lo1 = jnp.maximum(a, b), jnp.minimum(a, b)
        hi2, lo2 = jnp.maximum(c, d), jnp.minimum(c, d)
        gscore.append(jnp.maximum(hi1, hi2) + jnp.maximum(jnp.minimum(hi1, hi2), jnp.maximum(lo1, lo2)))
    gsel = []
    taken = None
    for gi in range(N_GROUPS):
        best = None
        for gj in range(gi + 1, N_GROUPS):
            best = gscore[gj] if best is None else jnp.maximum(best, gscore[gj])
        s = (gscore[gi] >= best) if best is not None else jnp.full((1, t), True)
        if taken is not None:
            s = jnp.logical_and(s, jnp.logical_not(taken))
        taken = s if taken is None else jnp.logical_or(taken, s)
        gsel.append(s)
    masks = []
    for gi in range(N_GROUPS):
        v = col[4 * gi:4 * gi + 4]
        for j in range(4):
            rank = jnp.zeros((1, t), F32)
            for i in range(4):
                if i == j:
                    continue
                ahead = (v[i] >= v[j]) if i < j else (v[i] > v[j])
                rank = rank + ahead.astype(F32)
            masks.append(jnp.logical_and(gsel[gi], rank < 2.0))
    wsel = [jnp.where(masks[e], sc[e], 0.0) for e in range(N_EXPERTS)]
    denom = wsel[0]
    for e in range(1, N_EXPERTS):
        denom = denom + wsel[e]
    return [w / denom for w in wsel]


def _moe_kernel(h_ref, g_ref, mod_ref, rw_ref, rb_ref, wg_ref, wu_ref, wd_ref, sg_ref, su_ref, sd_ref, fg_ref, o_ref,
                f_sc, comb_sc, acc_sc, *, final_norm, n_lat, has_ctx):
    i = pl.program_id(1)
    e = pl.program_id(2)
    n_e = pl.num_programs(2)
    tm = f_sc.shape[0]

    def mod_row(k):
        lat = mod_ref[0, 0, k:k + 1, :]
        if not has_ctx:
            return lat
        row = i * tm + lax.broadcasted_iota(jnp.int32, (tm, 1), 0)
        return jnp.where(row >= n_lat, mod_ref[0, 1, k:k + 1, :], lat)

    @pl.when(e == 0)
    def _():
        f = _norm_mod(h_ref[0], g_ref[...], mod_row(3), mod_row(4))
        f_sc[...] = f.astype(BF16)
        logits = lax.dot_general(rw_ref[...], f, (((1,), (1,)), ((), ())), precision=HI,
                                 preferred_element_type=F32)
        cw = _route(_sigmoid(logits), rb_ref[...])
        sub = lax.broadcasted_iota(jnp.int32, (128, tm), 0)
        comb = jnp.zeros((128, tm), F32)
        for ei in range(N_EXPERTS):
            comb = jnp.where(sub == ei, cw[ei], comb)
        comb_sc[...] = comb.T

    fb = f_sc[...]

    @pl.when(e == 0)
    def _():
        gt = jnp.dot(fb, sg_ref[...], preferred_element_type=F32)
        up = jnp.dot(fb, su_ref[...], preferred_element_type=F32)
        act = gt * _sigmoid(gt) * up
        acc_sc[...] = jnp.dot(act.astype(BF16), sd_ref[...], preferred_element_type=F32)

    eps = wg_ref.shape[0]
    lane = lax.broadcasted_iota(jnp.int32, (tm, 128), 1)
    comb = comb_sc[...]
    gts = [jnp.dot(fb, wg_ref[j], preferred_element_type=F32) for j in range(eps)]
    ups = [jnp.dot(fb, wu_ref[j], preferred_element_type=F32) for j in range(eps)]
    tot = None
    for j in range(eps):
        cw = jnp.sum(jnp.where(lane == e * eps + j, comb, 0.0), axis=1, keepdims=True)
        act = gts[j] * _sigmoid(gts[j]) * ups[j] * cw
        dn = jnp.dot(act.astype(BF16), wd_ref[j], preferred_element_type=F32)
        tot = dn if tot is None else tot + dn
    acc_sc[...] += tot

    @pl.when(e == n_e - 1)
    def _():
        y = h_ref[0] + mod_row(5) * acc_sc[...]
        if final_norm:
            y = y * lax.rsqrt(jnp.mean(y * y, -1, keepdims=True) + NORM_EPS) * fg_ref[...]
        o_ref[0] = y


def moe(h, n_lat, g, mod, rw, rb, wg, wu, wd, sg, su, sd, fg, final_norm):
    bsz, lo, d = h.shape
    eps = MOE_EXPERTS_PER_STEP
    tm = next(t for t in MOE_BLKS if lo % t == 0)
    kern = functools.partial(_moe_kernel, final_norm=final_norm, n_lat=n_lat, has_ctx=lo > n_lat)
    return pl.pallas_call(
        kern,
        grid=(bsz, lo // tm, N_EXPERTS // eps),
        in_specs=[pl.BlockSpec((1, tm, d), lambda b, i, e: (b, i, 0)),
                  pl.BlockSpec((1, d), lambda b, i, e: (0, 0)),
                  pl.BlockSpec((1, 2, 6, d), lambda b, i, e: (b, 0, 0, 0)),
                  pl.BlockSpec((N_EXPERTS, d), lambda b, i, e: (0, 0)),
                  pl.BlockSpec((N_EXPERTS, 1), lambda b, i, e: (0, 0)),
                  pl.BlockSpec((eps, d, D_EXPERT), lambda b, i, e: (e, 0, 0)),
                  pl.BlockSpec((eps, d, D_EXPERT), lambda b, i, e: (e, 0, 0)),
                  pl.BlockSpec((eps, D_EXPERT, d), lambda b, i, e: (e, 0, 0)),
                  pl.BlockSpec((d, D_EXPERT), lambda b, i, e: (0, 0)),
                  pl.BlockSpec((d, D_EXPERT), lambda b, i, e: (0, 0)),
                  pl.BlockSpec((D_EXPERT, d), lambda b, i, e: (0, 0)),
                  pl.BlockSpec((1, d), lambda b, i, e: (0, 0))],
        out_specs=pl.BlockSpec((1, tm, d), lambda b, i, e: (b, i, 0)),
        out_shape=jax.ShapeDtypeStruct((bsz, lo, d), F32),
        scratch_shapes=[pltpu.VMEM((tm, d), BF16),
                        pltpu.VMEM((tm, 128), F32),
                        pltpu.VMEM((tm, d), F32)],
        compiler_params=_cparams(("parallel", "parallel", "arbitrary")),
        name="moe",
    )(h, g, mod, rw, rb, wg, wu, wd, sg, su, sd, fg)


def _mla_proj_kernel(h_ref, g_ref, mod_ref, win_ref, qn_ref, qup_ref, qsw_ref, kvn_ref, kup_ref, vup_ref, epl_ref,
                     esw_ref, one_ref, ct_ref, st_ref, q_ref, k_ref, v_ref, *, n_lat):
    tm = h_ref.shape[1]
    is_ctx = pl.program_id(1) * tm + lax.broadcasted_iota(jnp.int32, (tm, 1), 0) >= n_lat
    shift = jnp.where(is_ctx, mod_ref[0, 1, 0:1, :], mod_ref[0, 0, 0:1, :])
    scale = jnp.where(is_ctx, mod_ref[0, 1, 1:2, :], mod_ref[0, 0, 1:2, :])
    a = _norm_mod(h_ref[0], g_ref[...], shift, scale)
    p = jnp.dot(a.astype(BF16), win_ref[...], preferred_element_type=F32)
    qd = p[:, :Q_LORA]
    kvd = p[:, Q_LORA:Q_LORA + KV_LORA]
    kpe = p[:, Q_LORA + KV_LORA:]
    qn = qd * lax.rsqrt(jnp.mean(qd * qd, -1, keepdims=True) + NORM_EPS) * qn_ref[...]
    kvn = kvd * lax.rsqrt(jnp.mean(kvd * kvd, -1, keepdims=True) + NORM_EPS) * kvn_ref[...]
    kvb = kvn.astype(BF16)
    ct = jnp.concatenate([ct_ref[...]] * MLA_HEADS, axis=1)
    st = jnp.concatenate([st_ref[...]] * MLA_HEADS, axis=1)
    qb = qn.astype(BF16)
    q = jnp.dot(qb, qup_ref[...], preferred_element_type=F32)
    q_sw = jnp.dot(qb, qsw_ref[...], preferred_element_type=F32)
    q_ref[0] = ((q * ct + q_sw * st) * MLA_SCALE).astype(BF16)
    kpb = kpe.astype(BF16)
    k = (jnp.dot(kvb, kup_ref[...], preferred_element_type=F32)
         + jnp.dot(kpb, epl_ref[...], preferred_element_type=F32))
    k_sw = jnp.dot(kpb, esw_ref[...], preferred_element_type=F32)
    k_ref[0] = (k * ct + k_sw * st).astype(BF16)
    v = jnp.dot(kvb, vup_ref[...], preferred_element_type=F32) + one_ref[...]
    v_ref[0] = v.astype(BF16)


def mla_proj(h, n_lat, g, mod, win, qn, qup, qsw, kvn, kup, vup, epl, esw, one, ct, st):
    bsz, lt, d = h.shape
    hp = MLA_HEADS * HEAD_PAD
    full = lambda shape: pl.BlockSpec(shape, lambda b, i: (0,) * len(shape))
    blk = next(t for t in MLA_BLKS if lt % t == 0)
    tab = pl.BlockSpec((blk, HEAD_PAD), lambda b, i: (i, 0))
    out = pl.BlockSpec((1, blk, hp), lambda b, i: (b, i, 0))
    return pl.pallas_call(
        functools.partial(_mla_proj_kernel, n_lat=n_lat),
        grid=(bsz, lt // blk),
        in_specs=[pl.BlockSpec((1, blk, d), lambda b, i: (b, i, 0)),
                  full((1, d)), pl.BlockSpec((1, 2, 6, d), lambda b, i: (b, 0, 0, 0)), full((d, MLA_IN)),
                  full((1, Q_LORA)), full((Q_LORA, hp)),
                  full((Q_LORA, hp)), full((1, KV_LORA)), full((KV_LORA, hp)), full((KV_LORA, hp)),
                  full((QK_ROPE, hp)), full((QK_ROPE, hp)), full((1, hp)), tab, tab],
        out_specs=[out, out, out],
        out_shape=[jax.ShapeDtypeStruct((bsz, lt, hp), BF16)] * 3,
        compiler_params=_cparams(("parallel", "parallel")),
        name="mla_proj",
    )(h, g, mod, win, qn, qup, qsw, kvn, kup, vup, epl, esw, one, ct, st)


def _attn_kernel(q_ref, k_ref, v_ref, o_ref, *, tk, nk):
    tq = q_ref.shape[1]
    nsub = tq // ATT_SUB
    qs = [q_ref[0, i * ATT_SUB:(i + 1) * ATT_SUB, :] for i in range(nsub)]

    def scores(j):
        kc = k_ref[0, j * tk:(j + 1) * tk, :]
        return [lax.dot_general(q, kc, (((1,), (1,)), ((), ())), preferred_element_type=F32) for q in qs]

    ms = [jnp.full((ATT_SUB, 1), -1e30, F32) for _ in range(nsub)]
    accs = [jnp.zeros((ATT_SUB, HEAD_PAD), F32) for _ in range(nsub)]
    ss = scores(0)
    for j in range(nk):
        ss_next = scores(j + 1) if j + 1 < nk else None
        vc = v_ref[0, j * tk:(j + 1) * tk, :]
        for i in range(nsub):
            m_new = jnp.maximum(ms[i], jnp.max(ss[i], axis=1, keepdims=True))
            alpha = jnp.exp(ms[i] - m_new)
            p = jnp.exp(ss[i] - m_new).astype(BF16)
            accs[i] = accs[i] * alpha + jnp.dot(p, vc, preferred_element_type=F32)
            ms[i] = m_new
        ss = ss_next
    for i, acc in enumerate(accs):
        o_ref[0, i * ATT_SUB:(i + 1) * ATT_SUB, :] = (acc / acc[:, V_DIM:V_DIM + 1]).astype(BF16)


def attention(q, k, v, lq):
    bsz, lt, hp = q.shape
    tq = next(t for t in ATT_TQS if lq % t == 0)
    tk = next(t for t in ATT_TKS if lt % t == 0)
    kern = functools.partial(_attn_kernel, tk=tk, nk=lt // tk)
    return pl.pallas_call(
        kern,
        grid=(bsz, MLA_HEADS, lq // tq),
        in_specs=[pl.BlockSpec((1, tq, HEAD_PAD), lambda b, h, i: (b, i, h)),
                  pl.BlockSpec((1, lt, HEAD_PAD), lambda b, h, i: (b, 0, h)),
                  pl.BlockSpec((1, lt, HEAD_PAD), lambda b, h, i: (b, 0, h))],
        out_specs=pl.BlockSpec((1, tq, HEAD_PAD), lambda b, h, i: (b, i, h)),
        out_shape=jax.ShapeDtypeStruct((bsz, lq, hp), BF16),
        compiler_params=_cparams(("parallel", "parallel", "parallel")),
        name="attention",
    )(q, k, v)


def _rope_tables(n_lat, n_ctx):
    rows = n_lat // GRID_W
    row = jnp.repeat(jnp.arange(rows, dtype=F32), GRID_W)
    col = jnp.tile(jnp.arange(GRID_W, dtype=F32), rows)
    inv_freq = ROPE_BASE ** (-jnp.arange(0, ROPE_AXIS_DIMS, 2, dtype=F32) / ROPE_AXIS_DIMS)
    ang = jnp.concatenate([row[:, None] * inv_freq, col[:, None] * inv_freq], -1)
    cos = jnp.concatenate([jnp.cos(ang), jnp.ones((n_ctx, ROPE_AXIS_DIMS), F32)], 0)
    sin = jnp.concatenate([jnp.sin(ang), jnp.zeros((n_ctx, ROPE_AXIS_DIMS), F32)], 0)
    lt = n_lat + n_ctx
    one = jnp.ones((lt, QK_NOPE), F32)
    z32 = jnp.zeros((lt, HEAD_PAD - QK_NOPE - QK_ROPE), F32)
    z64 = jnp.zeros((lt, QK_NOPE), F32)
    ct = jnp.concatenate([one, cos, cos, z32], 1)
    st = jnp.concatenate([z64, -sin, sin, z32], 1)
    return ct, st


def _pad_heads(w, width, offset=0):
    k = w.shape[0]
    w = w.reshape(k, MLA_HEADS, width)
    w = jnp.pad(w, ((0, 0), (0, 0), (offset, HEAD_PAD - width - offset)))
    return w.reshape(k, MLA_HEADS * HEAD_PAD)


def mla_layer(h, n_ctx, g, mod, w_in, q_norm, q_up, kv_norm, kv_up, w_out):
    bsz, lt, d = h.shape
    n_lat = lt - n_ctx
    ct, st = _rope_tables(n_lat, n_ctx)
    half = ROPE_AXIS_DIMS
    swap = lambda t: jnp.concatenate([t[..., half:], t[..., :half]], -1)
    qup = _pad_heads(q_up, QK_NOPE + QK_ROPE).astype(BF16)
    q_pe = q_up.reshape(Q_LORA, MLA_HEADS, QK_NOPE + QK_ROPE)[:, :, QK_NOPE:]
    qsw = _pad_heads(swap(q_pe).reshape(Q_LORA, -1), QK_ROPE, QK_NOPE).astype(BF16)
    kvu = kv_up.reshape(KV_LORA, MLA_HEADS, QK_NOPE + V_DIM)
    kup = _pad_heads(kvu[:, :, :QK_NOPE].reshape(KV_LORA, -1), QK_NOPE).astype(BF16)
    vup = _pad_heads(kvu[:, :, QK_NOPE:].reshape(KV_LORA, -1), V_DIM).astype(BF16)
    eye = jnp.eye(QK_ROPE, dtype=F32)
    epl = _pad_heads(jnp.tile(eye, (1, MLA_HEADS)), QK_ROPE, QK_NOPE).astype(BF16)
    esw = _pad_heads(jnp.tile(swap(eye), (1, MLA_HEADS)), QK_ROPE, QK_NOPE).astype(BF16)
    one = _pad_heads(jnp.ones((1, MLA_HEADS), F32), 1, V_DIM)
    q, k, v = mla_proj(h, n_lat, g, mod, w_in.astype(BF16), q_norm[None], qup, qsw, kv_norm[None], kup, vup, epl,
                       esw, one, ct, st)
    o = attention(q, k, v, n_lat)
    wo = w_out.reshape(MLA_HEADS, V_DIM, d)
    wo = jnp.pad(wo, ((0, 0), (0, HEAD_PAD - V_DIM), (0, 0))).reshape(MLA_HEADS * HEAD_PAD, d).astype(BF16)
    return outproj(o, wo, h, n_lat, mod)


def _stream_rows(x_ref, c_ref, mod_ref, b, is_ctx, ks):
    rows, d = x_ref.shape[1:]
    lat_rows = jnp.where(is_ctx, 0, rows)
    hb = jnp.where(lax.broadcasted_iota(jnp.int32, (rows, d), 0) < lat_rows, x_ref[b], c_ref[b])
    lat1 = lax.broadcasted_iota(jnp.int32, (1, d), 0) < lat_rows
    ms = [jnp.where(lat1, mod_ref[b, 0, k:k + 1, :], mod_ref[b, 1, k:k + 1, :]) for k in ks]
    return hb, ms


def _inproj2_kernel(x_ref, c_ref, g_ref, mod_ref, w_ref, p_ref, u_ref, *, nlb):
    is_ctx = pl.program_id(0) >= nlb
    bsz, rows, _ = x_ref.shape
    for b in range(bsz):
        hb, (shift, scale) = _stream_rows(x_ref, c_ref, mod_ref, b, is_ctx, (0, 1))
        a = _norm_mod(hb, g_ref[...], shift, scale)
        o = jnp.dot(a.astype(BF16), w_ref[...], preferred_element_type=F32)
        p_ref[b] = o[:, :RWKV_IN]
        for j in range(S5_TILES):
            u_ref[j, pl.ds(b, rows, stride=bsz), :] = o[:, RWKV_IN + j * LANES:RWKV_IN + (j + 1) * LANES]


def _stream_specs(x, ctx):
    bsz, n_lat, d = x.shape
    nlb = n_lat // SEQ_BLK
    return nlb, [pl.BlockSpec((bsz, SEQ_BLK, d), lambda i: (0, jnp.minimum(i, nlb - 1), 0)),
                 pl.BlockSpec((bsz, SEQ_BLK, d), lambda i: (0, jnp.maximum(i - nlb, 0), 0))]


def inproj2(x, ctx, g, mod, w):
    bsz, n_lat, d = x.shape
    lt = n_lat + ctx.shape[1]
    nlb, sspecs = _stream_specs(x, ctx)
    full = lambda a: pl.BlockSpec(a.shape, lambda i: (0,) * a.ndim)
    return pl.pallas_call(
        functools.partial(_inproj2_kernel, nlb=nlb),
        grid=(lt // SEQ_BLK,),
        in_specs=sspecs + [full(g), full(mod), full(w)],
        out_specs=[pl.BlockSpec((bsz, SEQ_BLK, RWKV_IN), lambda i: (0, i, 0)),
                   pl.BlockSpec((S5_TILES, bsz * SEQ_BLK, LANES), lambda i: (0, i, 0))],
        out_shape=[jax.ShapeDtypeStruct((bsz, lt, RWKV_IN), F32),
                   jax.ShapeDtypeStruct((S5_TILES, lt * bsz, LANES), F32)],
        compiler_params=_cparams(("parallel",)),
        name="inproj2",
    )(x, ctx, g, mod, w)


def _tri_inverse_all(mats, idx_r, idx_c):
    c = mats[0].shape[0]
    eye = (idx_r == idx_c).astype(F32)
    blk8 = (idx_r >> 3) == (idx_c >> 3)
    ns = [jnp.where(blk8, -a, 0.0) for a in mats]
    ts = [eye + n for n in ns]
    n2 = [_dot(n, n) for n in ns]
    ts = [t + _dot(m, t) for m, t in zip(n2, ts)]
    n4 = [_dot(m, m) for m in n2]
    ts = [t + _dot(m, t) for m, t in zip(n4, ts)]
    sh = 3
    while (1 << sh) < c:
        off = jnp.logical_and((idx_r >> (sh + 1)) == (idx_c >> (sh + 1)), (idx_r >> sh) != (idx_c >> sh))
        ms = [_dot(jnp.where(off, a, 0.0), t) for a, t in zip(mats, ts)]
        ts = [t - _dot(t, m) for m, t in zip(ms, ts)]
        sh += 1
    return ts


def _rwkv_conv(pc, prev_row, next_row, cw):
    c = pc.shape[0]
    row = lax.broadcasted_iota(jnp.int32, (c, 1), 0)
    xm1 = jnp.where(row == 0, prev_row, pltpu.roll(pc, 1, 0))
    xp1 = jnp.where(row == c - 1, next_row, pltpu.roll(pc, c - 1, 0))
    return cw[0:1] * xm1 + cw[1:2] * pc + cw[2:3] * xp1


def _rwkv_chunk_inputs(x, z, bi, w0, wup, a0, aup, kkv, kav):
    c = x.shape[0]
    n = RWKV_HEAD_DIM
    r = x[:, 0:RWKV_WIDTH]
    k = x[:, RWKV_WIDTH:2 * RWKV_WIDTH]
    v = x[:, 2 * RWKV_WIDTH:3 * RWKV_WIDTH]
    wd = x[:, 3 * RWKV_WIDTH:3 * RWKV_WIDTH + W_LORA]
    ad = x[:, 3 * RWKV_WIDTH + W_LORA:3 * RWKV_WIDTH + W_LORA + A_LORA]
    lw = -DECAY_SCALE * _sigmoid(w0 + _dot(jnp.tanh(wd), wup, True))
    a = _sigmoid(a0 + _dot(ad, aup, True))
    idx_r = lax.broadcasted_iota(jnp.int32, (c, c), 0)
    idx_c = lax.broadcasted_iota(jnp.int32, (c, c), 1)
    if z == 0:
        incl = idx_c <= idx_r
        strict = idx_c < idx_r
        last = c - 1
    else:
        incl = idx_c >= idx_r
        strict = idx_c > idx_r
        last = 0
    cum = _dot(incl.astype(F32), lw, True)
    tot = cum[last:last + 1]
    e1 = jnp.exp(cum)
    e2 = jnp.exp(cum - lw)
    e3 = jnp.exp(-cum)
    e4 = jnp.exp(tot - cum)
    etot = jnp.exp(tot)
    kkf = k * kkv
    heads = []
    for h in range(RWKV_HEADS):
        sl = slice(h * n, (h + 1) * n)
        kk = kkf[:, sl]
        kk = kk * lax.rsqrt(jnp.sum(kk * kk, -1, keepdims=True) + 1e-12)
        ah = a[:, sl]
        bh = ah * kk
        kd = k[:, sl] * (1.0 + (ah - 1.0) * kav[:, sl])
        heads.append(dict(
            xq=jnp.concatenate([kk * e2[:, sl], r[:, sl] * e1[:, sl]], 0),
            yk=jnp.concatenate([kd * e3[:, sl], bh * e3[:, sl]], 0),
            ke=jnp.concatenate([kd * e4[:, sl], bh * e4[:, sl]], 0),
            v=v[:, sl], etot=etot[:, sl], incl=incl, strict=strict, state=(z, bi, h)))
    return heads


def _rwkv_solve_all(heads, s_sc):
    c = heads[0]['v'].shape[0]
    idx_r = lax.broadcasted_iota(jnp.int32, (c, c), 0)
    idx_c = lax.broadcasted_iota(jnp.int32, (c, c), 1)
    gs = [_dot_nt(hd['xq'], hd['yk']) for hd in heads]
    ss = [s_sc[hd['state']] for hd in heads]
    xs = [_dot_nt(hd['xq'], s) for hd, s in zip(heads, ss)]
    avs = [_dot(jnp.where(hd['strict'], g[:c, :c], 0.0), hd['v']) for hd, g in zip(heads, gs)]
    ts = _tri_inverse_all([jnp.where(hd['strict'], g[:c, c:], 0.0) for hd, g in zip(heads, gs)], idx_r, idx_c)
    sas = [_dot(t, x[:c] + av) for t, x, av in zip(ts, xs, avs)]
    outs = []
    for hd, g, x, sa, s in zip(heads, gs, xs, sas, ss):
        a_r = jnp.concatenate([jnp.where(hd['incl'], g[c:, :c], 0.0), jnp.where(hd['incl'], -g[c:, c:], 0.0)], 1)
        outs.append(x[c:] + _dot(a_r, jnp.concatenate([hd['v'], sa], 0)))
        s_sc[hd['state']] = s * hd['etot'] + _dot_tn(jnp.concatenate([hd['v'], -sa], 0), hd['ke'])
    return outs


def _rwkv_kernel(pa_ref, pap_ref, pan_ref, pb_ref, pbp_ref, pbn_ref, cw_ref, w0_ref, wup_ref, a0_ref, aup_ref,
                 gup_ref, kk_ref, ka_ref, rk_ref, y0_ref, y1_ref, bonus_ref, g_ref, s_sc, *, nl, nc):
    s = pl.program_id(1)

    @pl.when(s == 0)
    def _():
        s_sc[...] = jnp.zeros_like(s_sc)

    ca = (s + nl) % nc
    cb = nc - 1 - s
    cw = cw_ref[...]

    def load(p_ref, pp_ref, pn_ref, cidx, bi):
        first = jnp.logical_or(cidx == 0, cidx == nl)
        lastc = jnp.logical_or(cidx == nl - 1, cidx == nc - 1)
        prev_row = jnp.where(first, 0.0, pp_ref[bi, 7:8, :])
        next_row = jnp.where(lastc, 0.0, pn_ref[bi, 0:1, :])
        return _rwkv_conv(p_ref[bi], prev_row, next_row, cw)

    nb = pa_ref.shape[0]
    heads = []
    xas = []
    for bi in range(nb):
        xa = load(pa_ref, pap_ref, pan_ref, ca, bi)
        xb = load(pb_ref, pbp_ref, pbn_ref, cb, bi)
        xas.append(xa)
        heads += _rwkv_chunk_inputs(xa, 0, bi, w0_ref[0:1], wup_ref[0], a0_ref[0:1], aup_ref[0], kk_ref[...],
                                    ka_ref[...])
        heads += _rwkv_chunk_inputs(xb, 1, bi, w0_ref[1:2], wup_ref[1], a0_ref[1:2], aup_ref[1], kk_ref[...],
                                    ka_ref[...])
    outs = _rwkv_solve_all(heads, s_sc)
    nh = RWKV_HEADS
    for bi in range(nb):
        y0_ref[bi] = jnp.concatenate(outs[2 * nh * bi:2 * nh * bi + nh], 1).astype(BF16)
        y1_ref[bi] = jnp.concatenate(outs[2 * nh * bi + nh:2 * nh * (bi + 1)], 1).astype(BF16)
        xa = xas[bi]
        gd = xa[:, 3 * RWKV_WIDTH + W_LORA + A_LORA:]
        g_ref[bi] = _dot(_sigmoid(gd), gup_ref[...], True).astype(BF16)
        r = xa[:, 0:RWKV_WIDTH]
        k = xa[:, RWKV_WIDTH:2 * RWKV_WIDTH]
        v = xa[:, 2 * RWKV_WIDTH:3 * RWKV_WIDTH]
        rkr = r * k * rk_ref[...]
        bon = []
        for h in range(nh):
            sl = slice(h * RWKV_HEAD_DIM, (h + 1) * RWKV_HEAD_DIM)
            bon.append(jnp.sum(rkr[:, sl], -1, keepdims=True) * v[:, sl])
        bonus_ref[bi] = jnp.concatenate(bon, 1).astype(BF16)


def rwkv_scan(p, n_lat, cw, w0, wup, a0, aup, gup, kkv, kav, rk):
    bsz, lt, _ = p.shape
    nl = n_lat // CHUNK
    nc = lt // CHUNK
    nb8 = lt // 8
    ca = lambda s: (s + nl) % nc
    cb = lambda s: nc - 1 - s
    nb = RWKV_NB if bsz % RWKV_NB == 0 else 1
    pspec = lambda cf: pl.BlockSpec((nb, CHUNK, RWKV_IN), lambda b, s: (b, cf(s), 0))
    prev = lambda cf: pl.BlockSpec((nb, 8, RWKV_IN), lambda b, s: (b, jnp.maximum(cf(s) * (CHUNK // 8) - 1, 0), 0))
    nxt = lambda cf: pl.BlockSpec((nb, 8, RWKV_IN),
                                  lambda b, s: (b, jnp.minimum((cf(s) + 1) * (CHUNK // 8), nb8 - 1), 0))
    full = lambda a: pl.BlockSpec(a.shape, lambda b, s: (0,) * a.ndim)
    ospec = lambda cf: pl.BlockSpec((nb, CHUNK, RWKV_WIDTH), lambda b, s: (b, cf(s), 0))
    oshape = jax.ShapeDtypeStruct((bsz, lt, RWKV_WIDTH), BF16)
    kern = functools.partial(_rwkv_kernel, nl=nl, nc=nc)
    params = (cw, w0, wup, a0, aup, gup, kkv, kav, rk)
    return pl.pallas_call(
        kern,
        grid=(bsz // nb, nc),
        in_specs=[pspec(ca), prev(ca), nxt(ca), pspec(cb), prev(cb), nxt(cb)] + [full(a) for a in params],
        out_specs=[ospec(ca), ospec(cb), ospec(ca), ospec(ca)],
        out_shape=[oshape] * 4,
        scratch_shapes=[pltpu.VMEM((2, nb, RWKV_HEADS, RWKV_HEAD_DIM, RWKV_HEAD_DIM), F32)],
        compiler_params=_cparams(("parallel", "arbitrary")),
        name="rwkv_scan",
    )(p, p, p, p, p, p, *params)


def _s5_kernel(uf_ref, ur_ref, lam_ref, bm_ref, cm_ref, yf_ref, yr_ref, bf_sc, br_sc, h_sc):
    s = pl.program_id(0)

    @pl.when(s == 0)
    def _():
        h_sc[...] = jnp.zeros_like(h_sc)

    rows = bf_sc.shape[0]
    win = S5_WIDTH // 2
    ubs = [jnp.concatenate([u_ref[j] for j in range(S5_TILES)], 1).astype(BF16) for u_ref in (uf_ref, ur_ref)]
    for k in range(2):
        for z, sc in enumerate((bf_sc, br_sc)):
            sc[:, 2 * S5_HALF * k:2 * S5_HALF * (k + 1)] = jnp.dot(
                ubs[z][:, win * k:win * (k + 1)], bm_ref[z, k], preferred_element_type=F32)
    ntile = rows // 8
    lo = lax.broadcasted_iota(jnp.int32, (8, 1), 0) < 4

    for k in range(2):
        cre = slice(2 * S5_HALF * k, 2 * S5_HALF * k + S5_HALF)
        cim = slice(2 * S5_HALF * k + S5_HALF, 2 * S5_HALF * (k + 1))
        lr = lam_ref[0, :, S5_HALF * k:S5_HALF * (k + 1)]
        li = lam_ref[1, :, S5_HALF * k:S5_HALF * (k + 1)]
        hr, hi = h_sc[2 * k], h_sc[2 * k + 1]
        for jj in range(ntile):
            sf = slice(jj * 8, jj * 8 + 8)
            sr = slice((ntile - 1 - jj) * 8, (ntile - jj) * 8)
            tfr, tfi = bf_sc[sf, cre], bf_sc[sf, cim]
            trr, tri = br_sc[sr, cre], br_sc[sr, cim]
            xar, xai = jnp.where(lo, tfr, trr), jnp.where(lo, tfi, tri)
            xbr = pltpu.roll(jnp.where(lo, trr, tfr), 4, 0)
            xbi = pltpu.roll(jnp.where(lo, tri, tfi), 4, 0)
            ar, ai = lr * hr - li * hi + xar, lr * hi + li * hr + xai
            hr, hi = lr * ar - li * ai + xbr, lr * ai + li * ar + xbi
            rbr, rbi = pltpu.roll(hr, 4, 0), pltpu.roll(hi, 4, 0)
            bf_sc[sf, cre] = jnp.where(lo, ar, rbr)
            bf_sc[sf, cim] = jnp.where(lo, ai, rbi)
            br_sc[sr, cre] = jnp.where(lo, rbr, ar)
            br_sc[sr, cim] = jnp.where(lo, rbi, ai)
        h_sc[2 * k] = hr
        h_sc[2 * k + 1] = hi
        for z, (sc, y_ref) in enumerate(((bf_sc, yf_ref), (br_sc, yr_ref))):
            y = jnp.dot(sc[:, 2 * S5_HALF * k:2 * S5_HALF * (k + 1)].astype(BF16), cm_ref[z, k],
                        preferred_element_type=F32)
            for j in range(win // LANES):
                y_ref[k * (win // LANES) + j] = y[:, j * LANES:(j + 1) * LANES]


def s5_scan(u, n_lat, bsz, lam, bmat, cmat):
    assert bsz == 4
    rows_total = u.shape[1]
    lt = rows_total // bsz
    nl = n_lat // CHUNK
    nc = lt // CHUNK
    rb = CHUNK * bsz
    cf = lambda s: (0, (s + nl) % nc, 0)
    cr = lambda s: (0, nc - 1 - s, 0)
    full = lambda a: pl.BlockSpec(a.shape, lambda s: (0,) * a.ndim)
    oshape = jax.ShapeDtypeStruct(u.shape, F32)
    blk = (S5_TILES, rb, LANES)
    return pl.pallas_call(
        _s5_kernel,
        grid=(nc,),
        in_specs=[pl.BlockSpec(blk, cf), pl.BlockSpec(blk, cr), full(lam), full(bmat), full(cmat)],
        out_specs=[pl.BlockSpec(blk, cf), pl.BlockSpec(blk, cr)],
        out_shape=[oshape, oshape],
        scratch_shapes=[pltpu.VMEM((rb, 2 * S5_N), F32), pltpu.VMEM((rb, 2 * S5_N), F32),
                        pltpu.VMEM((4, 8, S5_HALF), F32)],
        compiler_params=_cparams(("arbitrary",)),
        name="s5_scan",
    )(u, u, lam, bmat, cmat)


def _s5_params(lam_re, lam_im, log_dt, b_re, b_im, c_re, c_im):
    dt = jnp.exp(log_dt.astype(F32))[..., None]
    mag = jnp.exp(lam_re * dt)
    lbr = mag * jnp.cos(lam_im * dt)
    lbi = mag * jnp.sin(lam_im * dt)
    den = lam_re * lam_re + lam_im * lam_im
    nr, ni = lbr - 1.0, lbi
    fr = (nr * lam_re + ni * lam_im) / den
    fi = (ni * lam_re - nr * lam_im) / den
    bbr = fr[..., None] * b_re - fi[..., None] * b_im
    bbi = fr[..., None] * b_im + fi[..., None] * b_re
    gh = S5_GROUPS // 2
    eye = jnp.eye(gh, dtype=F32)

    def bd_in(w):
        return jnp.einsum('gpc,gh->gchp', w, eye).reshape(S5_WIDTH // 2, S5_HALF)

    def bd_out(w):
        return jnp.einsum('gcp,gh->gphc', w, eye).reshape(S5_HALF, S5_WIDTH // 2)

    halves = [slice(0, gh), slice(gh, S5_GROUPS)]
    bmat = jnp.stack([jnp.stack([jnp.concatenate([bd_in(bbr[z, hs]), bd_in(bbi[z, hs])], 1) for hs in halves])
                      for z in range(2)])
    cmat = jnp.stack([jnp.stack([jnp.concatenate([bd_out(c_re[z, hs]), -bd_out(c_im[z, hs])], 0) for hs in halves])
                      for z in range(2)])
    rows = lambda t: jnp.concatenate([jnp.tile(t[0].reshape(1, S5_N), (4, 1)),
                                      jnp.tile(t[1].reshape(1, S5_N), (4, 1))], 0)
    lam = jnp.stack([rows(lbr), rows(lbi)])
    return lam, bmat.astype(BF16), cmat.astype(BF16)


def _gelu(x):
    return 0.5 * x * (1.0 + jnp.tanh(math.sqrt(2.0 / math.pi) * (x + 0.044715 * x * x * x)))


def _merge_kernel(y0_ref, y1_ref, bonus_ref, g_ref, s0_ref, s1_ref, u_ref, x_ref, c_ref, mod_ref, lnw_ref, lnb_ref,
                  d_ref, gluw_ref, glub_ref, wout_ref, o_ref, *, nlb):
    is_ctx = pl.program_id(0) >= nlb
    bsz, rows, _ = x_ref.shape
    n = RWKV_HEAD_DIM
    for b in range(bsz):
        y = y0_ref[b].astype(F32) + y1_ref[b].astype(F32)
        parts = []
        for h in range(RWKV_HEADS):
            yh = y[:, h * n:(h + 1) * n]
            mu = jnp.mean(yh, -1, keepdims=True)
            dlt = yh - mu
            var = jnp.mean(dlt * dlt, -1, keepdims=True)
            parts.append(dlt * lax.rsqrt(var + GN_EPS))
        yn = jnp.concatenate(parts, 1) * lnw_ref[...] + lnb_ref[...]
        rw = (yn + bonus_ref[b].astype(F32)) * g_ref[b].astype(F32)
        tm = pl.ds(b, rows, stride=bsz)
        seq = lambda ref: jnp.concatenate([ref[j, tm, :] for j in range(S5_TILES)], 1)
        ys = seq(s0_ref) + seq(s1_ref) + d_ref[...] * seq(u_ref)
        zz = _gelu(ys)
        gate = _sigmoid(_dot(zz, gluw_ref[...]) + glub_ref[...])
        cat = jnp.concatenate([rw, zz * gate], 1).astype(BF16)
        hb, (gate_res,) = _stream_rows(x_ref, c_ref, mod_ref, b, is_ctx, (2,))
        o_ref[b] = hb + gate_res * jnp.dot(cat, wout_ref[...], preferred_element_type=F32)


def merge(y0, y1, bonus, g, s0, s1, u, x, ctx, mod, lnw, lnb, dvec, gluw, glub, wout):
    bsz, n_lat, d = x.shape
    lt = n_lat + ctx.shape[1]
    nlb, sspecs = _stream_specs(x, ctx)
    rspec = pl.BlockSpec((bsz, SEQ_BLK, RWKV_WIDTH), lambda i: (0, i, 0))
    tspec = pl.BlockSpec((S5_TILES, bsz * SEQ_BLK, LANES), lambda i: (0, i, 0))
    full = lambda a: pl.BlockSpec(a.shape, lambda i: (0,) * a.ndim)
    params = (mod, lnw, lnb, dvec, gluw, glub, wout)
    return pl.pallas_call(
        functools.partial(_merge_kernel, nlb=nlb),
        grid=(lt // SEQ_BLK,),
        in_specs=[rspec, rspec, rspec, rspec, tspec, tspec, tspec] + sspecs + [full(a) for a in params],
        out_specs=pl.BlockSpec((bsz, SEQ_BLK, d), lambda i: (0, i, 0)),
        out_shape=jax.ShapeDtypeStruct((bsz, lt, d), F32),
        compiler_params=_cparams(("parallel",)),
        name="merge",
    )(y0, y1, bonus, g, s0, s1, u, x, ctx, *params)


def hybrid_layer(x, ctx, g1, mod, w_in, w_out, conv_w, w0, w_up, a0, a_up, g_up, k_k, k_a, r_k, ln_w, ln_b,
                 lam_re, lam_im, log_dt, b_re, b_im, c_re, c_im, dvec, glu_w, glu_b):
    bsz, n_lat, _ = x.shape
    p, u = inproj2(x, ctx, g1, mod, w_in.astype(BF16))
    y0, y1, bonus, g = rwkv_scan(p, n_lat, conv_w, w0, w_up, a0, a_up, g_up, k_k[None], k_a[None],
                                 r_k.reshape(1, RWKV_WIDTH))
    lam, bmat, cmat = _s5_params(lam_re, lam_im, log_dt, b_re, b_im, c_re, c_im)
    s0, s1 = s5_scan(u, n_lat, bsz, lam, bmat, cmat)
    gluw = jnp.einsum('gce,gh->gche', glu_w, jnp.eye(S5_GROUPS, dtype=F32)).reshape(S5_WIDTH, S5_WIDTH)
    return merge(y0, y1, bonus, g, s0, s1, u, x, ctx, mod, ln_w[None], ln_b[None], dvec[None],
                 gluw.astype(BF16), glu_b[None], w_out.astype(BF16))


def kernel(x, c, ctx, c_ctx, mod_w, mod_b, norm1_g, norm2_g, final_g, hy_w_in, hy_w_out, rk_conv, rk_w0, rk_w_up, rk_a0, rk_a_up, rk_g_up, rk_k_k, rk_k_a, rk_r_k, rk_ln_w, rk_ln_b, s5_lam_re, s5_lam_im, s5_log_dt, s5_b_re, s5_b_im, s5_c_re, s5_c_im, s5_d, s5_glu_w, s5_glu_b, mla_w_in, mla_q_norm, mla_q_up, mla_kv_norm, mla_kv_up, mla_w_out, router_w, router_b, ex_gate, ex_up, ex_down, sh_gate, sh_up, sh_down):
    bsz, n_lat, d = x.shape
    n_ctx = ctx.shape[1]
    depth = mod_w.shape[0]
    assert n_ctx == ROW_BLK and depth == 2 and bsz <= 7
    cc = jnp.concatenate([c, c_ctx[None], jnp.zeros((8 - bsz - 1, d), F32)], 0)
    rb = router_b[:, None].astype(F32)
    rw_t = router_w.T
    fg = final_g[None]
    for layer in range(depth):
        last = layer == depth - 1
        i = layer // 2
        mv = adaln(cc, mod_w[layer], mod_b[layer][None])
        m_l = mv[:bsz].reshape(bsz, 1, 6, d)
        m_c = jnp.broadcast_to(mv[bsz].reshape(1, 1, 6, d), (bsz, 1, 6, d))
        mod = jnp.concatenate([m_l, m_c], 1)
        g1 = norm1_g[layer][None]
        g2 = norm2_g[layer][None]
        if layer % 2 == 0:
            h = hybrid_layer(x, ctx, g1, mod, hy_w_in[i], hy_w_out[i], rk_conv[i], rk_w0[i], rk_w_up[i], rk_a0[i],
                             rk_a_up[i], rk_g_up[i], rk_k_k[i], rk_k_a[i], rk_r_k[i], rk_ln_w[i], rk_ln_b[i],
                             s5_lam_re[i], s5_lam_im[i], s5_log_dt[i], s5_b_re[i], s5_b_im[i], s5_c_re[i],
                             s5_c_im[i], s5_d[i], s5_glu_w[i], s5_glu_b[i])
        else:
            h = mla_layer(h, n_ctx, g1, mod, mla_w_in[i], mla_q_norm[i], mla_q_up[i], mla_kv_norm[i],
                          mla_kv_up[i], mla_w_out[i])
        bf = lambda t: t[layer].astype(BF16)
        h = moe(h, n_lat, g2, mod, rw_t, rb, bf(ex_gate), bf(ex_up), bf(ex_down), bf(sh_gate), bf(sh_up),
                bf(sh_down), fg, last)
    return h
```
